```python
import jax, jax.numpy as jnp
from jax import lax
import numpy as np

D_MODEL = 2048
BATCH = 2
SEQ = 4096
DEPTH = 2

MEM_LEN = 256
EPS = 1e-6
D_FF = 5632
MLA_HEADS = 8
MLA_Q_RANK = 512
MLA_KV_RANK = 512
MLA_NOPE = 128
MLA_ROPE = 64
MLA_V = 128
MLA_WIDTH = MLA_HEADS * MLA_V
ROPE_THETA = 10000.0
Q_BLOCK = 128
GLA_HEADS = 4
GLA_DK = 64
GLA_DV = 128
GLA_WIDTH = GLA_HEADS * GLA_DV
GLA_GATE_RANK = 16
GLA_TAU = 16.0
GLA_CHUNK = 64
CONV_DIM = 512
CONV_WIDTH = 3
X_HEADS = 4
X_HEAD_DIM = D_MODEL // X_HEADS
N_BRANCH = 3
IN_SPLITS = (MLA_Q_RANK, MLA_KV_RANK, MLA_ROPE,
             GLA_HEADS * GLA_DK, GLA_HEADS * GLA_DK, GLA_WIDTH, GLA_WIDTH, GLA_GATE_RANK,
             3 * CONV_DIM, N_BRANCH * D_MODEL)
D_IN_PROJ = sum(IN_SPLITS)

kernel_name = "hybrid_mla_gla_shortconv_gated_macaron"


def _split(z, sizes):
    outs, off = [], 0
    for n in sizes:
        outs.append(z[..., off:off + n])
        off += n
    return outs


def rmsnorm(x, g):
    xf = x.astype(jnp.float32)
    y = xf * lax.rsqrt(jnp.mean(xf * xf, axis=-1, keepdims=True) + EPS)
    return (y * g.astype(jnp.float32)).astype(x.dtype)


def swiglu(x, w_in, w_out):
    gate, up = jnp.split(x @ w_in, 2, axis=-1)
    return (jax.nn.silu(gate) * up) @ w_out


def rope_tables(seq):
    inv_freq = 1.0 / (ROPE_THETA ** (jnp.arange(0, MLA_ROPE, 2, dtype=jnp.float32) / MLA_ROPE))
    ang = jnp.arange(seq, dtype=jnp.float32)[:, None] * inv_freq[None, :]
    return jnp.cos(ang), jnp.sin(ang)


def apply_rope(x, cos, sin):
    x1, x2 = jnp.split(x, 2, axis=-1)
    cos = cos.astype(x.dtype)
    sin = sin.astype(x.dtype)
    return jnp.concatenate([x1 * cos - x2 * sin, x2 * cos + x1 * sin], axis=-1)


def mla(c_q, c_kv, k_rope, q_norm_g, kv_norm_g, w_uq, w_ukv, cos, sin):
    B, S, _ = c_q.shape
    q = (rmsnorm(c_q, q_norm_g) @ w_uq).reshape(B, S, MLA_HEADS, MLA_NOPE + MLA_ROPE)
    q_nope, q_pe = q[..., :MLA_NOPE], q[..., MLA_NOPE:]
    kv = (rmsnorm(c_kv, kv_norm_g) @ w_ukv).reshape(B, S, MLA_HEADS, MLA_NOPE + MLA_V)
    k_nope, v = kv[..., :MLA_NOPE], kv[..., MLA_NOPE:]
    q_pe = apply_rope(q_pe, cos[:, None, :], sin[:, None, :])
    k_pe = apply_rope(k_rope, cos, sin)
    scale = (MLA_NOPE + MLA_ROPE) ** -0.5
    nb = S // Q_BLOCK
    qn_b = q_nope.reshape(B, nb, Q_BLOCK, MLA_HEADS, MLA_NOPE).transpose(1, 0, 2, 3, 4)
    qp_b = q_pe.reshape(B, nb, Q_BLOCK, MLA_HEADS, MLA_ROPE).transpose(1, 0, 2, 3, 4)
    k_pos = jnp.arange(S)

    def block(args):
        i, qn, qp = args
        s = (jnp.einsum('bqhd,bkhd->bhqk', qn, k_nope)
             + jnp.einsum('bqhr,bkr->bhqk', qp, k_pe)).astype(jnp.float32) * scale
        q_pos = i * Q_BLOCK + jnp.arange(Q_BLOCK)
        s = jnp.where(q_pos[:, None] >= k_pos[None, :], s, -jnp.inf)
        p = jax.nn.softmax(s, axis=-1).astype(v.dtype)
        return jnp.einsum('bhqk,bkhd->bqhd', p, v)

    o = lax.map(block, (jnp.arange(nb), qn_b, qp_b))
    return o.transpose(1, 0, 2, 3, 4).reshape(B, S, MLA_WIDTH)


def gla(q, k, v, r, a_low, w_a2, b_a, norm_g):
    B, S, _ = q.shape
    H, DK, DV, C = GLA_HEADS, GLA_DK, GLA_DV, GLA_CHUNK
    N = S // C
    f32 = jnp.float32
    log_a = jax.nn.log_sigmoid((a_low @ w_a2 + b_a).astype(f32)) / GLA_TAU

    def heads(t, d):
        return t.astype(f32).reshape(B, N, C, H, d).transpose(0, 3, 1, 2, 4)

    qh = heads(q, DK) * DK ** -0.5
    kh = heads(k, DK)
    vh = heads(v, DV)
    bcum = jnp.cumsum(heads(log_a, DK), axis=3)
    causal = jnp.tril(jnp.ones((C, C), dtype=bool))[:, :, None]
    decay = jnp.exp(jnp.where(causal, bcum[..., :, None, :] - bcum[..., None, :, :], -jnp.inf))
    attn = jnp.einsum('bhntd,bhnsd,bhntsd->bhnts', qh, kh, decay)
    o_intra = jnp.einsum('bhnts,bhnsv->bhntv', attn, vh)
    b_last = bcum[..., -1:, :]
    q_dec = qh * jnp.exp(bcum)
    k_dec = kh * jnp.exp(b_last - bcum)
    chunk_decay = jnp.exp(b_last[..., 0, :])

    def step(state, xs):
        qd, kd, vc, cd = xs
        o = jnp.einsum('bhtd,bhdv->bhtv', qd, state)
        state = cd[..., None] * state + jnp.einsum('bhsd,bhsv->bhdv', kd, vc)
        return state, o

    xs = (jnp.moveaxis(q_dec, 2, 0), jnp.moveaxis(k_dec, 2, 0),
          jnp.moveaxis(vh, 2, 0), jnp.moveaxis(chunk_decay, 2, 0))
    _, o_inter = lax.scan(step, jnp.zeros((B, H, DK, DV), f32), xs)
    o = o_intra + jnp.moveaxis(o_inter, 0, 2)
    o = o.transpose(0, 2, 3, 1, 4).reshape(B, S, H, DV)
    o = o * lax.rsqrt(jnp.mean(o * o, axis=-1, keepdims=True) + EPS)
    o = o.reshape(B, S, GLA_WIDTH) * norm_g.astype(f32)
    return (o * jax.nn.silu(r.astype(f32))).astype(q.dtype)


def short_conv(bch, conv_w):
    b_g, c_g, h_in = jnp.split(bch, 3, axis=-1)
    z = c_g * h_in
    y = lax.conv_general_dilated(z, conv_w.astype(z.dtype), window_strides=(1,),
                                 padding=((CONV_WIDTH - 1, 0),),
                                 dimension_numbers=('NWC', 'WIO', 'NWC'),
                                 feature_group_count=CONV_DIM)
    return b_g * y


def cross_attn(hn, memn, w_q, w_kv, w_o):
    B, S, _ = hn.shape
    M = memn.shape[1]
    q = (hn @ w_q).reshape(B, S, X_HEADS, X_HEAD_DIM)
    k, v = jnp.split(memn @ w_kv, 2, axis=-1)
    k = k.reshape(B, M, X_HEADS, X_HEAD_DIM)
    v = v.reshape(B, M, X_HEADS, X_HEAD_DIM)
    s = jnp.einsum('bshd,bmhd->bhsm', q, k).astype(jnp.float32) * X_HEAD_DIM ** -0.5
    p = jax.nn.softmax(s, axis=-1).astype(v.dtype)
    o = jnp.einsum('bhsm,bmhd->bshd', p, v).reshape(B, S, D_MODEL)
    return o @ w_o


def setup_inputs(seed: int = 0) -> dict:
    key = jax.random.key(seed)
    k = jax.random.split(key, 29)
    L, D, F = DEPTH, D_MODEL, D_FF
    f32 = jnp.float32

    def w(kk, shape, fan_in):
        return jax.random.normal(kk, shape, f32) * fan_in ** -0.5

    def gain(kk, shape):
        return 1.0 + 0.02 * jax.random.normal(kk, shape, f32)

    def bias(kk, shape, s):
        return s * jax.random.normal(kk, shape, f32)

    return {
        "x": jax.random.normal(k[0], (BATCH, SEQ, D), f32),
        "mem": jax.random.normal(k[1], (BATCH, MEM_LEN, D), f32),
        "ffn1_norm": gain(k[2], (L, D)),
        "ffn1_w_in": w(k[3], (L, D, 2 * F), D),
        "ffn1_w_out": w(k[4], (L, F, D), F),
        "mix_norm": gain(k[5], (L, D)),
        "mix_w_in": w(k[6], (L, D, D_IN_PROJ), D),
        "mix_b_gate": bias(k[7], (L, N_BRANCH * D), 0.02),
        "mla_q_norm": gain(k[8], (L, MLA_Q_RANK)),
        "mla_kv_norm": gain(k[9], (L, MLA_KV_RANK)),
        "mla_w_uq": w(k[10], (L, MLA_Q_RANK, MLA_HEADS * (MLA_NOPE + MLA_ROPE)), MLA_Q_RANK),
        "mla_w_ukv": w(k[11], (L, MLA_KV_RANK, MLA_HEADS * (MLA_NOPE + MLA_V)), MLA_KV_RANK),
        "mla_w_proj": w(k[12], (L, MLA_WIDTH, D), MLA_WIDTH),
        "gla_w_a2": w(k[13], (L, GLA_GATE_RANK, GLA_HEADS * GLA_DK), GLA_GATE_RANK),
        "gla_b_a": bias(k[14], (L, GLA_HEADS * GLA_DK), 0.1),
        "gla_norm": gain(k[15], (L, GLA_WIDTH)),
        "gla_w_proj": w(k[16], (L, GLA_WIDTH, D), GLA_WIDTH),
        "conv_w": w(k[17], (L, CONV_WIDTH, 1, CONV_DIM), CONV_WIDTH),
        "conv_w_proj": w(k[18], (L, CONV_DIM, D), CONV_DIM),
        "mix_w_out": w(k[19], (L, D, D), D),
        "xattn_norm": gain(k[20], (L, D)),
        "mem_norm": gain(k[21], (L, D)),
        "xattn_w_q": w(k[22], (L, D, D), D),
        "xattn_w_kv": w(k[23], (L, D, 2 * D), D),
        "xattn_w_o": w(k[24], (L, D, D), D),
        "ffn2_norm": gain(k[25], (L, D)),
        "ffn2_w_in": w(k[26], (L, D, 2 * F), D),
        "ffn2_w_out": w(k[27], (L, F, D), F),
        "final_norm": gain(k[28], (D,)),
    }


def reference(x, mem, ffn1_norm, ffn1_w_in, ffn1_w_out, mix_norm, mix_w_in, mix_b_gate,
              mla_q_norm, mla_kv_norm, mla_w_uq, mla_w_ukv, mla_w_proj,
              gla_w_a2, gla_b_a, gla_norm, gla_w_proj, conv_w, conv_w_proj, mix_w_out,
              xattn_norm, mem_norm, xattn_w_q, xattn_w_kv, xattn_w_o,
              ffn2_norm, ffn2_w_in, ffn2_w_out, final_norm):
    cos, sin = rope_tables(x.shape[1])
    h = x
    for l in range(DEPTH):
        h = h + 0.5 * swiglu(rmsnorm(h, ffn1_norm[l]), ffn1_w_in[l], ffn1_w_out[l])
        u = rmsnorm(h, mix_norm[l])
        z = u @ mix_w_in[l]
        (c_q, c_kv, k_rope, g_q, g_k, g_v, g_r, a_low, conv_in, gate_pre) = _split(z, IN_SPLITS)
        y_mla = mla(c_q, c_kv, k_rope, mla_q_norm[l], mla_kv_norm[l],
                    mla_w_uq[l], mla_w_ukv[l], cos, sin) @ mla_w_proj[l]
        y_gla = gla(g_q, g_k, g_v, g_r, a_low, gla_w_a2[l], gla_b_a[l], gla_norm[l]) @ gla_w_proj[l]
        y_conv = short_conv(conv_in, conv_w[l]) @ conv_w_proj[l]
        gates = jax.nn.sigmoid((gate_pre + mix_b_gate[l]).astype(jnp.float32)).astype(h.dtype)
        g_mla, g_gla, g_conv = jnp.split(gates, N_BRANCH, axis=-1)
        merged = g_mla * y_mla + g_gla * y_gla + g_conv * y_conv
        h = h + merged @ mix_w_out[l]
        h = h + cross_attn(rmsnorm(h, xattn_norm[l]), rmsnorm(mem, mem_norm[l]),
                           xattn_w_q[l], xattn_w_kv[l], xattn_w_o[l])
        h = h + 0.5 * swiglu(rmsnorm(h, ffn2_norm[l]), ffn2_w_in[l], ffn2_w_out[l])
    return rmsnorm(h, final_norm)
```

```python
import functools

import jax
import jax.numpy as jnp
from jax import lax
from jax.experimental import pallas as pl
from jax.experimental.pallas import tpu as pltpu

F32 = jnp.float32
BF16 = jnp.bfloat16

EPS = 1e-6
D_MODEL = 2048
D_FF = 5632
MLA_HEADS = 8
MLA_RANK = 512
MLA_NOPE = 128
MLA_ROPE = 64
MLA_V = 128
MLA_QK_PAD = 256
ROPE_THETA = 10000.0
GLA_HEADS = 4
GLA_DK = 64
GLA_DV = 128
GLA_RANK = 16
GLA_TAU = 16.0
GLA_CHUNK = 64
CONV_DIM = 512
X_HEADS = 4
X_HEAD_DIM = D_MODEL // X_HEADS
MEM_LEN = 256

LANES = 128
VMEM_LIMIT_BYTES = 56 * 1024 * 1024

Z_CONV = 0
Z_CQ = 1536
Z_CKV = 2048
Z_GV = 2560
Z_GR = 3072
Z_GQ = 3584
Z_GK = 3840
Z_KR = 4096
Z_AL = 4224
Z_WIDTH = 4352


def _params(*sem):
    return pltpu.CompilerParams(dimension_semantics=sem, vmem_limit_bytes=VMEM_LIMIT_BYTES)


def _rms(x, g):
    return x * lax.rsqrt(jnp.mean(x * x, axis=-1, keepdims=True) + EPS) * g


def _dot(a, b):
    return jnp.dot(a, b, preferred_element_type=F32)


def _dot_t(a, b):
    return lax.dot_general(a, b, (((1,), (1,)), ((), ())), preferred_element_type=F32)


def _norm_matmul_kernel(x_ref, g_ref, w_ref, o_ref):
    xn = _rms(x_ref[...], g_ref[...]).astype(BF16)
    o_ref[...] = _dot(xn, w_ref[...]).astype(o_ref.dtype)


def norm_matmul(x, g, w, *, tm, tn, out_dtype):
    m, k = x.shape
    n = w.shape[1]
    assert m % tm == 0 and n % tn == 0
    return pl.pallas_call(
        _norm_matmul_kernel,
        grid=(n // tn, m // tm),
        in_specs=[
            pl.BlockSpec((tm, k), lambda j, i: (i, 0)),
            pl.BlockSpec((1, k), lambda j, i: (0, 0)),
            pl.BlockSpec((k, tn), lambda j, i: (0, j)),
        ],
        out_specs=pl.BlockSpec((tm, tn), lambda j, i: (i, j)),
        out_shape=jax.ShapeDtypeStruct((m, n), out_dtype),
        compiler_params=_params("arbitrary", "arbitrary"),
        name="norm_matmul",
    )(x, g.reshape(1, k), w)


def _ffn_kernel(x_ref, g_ref, wg_ref, wu_ref, wo_ref, gf_ref, o_ref, xn_ref, *, final_norm):
    j = pl.program_id(1)

    @pl.when(j == 0)
    def _():
        x = x_ref[...]
        xn_ref[...] = _rms(x, g_ref[...]).astype(BF16)
        o_ref[...] = x

    xn = xn_ref[...]
    gate = _dot(xn, wg_ref[...])
    up = _dot(xn, wu_ref[...])
    act = (0.5 * gate) * jax.nn.sigmoid(gate) * up
    o_ref[...] += _dot(act.astype(BF16), wo_ref[...])

    if final_norm:
        @pl.when(j == pl.num_programs(1) - 1)
        def _():
            o_ref[...] = _rms(o_ref[...], gf_ref[...])


def ffn(x, g, w_in, w_out, g_final, *, tm, tf, final_norm):
    m, d = x.shape
    f = w_out.shape[0]
    assert m % tm == 0 and f % tf == 0
    nf = f // tf
    return pl.pallas_call(
        functools.partial(_ffn_kernel, final_norm=final_norm),
        grid=(m // tm, nf),
        in_specs=[
            pl.BlockSpec((tm, d), lambda i, j: (i, 0)),
            pl.BlockSpec((1, d), lambda i, j: (0, 0)),
            pl.BlockSpec((d, tf), lambda i, j: (0, j)),
            pl.BlockSpec((d, tf), lambda i, j: (0, nf + j)),
            pl.BlockSpec((tf, d), lambda i, j: (j, 0)),
            pl.BlockSpec((1, d), lambda i, j: (0, 0)),
        ],
        out_specs=pl.BlockSpec((tm, d), lambda i, j: (i, 0)),
        out_shape=jax.ShapeDtypeStruct((m, d), F32),
        scratch_shapes=[pltpu.VMEM((tm, d), BF16)],
        compiler_params=_params("arbitrary", "arbitrary"),
        name="ffn",
    )(x, g.reshape(1, d), w_in, w_in, w_out, g_final.reshape(1, d))


def _rope_pairs(x, table):
    y = x * table
    return y + pltpu.roll(y, MLA_ROPE, 1)


def _mla_prep_kernel(cq_ref, ckv_ref, kr_ref, tab_ref, gq_ref, gkv_ref, wq_ref, wk_ref, wv_ref,
                     q_ref, k_ref, v_ref):
    scale = (MLA_NOPE + MLA_ROPE) ** -0.5
    tab = tab_ref[...]
    nq = _rms(cq_ref[...], gq_ref[...]).astype(BF16)
    q = _dot(nq, wq_ref[...]) * scale
    nkv = _rms(ckv_ref[...], gkv_ref[...]).astype(BF16)
    kn = _dot(nkv, wk_ref[...])
    v = _dot(nkv, wv_ref[...])
    lane = lax.broadcasted_iota(jnp.int32, tab.shape, 1)
    k_pe = jnp.where(lane < MLA_ROPE, _rope_pairs(kr_ref[...], tab), 0.0).astype(BF16)
    for h in range(MLA_HEADS):
        q0 = h * MLA_QK_PAD
        q_ref[0, h, :, 0:LANES] = q[:, q0:q0 + LANES].astype(BF16)
        q_ref[0, h, :, LANES:2 * LANES] = _rope_pairs(q[:, q0 + LANES:q0 + 2 * LANES], tab).astype(BF16)
        k_ref[0, h, :, 0:LANES] = kn[:, h * LANES:(h + 1) * LANES].astype(BF16)
        k_ref[0, h, :, LANES:2 * LANES] = k_pe
        v_ref[0, h] = v[:, h * LANES:(h + 1) * LANES].astype(BF16)


def mla_prep(z, table, g_q, g_kv, wq, wk, wv, *, batch, seq, tm):
    assert seq % tm == 0
    nb = seq // tm
    r = MLA_RANK
    hw = MLA_HEADS
    tok = lambda width, col: pl.BlockSpec((tm, width), lambda i: (i, col // width))
    full = lambda a: pl.BlockSpec(a.shape, lambda i: (0,) * a.ndim)
    head_out = lambda width: pl.BlockSpec((1, hw, tm, width), lambda i: (i // nb, 0, i % nb, 0))
    return pl.pallas_call(
        _mla_prep_kernel,
        grid=(batch * nb,),
        in_specs=[
            tok(r, Z_CQ), tok(r, Z_CKV), tok(LANES, Z_KR),
            pl.BlockSpec((tm, LANES), lambda i: (i % nb, 0)),
            full(g_q), full(g_kv), full(wq), full(wk), full(wv),
        ],
        out_specs=[head_out(MLA_QK_PAD), head_out(MLA_QK_PAD), head_out(MLA_V)],
        out_shape=[
            jax.ShapeDtypeStruct((batch, hw, seq, MLA_QK_PAD), BF16),
            jax.ShapeDtypeStruct((batch, hw, seq, MLA_QK_PAD), BF16),
            jax.ShapeDtypeStruct((batch, hw, seq, MLA_V), BF16),
        ],
        compiler_params=_params("arbitrary"),
        name="mla_prep",
    )(z, z, z, table, g_q, g_kv, wq, wk, wv)


def _mla_attn_kernel(q_ref, k_ref, v_ref, o_ref, m_ref, l_ref, acc_ref, *, tq):
    qi = pl.program_id(2)
    q = q_ref[0, 0]
    m_ref[...] = jnp.full(m_ref.shape, -jnp.inf, F32)
    l_ref[...] = jnp.zeros(l_ref.shape, F32)
    acc_ref[...] = jnp.zeros(acc_ref.shape, F32)

    def step(j, masked):
        start = pl.multiple_of(j * tq, tq)
        s = _dot_t(q, k_ref[0, 0, pl.ds(start, tq), :])
        if masked:
            row = lax.broadcasted_iota(jnp.int32, s.shape, 0)
            col = lax.broadcasted_iota(jnp.int32, s.shape, 1)
            s = jnp.where(row >= col, s, -jnp.inf)
        m_old = m_ref[...]
        m_new = jnp.maximum(m_old, jnp.max(s, axis=-1, keepdims=True))
        p = jnp.exp(s - m_new)
        alpha = jnp.exp(m_old - m_new)
        l_ref[...] = alpha * l_ref[...] + jnp.sum(p, axis=-1, keepdims=True)
        acc_ref[...] = alpha * acc_ref[...] + _dot(p.astype(BF16), v_ref[0, 0, pl.ds(start, tq), :])
        m_ref[...] = m_new

    def body(j, carry):
        step(j, False)
        return carry

    lax.fori_loop(0, qi, body, 0)
    step(qi, True)
    o_ref[0] = (acc_ref[...] / l_ref[...]).astype(o_ref.dtype)


def mla_attn(q, k, v, *, tq):
    b, h, s, dq = q.shape
    dv = v.shape[-1]
    assert s % tq == 0
    return pl.pallas_call(
        functools.partial(_mla_attn_kernel, tq=tq),
        grid=(b, h, s // tq),
        in_specs=[
            pl.BlockSpec((1, 1, tq, dq), lambda bi, hi, qi: (bi, hi, qi, 0)),
            pl.BlockSpec((1, 1, s, dq), lambda bi, hi, qi: (bi, hi, 0, 0)),
            pl.BlockSpec((1, 1, s, dv), lambda bi, hi, qi: (bi, hi, 0, 0)),
        ],
        out_specs=pl.BlockSpec((1, tq, dv), lambda bi, hi, qi: (bi, qi, hi)),
        out_shape=jax.ShapeDtypeStruct((b, s, h * dv), BF16),
        scratch_shapes=[pltpu.VMEM((tq, 1), F32), pltpu.VMEM((tq, 1), F32), pltpu.VMEM((tq, dv), F32)],
        compiler_params=_params("arbitrary", "arbitrary", "arbitrary"),
        name="mla_attn",
    )(q, k, v)


def _gla_kernel(q_ref, k_ref, v_ref, r_ref, al_ref, wa_ref, ba_ref, ng_ref, o_ref,
                state_ref, b_scr, q_scr, k_scr, v_scr, oi_scr, *, tc):
    c = GLA_CHUNK
    pair_w = 2 * GLA_DK
    pair_v = 2 * GLA_DV
    n_pairs = GLA_HEADS // 2

    @pl.when(pl.program_id(1) == 0)
    def _():
        state_ref[...] = jnp.zeros(state_ref.shape, F32)

    row_i = lax.broadcasted_iota(jnp.int32, (c, c), 0)
    col_i = lax.broadcasted_iota(jnp.int32, (c, c), 1)
    tri = jnp.where(row_i >= col_i, 1.0, 0.0).astype(BF16)
    lane_k = lax.broadcasted_iota(jnp.int32, (pair_w, LANES), 0)
    head_sum = [jnp.where((lane_k // GLA_DK) == hh, 1.0, 0.0).astype(BF16) for hh in range(2)]
    srow = lax.broadcasted_iota(jnp.int32, (pair_w, pair_v), 0) // GLA_DK
    scol = lax.broadcasted_iota(jnp.int32, (pair_w, pair_v), 1) // GLA_DV
    own_head = srow == scol

    def chunk(ci, carry):
        r0 = pl.multiple_of(ci * c, c)
        rows = pl.ds(r0, c)
        x = _dot(al_ref[rows, :].astype(BF16), wa_ref[...]) + ba_ref[...]
        log_a = (jnp.minimum(x, 0.0) - jnp.log(1.0 + jnp.exp(-jnp.abs(x)))) * (1.0 / GLA_TAU)
        p1 = log_a.astype(BF16)
        r1 = log_a - p1.astype(F32)
        p2 = r1.astype(BF16)
        p3 = (r1 - p2.astype(F32)).astype(BF16)
        b_scr[...] = _dot(tri, p1) + _dot(tri, p2) + _dot(tri, p3)
        q_scr[...] = q_ref[rows, :] * (GLA_DK ** -0.5)
        k_scr[...] = k_ref[rows, :]
        v_scr[...] = v_ref[rows, :]

        for p in range(n_pairs):
            kl = slice(p * pair_w, (p + 1) * pair_w)
            vl = slice(p * pair_v, (p + 1) * pair_v)
            bp = b_scr[:, kl]
            qp = q_scr[:, kl]
            kp = k_scr[:, kl]
            vp = v_scr[:, vl]
            b_last = b_scr[c - 1:c, kl]
            st = state_ref[p]
            q_dec = (qp * jnp.exp(bp)).astype(BF16)
            k_dec = (kp * jnp.exp(b_last - bp)).astype(BF16)
            oi_scr[:, vl] = _dot(q_dec, st.astype(BF16))
            kv = lax.dot_general(k_dec, vp.astype(BF16), (((0,), (0,)), ((), ())),
                                 preferred_element_type=F32)
            decay_col = jnp.transpose(jnp.broadcast_to(jnp.exp(b_last), (LANES, pair_w)))
            decay_col = jnp.concatenate([decay_col, decay_col], axis=1)
            state_ref[p] = decay_col * st + jnp.where(own_head, kv, 0.0)
            for g in range(c // 8):
                t0 = 8 * g
                n = c - t0
                bt = b_scr[t0:c, kl]
                qt = q_scr[t0:c, kl]
                t_idx = t0 + lax.broadcasted_iota(jnp.int32, (n, pair_w), 0)
                ys = []
                for s in range(t0, t0 + 8):
                    e = jnp.exp(jnp.where(t_idx >= s, bt - b_scr[s:s + 1, kl], -jnp.inf))
                    ys.append(qt * e * k_scr[s:s + 1, kl])
                y = jnp.concatenate(ys, axis=0).astype(BF16)
                for hh in range(2):
                    hv = slice(p * pair_v + hh * GLA_DV, p * pair_v + (hh + 1) * GLA_DV)
                    a = _dot(y, head_sum[hh])
                    upd = a[0:n] * v_scr[t0:t0 + 1, hv]
                    for idx in range(1, 8):
                        upd += a[idx * n:(idx + 1) * n] * v_scr[t0 + idx:t0 + idx + 1, hv]
                    oi_scr[t0:c, hv] += upd

        o = oi_scr[...]
        gate = r_ref[rows, :]
        gate = gate * jax.nn.sigmoid(gate)
        for h in range(GLA_HEADS):
            hv = slice(h * GLA_DV, (h + 1) * GLA_DV)
            oh = o[:, hv]
            oh = oh * lax.rsqrt(jnp.mean(oh * oh, axis=-1, keepdims=True) + EPS)
            o_ref[rows, hv] = (oh * ng_ref[:, hv] * gate[:, hv]).astype(o_ref.dtype)
        return carry

    lax.fori_loop(0, tc // c, chunk, 0)


def gla(z, w_a2, b_a, norm_g, *, batch, seq, tc):
    assert seq % tc == 0 and tc % GLA_CHUNK == 0
    nb = seq // tc
    kw = GLA_HEADS * GLA_DK
    vw = GLA_HEADS * GLA_DV
    tok = lambda width, col: pl.BlockSpec((tc, width), lambda b, i: (b * nb + i, col // width))
    full = lambda a: pl.BlockSpec(a.shape, lambda b, i: (0,) * a.ndim)
    c = GLA_CHUNK
    return pl.pallas_call(
        functools.partial(_gla_kernel, tc=tc),
        grid=(batch, nb),
        in_specs=[
            tok(kw, Z_GQ), tok(kw, Z_GK), tok(vw, Z_GV), tok(vw, Z_GR), tok(LANES, Z_AL),
            full(w_a2), full(b_a), full(norm_g),
        ],
        out_specs=pl.BlockSpec((tc, vw), lambda b, i: (b * nb + i, 0)),
        out_shape=jax.ShapeDtypeStruct((batch * seq, vw), BF16),
        scratch_shapes=[
            pltpu.VMEM((GLA_HEADS // 2, 2 * GLA_DK, 2 * GLA_DV), F32),
            pltpu.VMEM((c, kw), F32), pltpu.VMEM((c, kw), F32), pltpu.VMEM((c, kw), F32),
            pltpu.VMEM((c, vw), F32), pltpu.VMEM((c, vw), F32),
        ],
        compiler_params=_params("arbitrary", "arbitrary"),
        name="gla",
    )(z, z, z, z, z, w_a2, b_a, norm_g)


def _conv_kernel(x_ref, halo_ref, w_ref, o_ref, *, blocks_per_seq):
    d = CONV_DIM
    x = x_ref[...]
    z = x[:, d:2 * d] * x[:, 2 * d:3 * d]
    hz = halo_ref[:, d:2 * d] * halo_ref[:, 2 * d:3 * d]
    hz = jnp.where(pl.program_id(0) % blocks_per_seq == 0, 0.0, hz)
    row = lax.broadcasted_iota(jnp.int32, z.shape, 0)
    z1 = jnp.where(row == 0, hz[7:8], pltpu.roll(z, 1, 0))
    z2 = jnp.where(row == 0, hz[6:7], jnp.where(row == 1, hz[7:8], pltpu.roll(z, 2, 0)))
    y = w_ref[2:3, :] * z + w_ref[1:2, :] * z1 + w_ref[0:1, :] * z2
    o_ref[...] = (x[:, 0:d] * y).astype(o_ref.dtype)


def short_conv(z, w, *, seq, tm):
    m = z.shape[0]
    assert seq % tm == 0 and tm % 8 == 0
    width = 3 * CONV_DIM
    return pl.pallas_call(
        functools.partial(_conv_kernel, blocks_per_seq=seq // tm),
        grid=(m // tm,),
        in_specs=[
            pl.BlockSpec((tm, width), lambda i: (i, Z_CONV // width)),
            pl.BlockSpec((8, width), lambda i: (jnp.maximum(i * (tm // 8) - 1, 0), Z_CONV // width)),
            pl.BlockSpec(w.shape, lambda i: (0, 0)),
        ],
        out_specs=pl.BlockSpec((tm, CONV_DIM), lambda i: (i, 0)),
        out_shape=jax.ShapeDtypeStruct((m, CONV_DIM), BF16),
        compiler_params=_params("arbitrary"),
        name="short_conv",
    )(z, z, w)


def _merge_kernel(h_ref, g_ref, a0_ref, a1_ref, a2_ref, wg0_ref, wg1_ref, wg2_ref, b0_ref, b1_ref, b2_ref,
                  wp0_ref, wp1_ref, wp2_ref, wo_ref, o_ref, u_ref):
    j = pl.program_id(1)

    @pl.when(j == 0)
    def _():
        h = h_ref[...]
        u_ref[...] = _rms(h, g_ref[...]).astype(BF16)
        o_ref[...] = h

    u = u_ref[...]
    merged = None
    for a_ref, wg_ref, b_ref, wp_ref in ((a0_ref, wg0_ref, b0_ref, wp0_ref),
                                         (a1_ref, wg1_ref, b1_ref, wp1_ref),
                                         (a2_ref, wg2_ref, b2_ref, wp2_ref)):
        gate = jax.nn.sigmoid(_dot(u, wg_ref[...]) + b_ref[...])
        term = gate * _dot(a_ref[...], wp_ref[...])
        merged = term if merged is None else merged + term
    o_ref[...] += _dot(merged.astype(BF16), wo_ref[...])


def merge(h, g, branches, w_gate, b_gate, w_projs, w_out, *, tm, tn):
    m, d = h.shape
    assert m % tm == 0 and d % tn == 0
    nj = d // tn
    row = lambda a: pl.BlockSpec((tm, a.shape[1]), lambda i, j: (i, 0))
    gate_w = lambda b: pl.BlockSpec((d, tn), lambda i, j: (0, b * nj + j))
    gate_b = lambda b: pl.BlockSpec((1, tn), lambda i, j: (0, b * nj + j))
    proj_w = lambda w: pl.BlockSpec((w.shape[0], tn), lambda i, j: (0, j))
    return pl.pallas_call(
        _merge_kernel,
        grid=(m // tm, nj),
        in_specs=[
            row(h), pl.BlockSpec((1, d), lambda i, j: (0, 0)),
            row(branches[0]), row(branches[1]), row(branches[2]),
            gate_w(0), gate_w(1), gate_w(2), gate_b(0), gate_b(1), gate_b(2),
            proj_w(w_projs[0]), proj_w(w_projs[1]), proj_w(w_projs[2]),
            pl.BlockSpec((tn, d), lambda i, j: (j, 0)),
        ],
        out_specs=pl.BlockSpec((tm, d), lambda i, j: (i, 0)),
        out_shape=jax.ShapeDtypeStruct((m, d), F32),
        scratch_shapes=[pltpu.VMEM((tm, d), BF16)],
        compiler_params=_params("arbitrary", "arbitrary"),
        name="merge",
    )(h, g.reshape(1, d), *branches, w_gate, w_gate, w_gate, b_gate, b_gate, b_gate, *w_projs, w_out)


def _xattn_kernel(h_ref, g_ref, wq_ref, k_ref, v_ref, wo_ref, o_ref, hn_ref):
    j = pl.program_id(1)

    @pl.when(j == 0)
    def _():
        h = h_ref[...]
        hn_ref[...] = _rms(h, g_ref[...]).astype(BF16)
        o_ref[...] = h

    q = (_dot(hn_ref[...], wq_ref[...]) * (X_HEAD_DIM ** -0.5)).astype(BF16)
    s = _dot_t(q, k_ref[...])
    p = jnp.exp(s - jnp.max(s, axis=-1, keepdims=True))
    o = _dot(p.astype(BF16), v_ref[...]) / jnp.sum(p, axis=-1, keepdims=True)
    o_ref[...] += _dot(o.astype(BF16), wo_ref[...])


def xattn(h, g, w_q, kv, w_o, *, seq, tm):
    m, d = h.shape
    assert seq % tm == 0
    nb = seq // tm
    hd = X_HEAD_DIM
    return pl.pallas_call(
        _xattn_kernel,
        grid=(m // tm, X_HEADS),
        in_specs=[
            pl.BlockSpec((tm, d), lambda i, j: (i, 0)),
            pl.BlockSpec((1, d), lambda i, j: (0, 0)),
            pl.BlockSpec((d, hd), lambda i, j: (0, j)),
            pl.BlockSpec((MEM_LEN, hd), lambda i, j: (i // nb, j)),
            pl.BlockSpec((MEM_LEN, hd), lambda i, j: (i // nb, X_HEADS + j)),
            pl.BlockSpec((hd, d), lambda i, j: (j, 0)),
        ],
        out_specs=pl.BlockSpec((tm, d), lambda i, j: (i, 0)),
        out_shape=jax.ShapeDtypeStruct((m, d), F32),
        scratch_shapes=[pltpu.VMEM((tm, d), BF16)],
        compiler_params=_params("arbitrary", "arbitrary"),
        name="xattn",
    )(h, g.reshape(1, d), w_q, kv, kv, w_o)


def _pack_in_proj(w):
    c_q, c_kv, k_rope = w[:, 0:512], w[:, 512:1024], w[:, 1024:1088]
    g_q, g_k, g_v, g_r = w[:, 1088:1344], w[:, 1344:1600], w[:, 1600:2112], w[:, 2112:2624]
    a_low, conv = w[:, 2624:2640], w[:, 2640:4176]
    half = MLA_ROPE // 2
    k_swap = jnp.concatenate([k_rope[:, half:], k_rope[:, :half]], axis=1)
    pad = jnp.zeros((w.shape[0], LANES - GLA_RANK), w.dtype)
    packed = jnp.concatenate([conv, c_q, c_kv, g_v, g_r, g_q, g_k, k_rope, k_swap, a_low, pad], axis=1)
    return packed.astype(BF16), w[:, 4176:].astype(BF16)


def _pack_uq(w):
    w = w.reshape(MLA_RANK, MLA_HEADS, MLA_NOPE + MLA_ROPE)
    half = MLA_ROPE // 2
    nope, pe = w[..., :MLA_NOPE], w[..., MLA_NOPE:]
    swap = jnp.concatenate([pe[..., half:], pe[..., :half]], axis=-1)
    return jnp.concatenate([nope, pe, swap], axis=-1).reshape(MLA_RANK, MLA_HEADS * MLA_QK_PAD).astype(BF16)


def _pack_ukv(w):
    w = w.reshape(MLA_RANK, MLA_HEADS, MLA_NOPE + MLA_V)
    wk = w[..., :MLA_NOPE].reshape(MLA_RANK, MLA_HEADS * MLA_NOPE)
    wv = w[..., MLA_NOPE:].reshape(MLA_RANK, MLA_HEADS * MLA_V)
    return wk.astype(BF16), wv.astype(BF16)


def _rope_table(seq):
    inv_freq = 1.0 / (ROPE_THETA ** (jnp.arange(0, MLA_ROPE, 2, dtype=F32) / MLA_ROPE))
    ang = jnp.arange(seq, dtype=F32)[:, None] * inv_freq[None, :]
    cos, sin = jnp.cos(ang), jnp.sin(ang)
    return jnp.concatenate([cos, cos, -sin, sin], axis=1)


def kernel(x, mem, ffn1_norm, ffn1_w_in, ffn1_w_out, mix_norm, mix_w_in, mix_b_gate, mla_q_norm, mla_kv_norm,
           mla_w_uq, mla_w_ukv, mla_w_proj, gla_w_a2, gla_b_a, gla_norm, gla_w_proj, conv_w, conv_w_proj,
           mix_w_out, xattn_norm, mem_norm, xattn_w_q, xattn_w_kv, xattn_w_o, ffn2_norm, ffn2_w_in,
           ffn2_w_out, final_norm):
    batch, seq, d = x.shape
    depth = ffn1_norm.shape[0]
    tokens = batch * seq
    table = _rope_table(seq)
    h = x.reshape(tokens, d)
    mem2 = mem.reshape(batch * mem.shape[1], d)
    bf = lambda w: w.astype(BF16)
    for l in range(depth):
        h = ffn(h, ffn1_norm[l], bf(ffn1_w_in[l]), bf(ffn1_w_out[l]), final_norm, tm=512, tf=512,
                final_norm=False)

        w_in, w_gate = _pack_in_proj(mix_w_in[l])
        z = norm_matmul(h, mix_norm[l], w_in, tm=512, tn=Z_WIDTH // 2, out_dtype=F32)
        wk, wv = _pack_ukv(mla_w_ukv[l])
        q, k, v = mla_prep(z, table, mla_q_norm[l].reshape(1, -1), mla_kv_norm[l].reshape(1, -1),
                           _pack_uq(mla_w_uq[l]), wk, wv, batch=batch, seq=seq, tm=512)
        a_mla = mla_attn(q, k, v, tq=512).reshape(tokens, MLA_HEADS * MLA_V)
        w_a2 = jnp.concatenate([gla_w_a2[l], jnp.zeros((LANES - GLA_RANK, GLA_HEADS * GLA_DK), F32)], axis=0)
        a_gla = gla(z, bf(w_a2), gla_b_a[l].reshape(1, -1), gla_norm[l].reshape(1, -1),
                    batch=batch, seq=seq, tc=512)
        a_conv = short_conv(z, conv_w[l].reshape(3, CONV_DIM), seq=seq, tm=512)
        h = merge(h, mix_norm[l], (a_mla, a_gla, a_conv), w_gate, mix_b_gate[l].reshape(1, -1),
                  (bf(mla_w_proj[l]), bf(gla_w_proj[l]), bf(conv_w_proj[l])), bf(mix_w_out[l]), tm=512, tn=512)

        kv = norm_matmul(mem2, mem_norm[l], bf(xattn_w_kv[l]), tm=mem2.shape[0], tn=1024, out_dtype=BF16)
        h = xattn(h, xattn_norm[l], bf(xattn_w_q[l]), kv, bf(xattn_w_o[l]), seq=seq, tm=512)

        h = ffn(h, ffn2_norm[l], bf(ffn2_w_in[l]), bf(ffn2_w_out[l]), final_norm, tm=512, tf=512,
                final_norm=(l == depth - 1))
    return h.reshape(batch, seq, d)
```

```python
import functools

import jax
import jax.numpy as jnp
from jax import lax
from jax.experimental import pallas as pl
from jax.experimental.pallas import tpu as pltpu

F32 = jnp.float32
BF16 = jnp.bfloat16

EPS = 1e-6
D_MODEL = 2048
D_FF = 5632
MLA_HEADS = 8
MLA_RANK = 512
MLA_NOPE = 128
MLA_ROPE = 64
MLA_V = 128
MLA_QK_PAD = 256
ROPE_THETA = 10000.0
LOG2_E = 1.4426950408889634
GLA_HEADS = 4
GLA_DK = 64
GLA_DV = 128
GLA_RANK = 16
GLA_TAU = 16.0
GLA_CHUNK = 64
CONV_DIM = 512
X_HEADS = 4
X_HEAD_DIM = D_MODEL // X_HEADS
MEM_LEN = 256

LANES = 128
VMEM_LIMIT_BYTES = 56 * 1024 * 1024

Z_CONV = 0
Z_CQ = 1536
Z_CKV = 2048
Z_GV = 2560
Z_GR = 3072
Z_GQ = 3584
Z_GK = 3840
Z_KR = 4096
Z_AL = 4224
Z_WIDTH = 4352


def _params(*sem):
    return pltpu.CompilerParams(dimension_semantics=sem, vmem_limit_bytes=VMEM_LIMIT_BYTES)


def _rms(x, g):
    return x * lax.rsqrt(jnp.mean(x * x, axis=-1, keepdims=True) + EPS) * g


def _dot(a, b):
    return jnp.dot(a, b, preferred_element_type=F32)


def _dot_t(a, b):
    return lax.dot_general(a, b, (((1,), (1,)), ((), ())), preferred_element_type=F32)


def _norm_matmul_kernel(x_ref, g_ref, w_ref, o_ref):
    xn = _rms(x_ref[...], g_ref[...]).astype(BF16)
    o_ref[...] = _dot(xn, w_ref[...]).astype(o_ref.dtype)


def norm_matmul(x, g, w, *, tm, tn, out_dtype):
    m, k = x.shape
    n = w.shape[1]
    assert m % tm == 0 and n % tn == 0
    return pl.pallas_call(
        _norm_matmul_kernel,
        grid=(n // tn, m // tm),
        in_specs=[
            pl.BlockSpec((tm, k), lambda j, i: (i, 0)),
            pl.BlockSpec((1, k), lambda j, i: (0, 0)),
            pl.BlockSpec((k, tn), lambda j, i: (0, j)),
        ],
        out_specs=pl.BlockSpec((tm, tn), lambda j, i: (i, j)),
        out_shape=jax.ShapeDtypeStruct((m, n), out_dtype),
        compiler_params=_params("arbitrary", "arbitrary"),
        name="norm_matmul",
    )(x, g.reshape(1, k), w)


def _ffn_kernel(x_ref, g_ref, wg_ref, wu_ref, wo_ref, gf_ref, o_ref, xn_ref, *, final_norm):
    j = pl.program_id(1)

    @pl.when(j == 0)
    def _():
        x = x_ref[...]
        xn_ref[...] = _rms(x, g_ref[...]).astype(BF16)
        o_ref[...] = x

    xn = xn_ref[...]
    gate = _dot(xn, wg_ref[...])
    up = _dot(xn, wu_ref[...])
    act = (0.5 * gate) * jax.nn.sigmoid(gate) * up
    o_ref[...] += _dot(act.astype(BF16), wo_ref[...])

    if final_norm:
        @pl.when(j == pl.num_programs(1) - 1)
        def _():
            o_ref[...] = _rms(o_ref[...], gf_ref[...])


def ffn(x, g, w_in, w_out, g_final, *, tm, tf, final_norm):
    m, d = x.shape
    f = w_out.shape[0]
    assert m % tm == 0 and f % tf == 0
    nf = f // tf
    return pl.pallas_call(
        functools.partial(_ffn_kernel, final_norm=final_norm),
        grid=(m // tm, nf),
        in_specs=[
            pl.BlockSpec((tm, d), lambda i, j: (i, 0)),
            pl.BlockSpec((1, d), lambda i, j: (0, 0)),
            pl.BlockSpec((d, tf), lambda i, j: (0, j)),
            pl.BlockSpec((d, tf), lambda i, j: (0, nf + j)),
            pl.BlockSpec((tf, d), lambda i, j: (j, 0)),
            pl.BlockSpec((1, d), lambda i, j: (0, 0)),
        ],
        out_specs=pl.BlockSpec((tm, d), lambda i, j: (i, 0)),
        out_shape=jax.ShapeDtypeStruct((m, d), F32),
        scratch_shapes=[pltpu.VMEM((tm, d), BF16)],
        compiler_params=_params("arbitrary", "arbitrary"),
        name="ffn",
    )(x, g.reshape(1, d), w_in, w_in, w_out, g_final.reshape(1, d))


def _rope_pairs(x, table):
    y = x * table
    return y + pltpu.roll(y, MLA_ROPE, 1)


def _mla_prep_kernel(cq_ref, ckv_ref, kr_ref, tab_ref, tab_t_ref, gq_ref, gkv_ref, wq_t_ref, wk_ref, wv_t_ref,
                     q_t_ref, k_ref, v_t_ref):
    scale = (MLA_NOPE + MLA_ROPE) ** -0.5 * LOG2_E
    nq_t = jnp.transpose(_rms(cq_ref[...], gq_ref[...])).astype(BF16)
    q_t = _dot(wq_t_ref[...], nq_t) * scale
    nkv = _rms(ckv_ref[...], gkv_ref[...])
    kn = _dot(nkv.astype(BF16), wk_ref[...])
    v_t = _dot(wv_t_ref[...], jnp.transpose(nkv).astype(BF16))
    tab = tab_ref[...]
    tab_t = tab_t_ref[...]
    lane = lax.broadcasted_iota(jnp.int32, tab.shape, 1)
    k_pe = jnp.where(lane < MLA_ROPE, _rope_pairs(kr_ref[...], tab), 0.0).astype(BF16)
    zeros = jnp.zeros((MLA_QK_PAD - MLA_NOPE - MLA_ROPE, q_t.shape[1]), BF16)
    for h in range(MLA_HEADS):
        q0 = h * MLA_QK_PAD
        q_t_ref[0, h, 0:MLA_NOPE, :] = q_t[q0:q0 + MLA_NOPE].astype(BF16)
        pe = q_t[q0 + MLA_NOPE:q0 + MLA_QK_PAD] * tab_t
        q_t_ref[0, h, MLA_NOPE:MLA_NOPE + MLA_ROPE, :] = (pe[0:MLA_ROPE] + pe[MLA_ROPE:2 * MLA_ROPE]).astype(BF16)
        q_t_ref[0, h, MLA_NOPE + MLA_ROPE:MLA_QK_PAD, :] = zeros
        k_ref[0, h, :, 0:LANES] = kn[:, h * LANES:(h + 1) * LANES].astype(BF16)
        k_ref[0, h, :, LANES:2 * LANES] = k_pe
        v_t_ref[0, h] = v_t[h * MLA_V:(h + 1) * MLA_V].astype(BF16)


def mla_prep(z, table, g_q, g_kv, wq_t, wk, wv_t, *, batch, seq, tm):
    assert seq % tm == 0
    nb = seq // tm
    r = MLA_RANK
    hw = MLA_HEADS
    tok = lambda width, col: pl.BlockSpec((tm, width), lambda i: (i, col // width))
    full = lambda a: pl.BlockSpec(a.shape, lambda i: (0,) * a.ndim)
    feat_major = lambda width: pl.BlockSpec((1, hw, width, tm), lambda i: (i // nb, 0, 0, i % nb))
    return pl.pallas_call(
        _mla_prep_kernel,
        grid=(batch * nb,),
        in_specs=[
            tok(r, Z_CQ), tok(r, Z_CKV), tok(LANES, Z_KR),
            pl.BlockSpec((tm, LANES), lambda i: (i % nb, 0)),
            pl.BlockSpec((LANES, tm), lambda i: (0, i % nb)),
            full(g_q), full(g_kv), full(wq_t), full(wk), full(wv_t),
        ],
        out_specs=[
            feat_major(MLA_QK_PAD),
            pl.BlockSpec((1, hw, tm, MLA_QK_PAD), lambda i: (i // nb, 0, i % nb, 0)),
            feat_major(MLA_V),
        ],
        out_shape=[
            jax.ShapeDtypeStruct((batch, hw, MLA_QK_PAD, seq), BF16),
            jax.ShapeDtypeStruct((batch, hw, seq, MLA_QK_PAD), BF16),
            jax.ShapeDtypeStruct((batch, hw, MLA_V, seq), BF16),
        ],
        compiler_params=_params("arbitrary"),
        name="mla_prep",
    )(z, z, z, table, jnp.transpose(table), g_q, g_kv, wq_t, wk, wv_t)


def _mla_attn_kernel(q_t_ref, k_ref, v_t_ref, o_ref, m_ref, l_ref, acc_ref, *, tq, heads):
    qi = pl.program_id(2)
    m_ref[...] = jnp.full(m_ref.shape, -jnp.inf, F32)
    l_ref[...] = jnp.zeros(l_ref.shape, F32)
    acc_ref[...] = jnp.zeros(acc_ref.shape, F32)

    def step(j, masked):
        start = pl.multiple_of(j * tq, tq)
        for g in range(heads):
            s = _dot(k_ref[0, g, pl.ds(start, tq), :], q_t_ref[0, g])
            if masked:
                key = lax.broadcasted_iota(jnp.int32, s.shape, 0)
                qry = lax.broadcasted_iota(jnp.int32, s.shape, 1)
                s = jnp.where(key <= qry, s, -jnp.inf)
            m_old = m_ref[g]
            m_new = jnp.maximum(m_old, jnp.max(s, axis=0, keepdims=True))
            p = jnp.exp2(s - m_new)
            alpha = jnp.exp2(m_old - m_new)
            l_ref[g] = alpha * l_ref[g] + jnp.sum(p, axis=0, keepdims=True)
            acc_ref[g] = alpha * acc_ref[g] + _dot(v_t_ref[0, g, :, pl.ds(start, tq)], p.astype(BF16))
            m_ref[g] = m_new

    def body(j, carry):
        step(j, False)
        return carry

    lax.fori_loop(0, qi, body, 0)
    step(qi, True)
    for g in range(heads):
        o_t = acc_ref[g] / l_ref[g]
        o_ref[0, :, g * MLA_V:(g + 1) * MLA_V] = jnp.transpose(o_t).astype(o_ref.dtype)


def mla_attn(q_t, k, v_t, *, tq, heads):
    b, h, s, dq = k.shape
    dv = v_t.shape[2]
    assert s % tq == 0 and h % heads == 0
    return pl.pallas_call(
        functools.partial(_mla_attn_kernel, tq=tq, heads=heads),
        grid=(b, h // heads, s // tq),
        in_specs=[
            pl.BlockSpec((1, heads, dq, tq), lambda bi, hi, qi: (bi, hi, 0, qi)),
            pl.BlockSpec((1, heads, s, dq), lambda bi, hi, qi: (bi, hi, 0, 0)),
            pl.BlockSpec((1, heads, dv, s), lambda bi, hi, qi: (bi, hi, 0, 0)),
        ],
        out_specs=pl.BlockSpec((1, tq, heads * dv), lambda bi, hi, qi: (bi, qi, hi)),
        out_shape=jax.ShapeDtypeStruct((b, s, h * dv), BF16),
        scratch_shapes=[pltpu.VMEM((heads, 1, tq), F32), pltpu.VMEM((heads, 1, tq), F32),
                        pltpu.VMEM((heads, dv, tq), F32)],
        compiler_params=_params("arbitrary", "arbitrary", "arbitrary"),
        name="mla_attn",
    )(q_t, k, v_t)


def _gla_kernel(q_ref, k_ref, v_ref, r_ref, al_ref, wa_ref, ba_ref, ng_ref, o_ref,
                state_ref, b_scr, q_scr, k_scr, v_scr, oi_scr, *, tc):
    c = GLA_CHUNK
    pair_w = 2 * GLA_DK
    pair_v = 2 * GLA_DV
    n_pairs = GLA_HEADS // 2

    @pl.when(pl.program_id(1) == 0)
    def _():
        state_ref[...] = jnp.zeros(state_ref.shape, F32)

    row_i = lax.broadcasted_iota(jnp.int32, (c, c), 0)
    col_i = lax.broadcasted_iota(jnp.int32, (c, c), 1)
    tri = jnp.where(row_i >= col_i, 1.0, 0.0).astype(BF16)
    lane_k = lax.broadcasted_iota(jnp.int32, (pair_w, LANES), 0)
    head_sum = [jnp.where((lane_k // GLA_DK) == hh, 1.0, 0.0).astype(BF16) for hh in range(2)]
    srow = lax.broadcasted_iota(jnp.int32, (pair_w, pair_v), 0) // GLA_DK
    scol = lax.broadcasted_iota(jnp.int32, (pair_w, pair_v), 1) // GLA_DV
    own_head = srow == scol

    def chunk(ci, carry):
        r0 = pl.multiple_of(ci * c, c)
        rows = pl.ds(r0, c)
        x = _dot(al_ref[rows, :].astype(BF16), wa_ref[...]) + ba_ref[...]
        log_a = (jnp.minimum(x, 0.0) - jnp.log(1.0 + jnp.exp(-jnp.abs(x)))) * (1.0 / GLA_TAU)
        p1 = log_a.astype(BF16)
        r1 = log_a - p1.astype(F32)
        p2 = r1.astype(BF16)
        p3 = (r1 - p2.astype(F32)).astype(BF16)
        b_scr[...] = _dot(tri, p1) + _dot(tri, p2) + _dot(tri, p3)
        q_scr[...] = q_ref[rows, :] * (GLA_DK ** -0.5)
        k_scr[...] = k_ref[rows, :]
        v_scr[...] = v_ref[rows, :]

        for p in range(n_pairs):
            kl = slice(p * pair_w, (p + 1) * pair_w)
            vl = slice(p * pair_v, (p + 1) * pair_v)
            bp = b_scr[:, kl]
            qp = q_scr[:, kl]
            kp = k_scr[:, kl]
            vp = v_scr[:, vl]
            b_last = b_scr[c - 1:c, kl]
            st = state_ref[p]
            q_dec = (qp * jnp.exp(bp)).astype(BF16)
            k_dec = (kp * jnp.exp(b_last - bp)).astype(BF16)
            oi_scr[:, vl] = _dot(q_dec, st.astype(BF16))
            kv = lax.dot_general(k_dec, vp.astype(BF16), (((0,), (0,)), ((), ())),
                                 preferred_element_type=F32)
            decay_col = jnp.transpose(jnp.broadcast_to(jnp.exp(b_last), (LANES, pair_w)))
            decay_col = jnp.concatenate([decay_col, decay_col], axis=1)
            state_ref[p] = decay_col * st + jnp.where(own_head, kv, 0.0)
            for g in range(c // 8):
                t0 = 8 * g
                n = c - t0
                bt = b_scr[t0:c, kl]
                qt = q_scr[t0:c, kl]
                t_idx = t0 + lax.broadcasted_iota(jnp.int32, (n, pair_w), 0)
                ys = []
                for s in range(t0, t0 + 8):
                    e = jnp.exp(jnp.where(t_idx >= s, bt - b_scr[s:s + 1, kl], -jnp.inf))
                    ys.append(qt * e * k_scr[s:s + 1, kl])
                y = jnp.concatenate(ys, axis=0).astype(BF16)
                for hh in range(2):
                    hv = slice(p * pair_v + hh * GLA_DV, p * pair_v + (hh + 1) * GLA_DV)
                    a = _dot(y, head_sum[hh])
                    upd = a[0:n] * v_scr[t0:t0 + 1, hv]
                    for idx in range(1, 8):
                        upd += a[idx * n:(idx + 1) * n] * v_scr[t0 + idx:t0 + idx + 1, hv]
                    oi_scr[t0:c, hv] += upd

        o = oi_scr[...]
        gate = r_ref[rows, :]
        gate = gate * jax.nn.sigmoid(gate)
        for h in range(GLA_HEADS):
            hv = slice(h * GLA_DV, (h + 1) * GLA_DV)
            oh = o[:, hv]
            oh = oh * lax.rsqrt(jnp.mean(oh * oh, axis=-1, keepdims=True) + EPS)
            o_ref[rows, hv] = (oh * ng_ref[:, hv] * gate[:, hv]).astype(o_ref.dtype)
        return carry

    lax.fori_loop(0, tc // c, chunk, 0)


def gla(z, w_a2, b_a, norm_g, *, batch, seq, tc):
    assert seq % tc == 0 and tc % GLA_CHUNK == 0
    nb = seq // tc
    kw = GLA_HEADS * GLA_DK
    vw = GLA_HEADS * GLA_DV
    tok = lambda width, col: pl.BlockSpec((tc, width), lambda b, i: (b * nb + i, col // width))
    full = lambda a: pl.BlockSpec(a.shape, lambda b, i: (0,) * a.ndim)
    c = GLA_CHUNK
    return pl.pallas_call(
        functools.partial(_gla_kernel, tc=tc),
        grid=(batch, nb),
        in_specs=[
            tok(kw, Z_GQ), tok(kw, Z_GK), tok(vw, Z_GV), tok(vw, Z_GR), tok(LANES, Z_AL),
            full(w_a2), full(b_a), full(norm_g),
        ],
        out_specs=pl.BlockSpec((tc, vw), lambda b, i: (b * nb + i, 0)),
        out_shape=jax.ShapeDtypeStruct((batch * seq, vw), BF16),
        scratch_shapes=[
            pltpu.VMEM((GLA_HEADS // 2, 2 * GLA_DK, 2 * GLA_DV), F32),
            pltpu.VMEM((c, kw), F32), pltpu.VMEM((c, kw), F32), pltpu.VMEM((c, kw), F32),
            pltpu.VMEM((c, vw), F32), pltpu.VMEM((c, vw), F32),
        ],
        compiler_params=_params("arbitrary", "arbitrary"),
        name="gla",
    )(z, z, z, z, z, w_a2, b_a, norm_g)


def _conv_kernel(x_ref, halo_ref, w_ref, o_ref, *, blocks_per_seq):
    d = CONV_DIM
    x = x_ref[...]
    z = x[:, d:2 * d] * x[:, 2 * d:3 * d]
    hz = halo_ref[:, d:2 * d] * halo_ref[:, 2 * d:3 * d]
    hz = jnp.where(pl.program_id(0) % blocks_per_seq == 0, 0.0, hz)
    row = lax.broadcasted_iota(jnp.int32, z.shape, 0)
    z1 = jnp.where(row == 0, hz[7:8], pltpu.roll(z, 1, 0))
    z2 = jnp.where(row == 0, hz[6:7], jnp.where(row == 1, hz[7:8], pltpu.roll(z, 2, 0)))
    y = w_ref[2:3, :] * z + w_ref[1:2, :] * z1 + w_ref[0:1, :] * z2
    o_ref[...] = (x[:, 0:d] * y).astype(o_ref.dtype)


def short_conv(z, w, *, seq, tm):
    m = z.shape[0]
    assert seq % tm == 0 and tm % 8 == 0
    width = 3 * CONV_DIM
    return pl.pallas_call(
        functools.partial(_conv_kernel, blocks_per_seq=seq // tm),
        grid=(m // tm,),
        in_specs=[
            pl.BlockSpec((tm, width), lambda i: (i, Z_CONV // width)),
            pl.BlockSpec((8, width), lambda i: (jnp.maximum(i * (tm // 8) - 1, 0), Z_CONV // width)),
            pl.BlockSpec(w.shape, lambda i: (0, 0)),
        ],
        out_specs=pl.BlockSpec((tm, CONV_DIM), lambda i: (i, 0)),
        out_shape=jax.ShapeDtypeStruct((m, CONV_DIM), BF16),
        compiler_params=_params("arbitrary"),
        name="short_conv",
    )(z, z, w)


def _merge_kernel(h_ref, g_ref, a0_ref, a1_ref, a2_ref, wg0_ref, wg1_ref, wg2_ref, b0_ref, b1_ref, b2_ref,
                  wp0_ref, wp1_ref, wp2_ref, wo_ref, o_ref, u_ref):
    j = pl.program_id(1)

    @pl.when(j == 0)
    def _():
        h = h_ref[...]
        u_ref[...] = _rms(h, g_ref[...]).astype(BF16)
        o_ref[...] = h

    u = u_ref[...]
    merged = None
    for a_ref, wg_ref, b_ref, wp_ref in ((a0_ref, wg0_ref, b0_ref, wp0_ref),
                                         (a1_ref, wg1_ref, b1_ref, wp1_ref),
                                         (a2_ref, wg2_ref, b2_ref, wp2_ref)):
        gate = jax.nn.sigmoid(_dot(u, wg_ref[...]) + b_ref[...])
        term = gate * _dot(a_ref[...], wp_ref[...])
        merged = term if merged is None else merged + term
    o_ref[...] += _dot(merged.astype(BF16), wo_ref[...])


def merge(h, g, branches, w_gate, b_gate, w_projs, w_out, *, tm, tn):
    m, d = h.shape
    assert m % tm == 0 and d % tn == 0
    nj = d // tn
    row = lambda a: pl.BlockSpec((tm, a.shape[1]), lambda i, j: (i, 0))
    gate_w = lambda b: pl.BlockSpec((d, tn), lambda i, j: (0, b * nj + j))
    gate_b = lambda b: pl.BlockSpec((1, tn), lambda i, j: (0, b * nj + j))
    proj_w = lambda w: pl.BlockSpec((w.shape[0], tn), lambda i, j: (0, j))
    return pl.pallas_call(
        _merge_kernel,
        grid=(m // tm, nj),
        in_specs=[
            row(h), pl.BlockSpec((1, d), lambda i, j: (0, 0)),
            row(branches[0]), row(branches[1]), row(branches[2]),
            gate_w(0), gate_w(1), gate_w(2), gate_b(0), gate_b(1), gate_b(2),
            proj_w(w_projs[0]), proj_w(w_projs[1]), proj_w(w_projs[2]),
            pl.BlockSpec((tn, d), lambda i, j: (j, 0)),
        ],
        out_specs=pl.BlockSpec((tm, d), lambda i, j: (i, 0)),
        out_shape=jax.ShapeDtypeStruct((m, d), F32),
        scratch_shapes=[pltpu.VMEM((tm, d), BF16)],
        compiler_params=_params("arbitrary", "arbitrary"),
        name="merge",
    )(h, g.reshape(1, d), *branches, w_gate, w_gate, w_gate, b_gate, b_gate, b_gate, *w_projs, w_out)


def _xattn_kernel(h_ref, g_ref, wq_ref, k_ref, v_ref, wo_ref, o_ref, hn_ref):
    j = pl.program_id(1)

    @pl.when(j == 0)
    def _():
        h = h_ref[...]
        hn_ref[...] = _rms(h, g_ref[...]).astype(BF16)
        o_ref[...] = h

    q = (_dot(hn_ref[...], wq_ref[...]) * (X_HEAD_DIM ** -0.5)).astype(BF16)
    s = _dot_t(q, k_ref[...])
    p = jnp.exp(s - jnp.max(s, axis=-1, keepdims=True))
    o = _dot(p.astype(BF16), v_ref[...]) / jnp.sum(p, axis=-1, keepdims=True)
    o_ref[...] += _dot(o.astype(BF16), wo_ref[...])


def xattn(h, g, w_q, kv, w_o, *, seq, tm):
    m, d = h.shape
    assert seq % tm == 0
    nb = seq // tm
    hd = X_HEAD_DIM
    return pl.pallas_call(
        _xattn_kernel,
        grid=(m // tm, X_HEADS),
        in_specs=[
            pl.BlockSpec((tm, d), lambda i, j: (i, 0)),
            pl.BlockSpec((1, d), lambda i, j: (0, 0)),
            pl.BlockSpec((d, hd), lambda i, j: (0, j)),
            pl.BlockSpec((MEM_LEN, hd), lambda i, j: (i // nb, j)),
            pl.BlockSpec((MEM_LEN, hd), lambda i, j: (i // nb, X_HEADS + j)),
            pl.BlockSpec((hd, d), lambda i, j: (j, 0)),
        ],
        out_specs=pl.BlockSpec((tm, d), lambda i, j: (i, 0)),
        out_shape=jax.ShapeDtypeStruct((m, d), F32),
        scratch_shapes=[pltpu.VMEM((tm, d), BF16)],
        compiler_params=_params("arbitrary", "arbitrary"),
        name="xattn",
    )(h, g.reshape(1, d), w_q, kv, kv, w_o)


def _pack_in_proj(w):
    c_q, c_kv, k_rope = w[:, 0:512], w[:, 512:1024], w[:, 1024:1088]
    g_q, g_k, g_v, g_r = w[:, 1088:1344], w[:, 1344:1600], w[:, 1600:2112], w[:, 2112:2624]
    a_low, conv = w[:, 2624:2640], w[:, 2640:4176]
    half = MLA_ROPE // 2
    k_swap = jnp.concatenate([k_rope[:, half:], k_rope[:, :half]], axis=1)
    pad = jnp.zeros((w.shape[0], LANES - GLA_RANK), w.dtype)
    packed = jnp.concatenate([conv, c_q, c_kv, g_v, g_r, g_q, g_k, k_rope, k_swap, a_low, pad], axis=1)
    return packed.astype(BF16), w[:, 4176:].astype(BF16)


def _pack_uq(w):
    w = w.reshape(MLA_RANK, MLA_HEADS, MLA_NOPE + MLA_ROPE)
    half = MLA_ROPE // 2
    nope, pe = w[..., :MLA_NOPE], w[..., MLA_NOPE:]
    swap = jnp.concatenate([pe[..., half:], pe[..., :half]], axis=-1)
    packed = jnp.concatenate([nope, pe, swap], axis=-1).reshape(MLA_RANK, MLA_HEADS * MLA_QK_PAD)
    return jnp.transpose(packed).astype(BF16)


def _pack_ukv(w):
    w = w.reshape(MLA_RANK, MLA_HEADS, MLA_NOPE + MLA_V)
    wk = w[..., :MLA_NOPE].reshape(MLA_RANK, MLA_HEADS * MLA_NOPE)
    wv = w[..., MLA_NOPE:].reshape(MLA_RANK, MLA_HEADS * MLA_V)
    return wk.astype(BF16), jnp.transpose(wv).astype(BF16)


def _rope_table(seq):
    inv_freq = 1.0 / (ROPE_THETA ** (jnp.arange(0, MLA_ROPE, 2, dtype=F32) / MLA_ROPE))
    ang = jnp.arange(seq, dtype=F32)[:, None] * inv_freq[None, :]
    cos, sin = jnp.cos(ang), jnp.sin(ang)
    return jnp.concatenate([cos, cos, -sin, sin], axis=1)


def kernel(x, mem, ffn1_norm, ffn1_w_in, ffn1_w_out, mix_norm, mix_w_in, mix_b_gate, mla_q_norm, mla_kv_norm,
           mla_w_uq, mla_w_ukv, mla_w_proj, gla_w_a2, gla_b_a, gla_norm, gla_w_proj, conv_w, conv_w_proj,
           mix_w_out, xattn_norm, mem_norm, xattn_w_q, xattn_w_kv, xattn_w_o, ffn2_norm, ffn2_w_in,
           ffn2_w_out, final_norm):
    batch, seq, d = x.shape
    depth = ffn1_norm.shape[0]
    tokens = batch * seq
    table = _rope_table(seq)
    h = x.reshape(tokens, d)
    mem2 = mem.reshape(batch * mem.shape[1], d)
    bf = lambda w: w.astype(BF16)
    for l in range(depth):
        h = ffn(h, ffn1_norm[l], bf(ffn1_w_in[l]), bf(ffn1_w_out[l]), final_norm, tm=512, tf=512,
                final_norm=False)

        w_in, w_gate = _pack_in_proj(mix_w_in[l])
        z = norm_matmul(h, mix_norm[l], w_in, tm=512, tn=Z_WIDTH // 2, out_dtype=F32)
        wk, wv_t = _pack_ukv(mla_w_ukv[l])
        q_t, k, v_t = mla_prep(z, table, mla_q_norm[l].reshape(1, -1), mla_kv_norm[l].reshape(1, -1),
                               _pack_uq(mla_w_uq[l]), wk, wv_t, batch=batch, seq=seq, tm=512)
        a_mla = mla_attn(q_t, k, v_t, tq=512, heads=2).reshape(tokens, MLA_HEADS * MLA_V)
        w_a2 = jnp.concatenate([gla_w_a2[l], jnp.zeros((LANES - GLA_RANK, GLA_HEADS * GLA_DK), F32)], axis=0)
        a_gla = gla(z, bf(w_a2), gla_b_a[l].reshape(1, -1), gla_norm[l].reshape(1, -1),
                    batch=batch, seq=seq, tc=512)
        a_conv = short_conv(z, conv_w[l].reshape(3, CONV_DIM), seq=seq, tm=512)
        h = merge(h, mix_norm[l], (a_mla, a_gla, a_conv), w_gate, mix_b_gate[l].reshape(1, -1),
                  (bf(mla_w_proj[l]), bf(gla_w_proj[l]), bf(conv_w_proj[l])), bf(mix_w_out[l]), tm=512, tn=512)

        kv = norm_matmul(mem2, mem_norm[l], bf(xattn_w_kv[l]), tm=mem2.shape[0], tn=1024, out_dtype=BF16)
        h = xattn(h, xattn_norm[l], bf(xattn_w_q[l]), kv, bf(xattn_w_o[l]), seq=seq, tm=512)

        h = ffn(h, ffn2_norm[l], bf(ffn2_w_in[l]), bf(ffn2_w_out[l]), final_norm, tm=512, tf=512,
                final_norm=(l == depth - 1))
    return h.reshape(batch, seq, d)
```

```python
import functools

import jax
import jax.numpy as jnp
from jax import lax
from jax.experimental import pallas as pl
from jax.experimental.pallas import tpu as pltpu

F32 = jnp.float32
BF16 = jnp.bfloat16

EPS = 1e-6
D_MODEL = 2048
D_FF = 5632
MLA_HEADS = 8
MLA_RANK = 512
MLA_NOPE = 128
MLA_ROPE = 64
MLA_V = 128
MLA_QK_PAD = 256
ROPE_THETA = 10000.0
LOG2_E = 1.4426950408889634
GLA_HEADS = 4
GLA_DK = 64
GLA_DV = 128
GLA_RANK = 16
GLA_TAU = 16.0
GLA_CHUNK = 64
CONV_DIM = 512
X_HEADS = 4
X_HEAD_DIM = D_MODEL // X_HEADS
MEM_LEN = 256

LANES = 128
VMEM_LIMIT_BYTES = 56 * 1024 * 1024

Z_CONV = 0
Z_CQ = 1536
Z_CKV = 2048
Z_GV = 2560
Z_GR = 3072
Z_GQ = 3584
Z_GK = 3840
Z_KR = 4096
Z_AL = 4224
Z_WIDTH = 4352


def _params(*sem):
    return pltpu.CompilerParams(dimension_semantics=sem, vmem_limit_bytes=VMEM_LIMIT_BYTES)


def _rms(x, g):
    return x * lax.rsqrt(jnp.mean(x * x, axis=-1, keepdims=True) + EPS) * g


def _dot(a, b):
    return jnp.dot(a, b, preferred_element_type=F32)


def _dot_t(a, b):
    return lax.dot_general(a, b, (((1,), (1,)), ((), ())), preferred_element_type=F32)


def _norm_matmul_kernel(x_ref, g_ref, w_ref, o_ref):
    xn = _rms(x_ref[...], g_ref[...]).astype(BF16)
    o_ref[...] = _dot(xn, w_ref[...].astype(BF16)).astype(o_ref.dtype)


def _layer(layer, block, index):
    return pl.BlockSpec((None,) + tuple(block), lambda *ids: (layer,) + tuple(index(*ids)))


def _vec(layer, width):
    return _layer(layer, (1, width), lambda *ids: (0, 0))


def norm_matmul(x, g, w, layer, *, tm, tn, out_dtype):
    m, k = x.shape
    n = w.shape[2]
    assert m % tm == 0 and n % tn == 0
    return pl.pallas_call(
        _norm_matmul_kernel,
        grid=(n // tn, m // tm),
        in_specs=[
            pl.BlockSpec((tm, k), lambda j, i: (i, 0)),
            _vec(layer, k),
            _layer(layer, (k, tn), lambda j, i: (0, j)),
        ],
        out_specs=pl.BlockSpec((tm, tn), lambda j, i: (i, j)),
        out_shape=jax.ShapeDtypeStruct((m, n), out_dtype),
        compiler_params=_params("arbitrary", "arbitrary"),
        name="norm_matmul",
    )(x, g, w)


def _ffn_kernel(x_ref, g_ref, wg_ref, wu_ref, wo_ref, gf_ref, o_ref, xn_ref, *, final_norm):
    j = pl.program_id(1)

    @pl.when(j == 0)
    def _():
        x = x_ref[...]
        xn_ref[...] = _rms(x, g_ref[...]).astype(BF16)
        o_ref[...] = x

    xn = xn_ref[...]
    gate = _dot(xn, wg_ref[...])
    up = _dot(xn, wu_ref[...])
    act = (0.5 * gate) * jax.nn.sigmoid(gate) * up
    o_ref[...] += _dot(act.astype(BF16), wo_ref[...])

    if final_norm:
        @pl.when(j == pl.num_programs(1) - 1)
        def _():
            o_ref[...] = _rms(o_ref[...], gf_ref[...])


def ffn(x, g, w_in, w_out, g_final, layer, *, tm, tf, final_norm):
    m, d = x.shape
    f = w_out.shape[1]
    assert m % tm == 0 and f % tf == 0
    nf = f // tf
    return pl.pallas_call(
        functools.partial(_ffn_kernel, final_norm=final_norm),
        grid=(m // tm, nf),
        in_specs=[
            pl.BlockSpec((tm, d), lambda i, j: (i, 0), pipeline_mode=pl.Buffered(1)),
            _vec(layer, d),
            _layer(layer, (d, tf), lambda i, j: (0, j)),
            _layer(layer, (d, tf), lambda i, j: (0, nf + j)),
            _layer(layer, (tf, d), lambda i, j: (j, 0)),
            pl.BlockSpec((1, d), lambda i, j: (0, 0)),
        ],
        out_specs=pl.BlockSpec((tm, d), lambda i, j: (i, 0)),
        out_shape=jax.ShapeDtypeStruct((m, d), F32),
        scratch_shapes=[pltpu.VMEM((tm, d), BF16)],
        compiler_params=_params("arbitrary", "arbitrary"),
        name="ffn",
    )(x, g, w_in, w_in, w_out, g_final.reshape(1, d))


def _rope_pairs(x, table):
    y = x * table
    return y + pltpu.roll(y, MLA_ROPE, 1)


def _mla_prep_kernel(cq_ref, ckv_ref, kr_ref, tab_ref, tab_t_ref, gq_ref, gkv_ref, wq_t_ref, wk_ref, wv_t_ref,
                     q_t_ref, k_ref, v_t_ref):
    scale = (MLA_NOPE + MLA_ROPE) ** -0.5 * LOG2_E
    nq_t = jnp.transpose(_rms(cq_ref[...], gq_ref[...])).astype(BF16)
    q_t = _dot(wq_t_ref[...], nq_t) * scale
    nkv = _rms(ckv_ref[...], gkv_ref[...])
    kn = _dot(nkv.astype(BF16), wk_ref[...])
    v_t = _dot(wv_t_ref[...], jnp.transpose(nkv).astype(BF16))
    tab = tab_ref[...]
    tab_t = tab_t_ref[...]
    lane = lax.broadcasted_iota(jnp.int32, tab.shape, 1)
    k_pe = jnp.where(lane < MLA_ROPE, _rope_pairs(kr_ref[...], tab), 0.0).astype(BF16)
    zeros = jnp.zeros((MLA_QK_PAD - MLA_NOPE - MLA_ROPE, q_t.shape[1]), BF16)
    for h in range(MLA_HEADS):
        q0 = h * MLA_QK_PAD
        q_t_ref[0, h, 0:MLA_NOPE, :] = q_t[q0:q0 + MLA_NOPE].astype(BF16)
        pe = q_t[q0 + MLA_NOPE:q0 + MLA_QK_PAD] * tab_t
        q_t_ref[0, h, MLA_NOPE:MLA_NOPE + MLA_ROPE, :] = (pe[0:MLA_ROPE] + pe[MLA_ROPE:2 * MLA_ROPE]).astype(BF16)
        q_t_ref[0, h, MLA_NOPE + MLA_ROPE:MLA_QK_PAD, :] = zeros
        k_ref[0, h, :, 0:LANES] = kn[:, h * LANES:(h + 1) * LANES].astype(BF16)
        k_ref[0, h, :, LANES:2 * LANES] = k_pe
        v_t_ref[0, h] = v_t[h * MLA_V:(h + 1) * MLA_V].astype(BF16)


def mla_prep(z, table, g_q, g_kv, wq_t, wk, wv_t, layer, *, batch, seq, tm):
    assert seq % tm == 0
    nb = seq // tm
    r = MLA_RANK
    hw = MLA_HEADS
    tok = lambda width, col: pl.BlockSpec((tm, width), lambda i: (i, col // width))
    full = lambda a: _layer(layer, a.shape[1:], lambda i: (0, 0))
    feat_major = lambda width: pl.BlockSpec((1, hw, width, tm), lambda i: (i // nb, 0, 0, i % nb))
    return pl.pallas_call(
        _mla_prep_kernel,
        grid=(batch * nb,),
        in_specs=[
            tok(r, Z_CQ), tok(r, Z_CKV), tok(LANES, Z_KR),
            pl.BlockSpec((tm, LANES), lambda i: (i % nb, 0)),
            pl.BlockSpec((LANES, tm), lambda i: (0, i % nb)),
            full(g_q), full(g_kv), full(wq_t), full(wk), full(wv_t),
        ],
        out_specs=[
            feat_major(MLA_QK_PAD),
            pl.BlockSpec((1, hw, tm, MLA_QK_PAD), lambda i: (i // nb, 0, i % nb, 0)),
            feat_major(MLA_V),
        ],
        out_shape=[
            jax.ShapeDtypeStruct((batch, hw, MLA_QK_PAD, seq), BF16),
            jax.ShapeDtypeStruct((batch, hw, seq, MLA_QK_PAD), BF16),
            jax.ShapeDtypeStruct((batch, hw, MLA_V, seq), BF16),
        ],
        compiler_params=_params("arbitrary"),
        name="mla_prep",
    )(z, z, z, table, jnp.transpose(table), g_q, g_kv, wq_t, wk, wv_t)


def _mla_attn_kernel(q_t_ref, k_ref, v_t_ref, o_ref, m_ref, l_ref, acc_ref, *, tq, heads):
    qi = pl.program_id(2)
    m_ref[...] = jnp.full(m_ref.shape, -jnp.inf, F32)
    l_ref[...] = jnp.zeros(l_ref.shape, F32)
    acc_ref[...] = jnp.zeros(acc_ref.shape, F32)

    def step(j, masked):
        start = pl.multiple_of(j * tq, tq)
        for g in range(heads):
            s = _dot(k_ref[0, g, pl.ds(start, tq), :], q_t_ref[0, g])
            if masked:
                key = lax.broadcasted_iota(jnp.int32, s.shape, 0)
                qry = lax.broadcasted_iota(jnp.int32, s.shape, 1)
                s = jnp.where(key <= qry, s, -jnp.inf)
            m_old = m_ref[g]
            m_new = jnp.maximum(m_old, jnp.max(s, axis=0, keepdims=True))
            p = jnp.exp2(s - m_new)
            alpha = jnp.exp2(m_old - m_new)
            l_ref[g] = alpha * l_ref[g] + jnp.sum(p, axis=0, keepdims=True)
            acc_ref[g] = alpha * acc_ref[g] + _dot(v_t_ref[0, g, :, pl.ds(start, tq)], p.astype(BF16))
            m_ref[g] = m_new

    def body(j, carry):
        step(j, False)
        return carry

    lax.fori_loop(0, qi, body, 0)
    step(qi, True)
    for g in range(heads):
        o_t = acc_ref[g] / l_ref[g]
        o_ref[0, :, g * MLA_V:(g + 1) * MLA_V] = jnp.transpose(o_t).astype(o_ref.dtype)


def mla_attn(q_t, k, v_t, *, tq, heads):
    b, h, s, dq = k.shape
    dv = v_t.shape[2]
    assert s % tq == 0 and h % heads == 0
    return pl.pallas_call(
        functools.partial(_mla_attn_kernel, tq=tq, heads=heads),
        grid=(b, h // heads, s // tq),
        in_specs=[
            pl.BlockSpec((1, heads, dq, tq), lambda bi, hi, qi: (bi, hi, 0, qi)),
            pl.BlockSpec((1, heads, s, dq), lambda bi, hi, qi: (bi, hi, 0, 0)),
            pl.BlockSpec((1, heads, dv, s), lambda bi, hi, qi: (bi, hi, 0, 0)),
        ],
        out_specs=pl.BlockSpec((1, tq, heads * dv), lambda bi, hi, qi: (bi, qi, hi)),
        out_shape=jax.ShapeDtypeStruct((b, s, h * dv), BF16),
        scratch_shapes=[pltpu.VMEM((heads, 1, tq), F32), pltpu.VMEM((heads, 1, tq), F32),
                        pltpu.VMEM((heads, dv, tq), F32)],
        compiler_params=_params("arbitrary", "arbitrary", "arbitrary"),
        name="mla_attn",
    )(q_t, k, v_t)


def _gla_kernel(q_ref, k_ref, v_ref, r_ref, al_ref, wa_ref, ba_ref, ng_ref, o_ref,
                state_ref, b_scr, q_scr, k_scr, v_scr, oi_scr, *, tc):
    c = GLA_CHUNK
    pair_w = 2 * GLA_DK
    pair_v = 2 * GLA_DV
    n_pairs = GLA_HEADS // 2

    @pl.when(pl.program_id(1) == 0)
    def _():
        state_ref[...] = jnp.zeros(state_ref.shape, F32)

    row_i = lax.broadcasted_iota(jnp.int32, (c, c), 0)
    col_i = lax.broadcasted_iota(jnp.int32, (c, c), 1)
    tri = jnp.where(row_i >= col_i, 1.0, 0.0).astype(BF16)
    lane_k = lax.broadcasted_iota(jnp.int32, (pair_w, LANES), 0)
    head_sum = [jnp.where((lane_k // GLA_DK) == hh, 1.0, 0.0).astype(BF16) for hh in range(2)]
    srow = lax.broadcasted_iota(jnp.int32, (pair_w, pair_v), 0) // GLA_DK
    scol = lax.broadcasted_iota(jnp.int32, (pair_w, pair_v), 1) // GLA_DV
    own_head = srow == scol

    def chunk(ci, carry):
        r0 = pl.multiple_of(ci * c, c)
        rows = pl.ds(r0, c)
        x = _dot(al_ref[rows, :].astype(BF16), wa_ref[...]) + ba_ref[...]
        log_a = (jnp.minimum(x, 0.0) - jnp.log(1.0 + jnp.exp(-jnp.abs(x)))) * (1.0 / GLA_TAU)
        p1 = log_a.astype(BF16)
        r1 = log_a - p1.astype(F32)
        p2 = r1.astype(BF16)
        p3 = (r1 - p2.astype(F32)).astype(BF16)
        b_scr[...] = _dot(tri, p1) + _dot(tri, p2) + _dot(tri, p3)
        q_scr[...] = q_ref[rows, :] * (GLA_DK ** -0.5)
        k_scr[...] = k_ref[rows, :]
        v_scr[...] = v_ref[rows, :]

        for p in range(n_pairs):
            kl = slice(p * pair_w, (p + 1) * pair_w)
            vl = slice(p * pair_v, (p + 1) * pair_v)
            bp = b_scr[:, kl]
            qp = q_scr[:, kl]
            kp = k_scr[:, kl]
            vp = v_scr[:, vl]
            b_last = b_scr[c - 1:c, kl]
            st = state_ref[p]
            q_dec = (qp * jnp.exp(bp)).astype(BF16)
            k_dec = (kp * jnp.exp(b_last - bp)).astype(BF16)
            oi_scr[:, vl] = _dot(q_dec, st.astype(BF16))
            kv = lax.dot_general(k_dec, vp.astype(BF16), (((0,), (0,)), ((), ())),
                                 preferred_element_type=F32)
            decay_col = jnp.transpose(jnp.broadcast_to(jnp.exp(b_last), (LANES, pair_w)))
            decay_col = jnp.concatenate([decay_col, decay_col], axis=1)
            state_ref[p] = decay_col * st + jnp.where(own_head, kv, 0.0)
            for g in range(c // 8):
                t0 = 8 * g
                n = c - t0
                bt = b_scr[t0:c, kl]
                qt = q_scr[t0:c, kl]
                t_idx = t0 + lax.broadcasted_iota(jnp.int32, (n, pair_w), 0)
                ys = []
                for s in range(t0, t0 + 8):
                    e = jnp.exp(jnp.where(t_idx >= s, bt - b_scr[s:s + 1, kl], -jnp.inf))
                    ys.append(qt * e * k_scr[s:s + 1, kl])
                y = jnp.concatenate(ys, axis=0).astype(BF16)
                for hh in range(2):
                    hv = slice(p * pair_v + hh * GLA_DV, p * pair_v + (hh + 1) * GLA_DV)
                    a = _dot(y, head_sum[hh])
                    upd = a[0:n] * v_scr[t0:t0 + 1, hv]
                    for idx in range(1, 8):
                        upd += a[idx * n:(idx + 1) * n] * v_scr[t0 + idx:t0 + idx + 1, hv]
                    oi_scr[t0:c, hv] += upd

        o = oi_scr[...]
        gate = r_ref[rows, :]
        gate = gate * jax.nn.sigmoid(gate)
        for h in range(GLA_HEADS):
            hv = slice(h * GLA_DV, (h + 1) * GLA_DV)
            oh = o[:, hv]
            oh = oh * lax.rsqrt(jnp.mean(oh * oh, axis=-1, keepdims=True) + EPS)
            o_ref[rows, hv] = (oh * ng_ref[:, hv] * gate[:, hv]).astype(o_ref.dtype)
        return carry

    lax.fori_loop(0, tc // c, chunk, 0)


def gla(z, w_a2, b_a, norm_g, layer, *, batch, seq, tc):
    assert seq % tc == 0 and tc % GLA_CHUNK == 0
    nb = seq // tc
    kw = GLA_HEADS * GLA_DK
    vw = GLA_HEADS * GLA_DV
    tok = lambda width, col: pl.BlockSpec((tc, width), lambda b, i: (b * nb + i, col // width))
    full = lambda a: _layer(layer, a.shape[1:], lambda b, i: (0, 0))
    c = GLA_CHUNK
    return pl.pallas_call(
        functools.partial(_gla_kernel, tc=tc),
        grid=(batch, nb),
        in_specs=[
            tok(kw, Z_GQ), tok(kw, Z_GK), tok(vw, Z_GV), tok(vw, Z_GR), tok(LANES, Z_AL),
            full(w_a2), full(b_a), full(norm_g),
        ],
        out_specs=pl.BlockSpec((tc, vw), lambda b, i: (b * nb + i, 0)),
        out_shape=jax.ShapeDtypeStruct((batch * seq, vw), BF16),
        scratch_shapes=[
            pltpu.VMEM((GLA_HEADS // 2, 2 * GLA_DK, 2 * GLA_DV), F32),
            pltpu.VMEM((c, kw), F32), pltpu.VMEM((c, kw), F32), pltpu.VMEM((c, kw), F32),
            pltpu.VMEM((c, vw), F32), pltpu.VMEM((c, vw), F32),
        ],
        compiler_params=_params("arbitrary", "arbitrary"),
        name="gla",
    )(z, z, z, z, z, w_a2, b_a, norm_g)


def _conv_kernel(x_ref, halo_ref, w_ref, o_ref, *, blocks_per_seq):
    d = CONV_DIM
    x = x_ref[...]
    z = x[:, d:2 * d] * x[:, 2 * d:3 * d]
    hz = halo_ref[:, d:2 * d] * halo_ref[:, 2 * d:3 * d]
    hz = jnp.where(pl.program_id(0) % blocks_per_seq == 0, 0.0, hz)
    row = lax.broadcasted_iota(jnp.int32, z.shape, 0)
    z1 = jnp.where(row == 0, hz[7:8], pltpu.roll(z, 1, 0))
    z2 = jnp.where(row == 0, hz[6:7], jnp.where(row == 1, hz[7:8], pltpu.roll(z, 2, 0)))
    y = w_ref[2:3, :] * z + w_ref[1:2, :] * z1 + w_ref[0:1, :] * z2
    o_ref[...] = (x[:, 0:d] * y).astype(o_ref.dtype)


def short_conv(z, w, layer, *, seq, tm):
    m = z.shape[0]
    assert seq % tm == 0 and tm % 8 == 0
    width = 3 * CONV_DIM
    return pl.pallas_call(
        functools.partial(_conv_kernel, blocks_per_seq=seq // tm),
        grid=(m // tm,),
        in_specs=[
            pl.BlockSpec((tm, width), lambda i: (i, Z_CONV // width)),
            pl.BlockSpec((8, width), lambda i: (jnp.maximum(i * (tm // 8) - 1, 0), Z_CONV // width)),
            _layer(layer, w.shape[1:], lambda i: (0, 0)),
        ],
        out_specs=pl.BlockSpec((tm, CONV_DIM), lambda i: (i, 0)),
        out_shape=jax.ShapeDtypeStruct((m, CONV_DIM), BF16),
        compiler_params=_params("arbitrary"),
        name="short_conv",
    )(z, z, w)


def _merge_kernel(h_ref, g_ref, a0_ref, a1_ref, a2_ref, wg0_ref, wg1_ref, wg2_ref, b0_ref, b1_ref, b2_ref,
                  wp0_ref, wp1_ref, wp2_ref, wo_ref, o_ref, u_ref):
    j = pl.program_id(1)

    @pl.when(j == 0)
    def _():
        h = h_ref[...]
        u_ref[...] = _rms(h, g_ref[...]).astype(BF16)
        o_ref[...] = h

    u = u_ref[...]
    merged = None
    for a_ref, wg_ref, b_ref, wp_ref in ((a0_ref, wg0_ref, b0_ref, wp0_ref),
                                         (a1_ref, wg1_ref, b1_ref, wp1_ref),
                                         (a2_ref, wg2_ref, b2_ref, wp2_ref)):
        gate = jax.nn.sigmoid(_dot(u, wg_ref[...]) + b_ref[...])
        term = gate * _dot(a_ref[...], wp_ref[...])
        merged = term if merged is None else merged + term
    o_ref[...] += _dot(merged.astype(BF16), wo_ref[...])


def merge(h, g, branches, w_gate, b_gate, w_projs, w_out, layer, *, tm, tn):
    m, d = h.shape
    assert m % tm == 0 and d % tn == 0
    nj = d // tn
    row = lambda a: pl.BlockSpec((tm, a.shape[1]), lambda i, j: (i, 0))
    gate_w = lambda b: _layer(layer, (d, tn), lambda i, j: (0, b * nj + j))
    gate_b = lambda b: _layer(layer, (1, tn), lambda i, j: (0, b * nj + j))
    proj_w = lambda w: _layer(layer, (w.shape[1], tn), lambda i, j: (0, j))
    return pl.pallas_call(
        _merge_kernel,
        grid=(m // tm, nj),
        in_specs=[
            row(h), _vec(layer, d),
            row(branches[0]), row(branches[1]), row(branches[2]),
            gate_w(0), gate_w(1), gate_w(2), gate_b(0), gate_b(1), gate_b(2),
            proj_w(w_projs[0]), proj_w(w_projs[1]), proj_w(w_projs[2]),
            _layer(layer, (tn, d), lambda i, j: (j, 0)),
        ],
        out_specs=pl.BlockSpec((tm, d), lambda i, j: (i, 0)),
        out_shape=jax.ShapeDtypeStruct((m, d), F32),
        scratch_shapes=[pltpu.VMEM((tm, d), BF16)],
        compiler_params=_params("arbitrary", "arbitrary"),
        name="merge",
    )(h, g, *branches, w_gate, w_gate, w_gate, b_gate, b_gate, b_gate, *w_projs, w_out)


def _xattn_kernel(h_ref, g_ref, wq_ref, k_ref, v_ref, wo_ref, o_ref, hn_ref):
    j = pl.program_id(1)

    @pl.when(j == 0)
    def _():
        h = h_ref[...]
        hn_ref[...] = _rms(h, g_ref[...]).astype(BF16)
        o_ref[...] = h

    q = (_dot(hn_ref[...], wq_ref[...]) * (X_HEAD_DIM ** -0.5)).astype(BF16)
    s = _dot_t(q, k_ref[...])
    p = jnp.exp(s - jnp.max(s, axis=-1, keepdims=True))
    o = _dot(p.astype(BF16), v_ref[...]) / jnp.sum(p, axis=-1, keepdims=True)
    o_ref[...] += _dot(o.astype(BF16), wo_ref[...])


def xattn(h, g, w_q, kv, w_o, layer, *, seq, tm):
    m, d = h.shape
    assert seq % tm == 0
    nb = seq // tm
    hd = X_HEAD_DIM
    return pl.pallas_call(
        _xattn_kernel,
        grid=(m // tm, X_HEADS),
        in_specs=[
            pl.BlockSpec((tm, d), lambda i, j: (i, 0)),
            _vec(layer, d),
            _layer(layer, (d, hd), lambda i, j: (0, j)),
            pl.BlockSpec((MEM_LEN, hd), lambda i, j: (i // nb, j)),
            pl.BlockSpec((MEM_LEN, hd), lambda i, j: (i // nb, X_HEADS + j)),
            _layer(layer, (hd, d), lambda i, j: (j, 0)),
        ],
        out_specs=pl.BlockSpec((tm, d), lambda i, j: (i, 0)),
        out_shape=jax.ShapeDtypeStruct((m, d), F32),
        scratch_shapes=[pltpu.VMEM((tm, d), BF16)],
        compiler_params=_params("arbitrary", "arbitrary"),
        name="xattn",
    )(h, g, w_q, kv, kv, w_o)


def _pack_in_proj(w):
    c_q, c_kv, k_rope = w[..., 0:512], w[..., 512:1024], w[..., 1024:1088]
    g_q, g_k, g_v, g_r = w[..., 1088:1344], w[..., 1344:1600], w[..., 1600:2112], w[..., 2112:2624]
    a_low, conv = w[..., 2624:2640], w[..., 2640:4176]
    half = MLA_ROPE // 2
    k_swap = jnp.concatenate([k_rope[..., half:], k_rope[..., :half]], axis=-1)
    pad = jnp.zeros(w.shape[:-1] + (LANES - GLA_RANK,), w.dtype)
    packed = jnp.concatenate([conv, c_q, c_kv, g_v, g_r, g_q, g_k, k_rope, k_swap, a_low, pad], axis=-1)
    return packed.astype(BF16), w[..., 4176:].astype(BF16)


def _pack_uq(w):
    depth = w.shape[0]
    w = w.reshape(depth, MLA_RANK, MLA_HEADS, MLA_NOPE + MLA_ROPE)
    half = MLA_ROPE // 2
    nope, pe = w[..., :MLA_NOPE], w[..., MLA_NOPE:]
    swap = jnp.concatenate([pe[..., half:], pe[..., :half]], axis=-1)
    packed = jnp.concatenate([nope, pe, swap], axis=-1).reshape(depth, MLA_RANK, MLA_HEADS * MLA_QK_PAD)
    return jnp.swapaxes(packed, 1, 2).astype(BF16)


def _pack_ukv(w):
    depth = w.shape[0]
    w = w.reshape(depth, MLA_RANK, MLA_HEADS, MLA_NOPE + MLA_V)
    wk = w[..., :MLA_NOPE].reshape(depth, MLA_RANK, MLA_HEADS * MLA_NOPE)
    wv = w[..., MLA_NOPE:].reshape(depth, MLA_RANK, MLA_HEADS * MLA_V)
    return wk.astype(BF16), jnp.swapaxes(wv, 1, 2).astype(BF16)


def _rope_table(seq):
    inv_freq = 1.0 / (ROPE_THETA ** (jnp.arange(0, MLA_ROPE, 2, dtype=F32) / MLA_ROPE))
    ang = jnp.arange(seq, dtype=F32)[:, None] * inv_freq[None, :]
    cos, sin = jnp.cos(ang), jnp.sin(ang)
    return jnp.concatenate([cos, cos, -sin, sin], axis=1)


def kernel(x, mem, ffn1_norm, ffn1_w_in, ffn1_w_out, mix_norm, mix_w_in, mix_b_gate, mla_q_norm, mla_kv_norm,
           mla_w_uq, mla_w_ukv, mla_w_proj, gla_w_a2, gla_b_a, gla_norm, gla_w_proj, conv_w, conv_w_proj,
           mix_w_out, xattn_norm, mem_norm, xattn_w_q, xattn_w_kv, xattn_w_o, ffn2_norm, ffn2_w_in,
           ffn2_w_out, final_norm):
    batch, seq, d = x.shape
    depth = ffn1_norm.shape[0]
    tokens = batch * seq
    table = _rope_table(seq)
    h = x.reshape(tokens, d)
    mem2 = mem.reshape(batch * mem.shape[1], d)
    bf = lambda w: w.astype(BF16)
    vec = lambda p: p.reshape(depth, 1, -1)

    ffn1_in, ffn1_out, ffn2_in, ffn2_out = bf(ffn1_w_in), bf(ffn1_w_out), bf(ffn2_w_in), bf(ffn2_w_out)
    w_in, w_gate = _pack_in_proj(mix_w_in)
    wq_t = _pack_uq(mla_w_uq)
    wk, wv_t = _pack_ukv(mla_w_ukv)
    w_a2 = bf(jnp.concatenate(
        [gla_w_a2, jnp.zeros((depth, LANES - GLA_RANK, GLA_HEADS * GLA_DK), F32)], axis=1))
    w_projs = (bf(mla_w_proj), bf(gla_w_proj), bf(conv_w_proj))
    w_mix_out, w_xq, w_xo = bf(mix_w_out), bf(xattn_w_q), bf(xattn_w_o)
    conv_taps = conv_w.reshape(depth, 3, CONV_DIM)

    for l in range(depth):
        h = ffn(h, vec(ffn1_norm), ffn1_in, ffn1_out, final_norm, l, tm=1024, tf=512, final_norm=False)

        z = norm_matmul(h, vec(mix_norm), w_in, l, tm=512, tn=Z_WIDTH // 2, out_dtype=F32)
        q_t, k, v_t = mla_prep(z, table, vec(mla_q_norm), vec(mla_kv_norm), wq_t, wk, wv_t, l,
                               batch=batch, seq=seq, tm=512)
        a_mla = mla_attn(q_t, k, v_t, tq=512, heads=2).reshape(tokens, MLA_HEADS * MLA_V)
        a_gla = gla(z, w_a2, vec(gla_b_a), vec(gla_norm), l, batch=batch, seq=seq, tc=512)
        a_conv = short_conv(z, conv_taps, l, seq=seq, tm=512)
        h = merge(h, vec(mix_norm), (a_mla, a_gla, a_conv), w_gate, vec(mix_b_gate), w_projs, w_mix_out, l,
                  tm=512, tn=512)

        kv = norm_matmul(mem2, vec(mem_norm), xattn_w_kv, l, tm=mem2.shape[0], tn=1024, out_dtype=BF16)
        h = xattn(h, vec(xattn_norm), w_xq, kv, w_xo, l, seq=seq, tm=512)

        h = ffn(h, vec(ffn2_norm), ffn2_in, ffn2_out, final_norm, l, tm=1024, tf=512,
                final_norm=(l == depth - 1))
    return h.reshape(batch, seq, d)
```

```python
import functools

import jax
import jax.numpy as jnp
from jax import lax
from jax.experimental import pallas as pl
from jax.experimental.pallas import tpu as pltpu

F32 = jnp.float32
BF16 = jnp.bfloat16

EPS = 1e-6
D_MODEL = 2048
D_FF = 5632
MLA_HEADS = 8
MLA_RANK = 512
MLA_NOPE = 128
MLA_ROPE = 64
MLA_V = 128
MLA_QK_PAD = 256
ROPE_THETA = 10000.0
LOG2_E = 1.4426950408889634
GLA_HEADS = 4
GLA_DK = 64
GLA_DV = 128
GLA_RANK = 16
GLA_TAU = 16.0
GLA_CHUNK = 64
GLA_SUBCHUNK = 16
CONV_DIM = 512
X_HEADS = 4
X_HEAD_DIM = D_MODEL // X_HEADS
MEM_LEN = 256

LANES = 128
VMEM_LIMIT_BYTES = 56 * 1024 * 1024

Z_CONV = 0
Z_CQ = 1536
Z_CKV = 2048
Z_GV = 2560
Z_GR = 3072
Z_GQ = 3584
Z_GK = 3840
Z_KR = 4096
Z_AL = 4224
Z_WIDTH = 4352


def _params(*sem):
    return pltpu.CompilerParams(dimension_semantics=sem, vmem_limit_bytes=VMEM_LIMIT_BYTES)


def _rms(x, g):
    return x * lax.rsqrt(jnp.mean(x * x, axis=-1, keepdims=True) + EPS) * g


def _dot(a, b):
    return jnp.dot(a, b, preferred_element_type=F32)


def _dot_t(a, b):
    return lax.dot_general(a, b, (((1,), (1,)), ((), ())), preferred_element_type=F32)


def _norm_matmul_kernel(x_ref, g_ref, w_ref, o_ref):
    xn = _rms(x_ref[...], g_ref[...]).astype(BF16)
    o_ref[...] = _dot(xn, w_ref[...].astype(BF16)).astype(o_ref.dtype)


def _layer(layer, block, index):
    return pl.BlockSpec((None,) + tuple(block), lambda *ids: (layer,) + tuple(index(*ids)))


def _vec(layer, width):
    return _layer(layer, (1, width), lambda *ids: (0, 0))


def norm_matmul(x, g, w, layer, *, tm, tn, out_dtype):
    m, k = x.shape
    n = w.shape[2]
    assert m % tm == 0 and n % tn == 0
    return pl.pallas_call(
        _norm_matmul_kernel,
        grid=(n // tn, m // tm),
        in_specs=[
            pl.BlockSpec((tm, k), lambda j, i: (i, 0)),
            _vec(layer, k),
            _layer(layer, (k, tn), lambda j, i: (0, j)),
        ],
        out_specs=pl.BlockSpec((tm, tn), lambda j, i: (i, j)),
        out_shape=jax.ShapeDtypeStruct((m, n), out_dtype),
        compiler_params=_params("arbitrary", "arbitrary"),
        name="norm_matmul",
    )(x, g, w)


def _ffn_kernel(x_ref, g_ref, wg_ref, wu_ref, wo_ref, gf_ref, o_ref, xn_ref, *, final_norm):
    j = pl.program_id(1)

    @pl.when(j == 0)
    def _():
        x = x_ref[...]
        xn_ref[...] = _rms(x, g_ref[...]).astype(BF16)
        o_ref[...] = x

    xn = xn_ref[...]
    gate = _dot(xn, wg_ref[...].astype(BF16))
    up = _dot(xn, wu_ref[...].astype(BF16))
    act = (0.5 * gate) * jax.nn.sigmoid(gate) * up
    o_ref[...] += _dot(act.astype(BF16), wo_ref[...].astype(BF16))

    if final_norm:
        @pl.when(j == pl.num_programs(1) - 1)
        def _():
            o_ref[...] = _rms(o_ref[...], gf_ref[...])


def ffn(x, g, w_in, w_out, g_final, layer, *, tm, tf, final_norm):
    m, d = x.shape
    f = w_out.shape[1]
    assert m % tm == 0 and f % tf == 0
    nf = f // tf
    return pl.pallas_call(
        functools.partial(_ffn_kernel, final_norm=final_norm),
        grid=(m // tm, nf),
        in_specs=[
            pl.BlockSpec((tm, d), lambda i, j: (i, 0), pipeline_mode=pl.Buffered(1)),
            _vec(layer, d),
            _layer(layer, (d, tf), lambda i, j: (0, j)),
            _layer(layer, (d, tf), lambda i, j: (0, nf + j)),
            _layer(layer, (tf, d), lambda i, j: (j, 0)),
            pl.BlockSpec((1, d), lambda i, j: (0, 0)),
        ],
        out_specs=pl.BlockSpec((tm, d), lambda i, j: (i, 0)),
        out_shape=jax.ShapeDtypeStruct((m, d), F32),
        scratch_shapes=[pltpu.VMEM((tm, d), BF16)],
        compiler_params=_params("arbitrary", "arbitrary"),
        name="ffn",
    )(x, g, w_in, w_in, w_out, g_final.reshape(1, d))


def _rope_pairs(x, table):
    y = x * table
    return y + pltpu.roll(y, MLA_ROPE, 1)


def _mla_prep_kernel(cq_ref, ckv_ref, kr_ref, tab_ref, tab_t_ref, gq_ref, gkv_ref, wq_t_ref, wk_ref, wv_t_ref,
                     q_t_ref, k_ref, v_t_ref):
    scale = (MLA_NOPE + MLA_ROPE) ** -0.5 * LOG2_E
    nq_t = jnp.transpose(_rms(cq_ref[...], gq_ref[...])).astype(BF16)
    q_t = _dot(wq_t_ref[...], nq_t) * scale
    nkv = _rms(ckv_ref[...], gkv_ref[...])
    kn = _dot(nkv.astype(BF16), wk_ref[...])
    v_t = _dot(wv_t_ref[...], jnp.transpose(nkv).astype(BF16))
    tab = tab_ref[...]
    tab_t = tab_t_ref[...]
    lane = lax.broadcasted_iota(jnp.int32, tab.shape, 1)
    k_pe = jnp.where(lane < MLA_ROPE, _rope_pairs(kr_ref[...], tab), 0.0).astype(BF16)
    zeros = jnp.zeros((MLA_QK_PAD - MLA_NOPE - MLA_ROPE, q_t.shape[1]), BF16)
    for h in range(MLA_HEADS):
        q0 = h * MLA_QK_PAD
        q_t_ref[0, h, 0:MLA_NOPE, :] = q_t[q0:q0 + MLA_NOPE].astype(BF16)
        pe = q_t[q0 + MLA_NOPE:q0 + MLA_QK_PAD] * tab_t
        q_t_ref[0, h, MLA_NOPE:MLA_NOPE + MLA_ROPE, :] = (pe[0:MLA_ROPE] + pe[MLA_ROPE:2 * MLA_ROPE]).astype(BF16)
        q_t_ref[0, h, MLA_NOPE + MLA_ROPE:MLA_QK_PAD, :] = zeros
        k_ref[0, h, :, 0:LANES] = kn[:, h * LANES:(h + 1) * LANES].astype(BF16)
        k_ref[0, h, :, LANES:2 * LANES] = k_pe
        v_t_ref[0, h] = v_t[h * MLA_V:(h + 1) * MLA_V].astype(BF16)


def mla_prep(z, table, g_q, g_kv, wq_t, wk, wv_t, layer, *, batch, seq, tm):
    assert seq % tm == 0
    nb = seq // tm
    r = MLA_RANK
    hw = MLA_HEADS
    tok = lambda width, col: pl.BlockSpec((tm, width), lambda i: (i, col // width))
    full = lambda a: _layer(layer, a.shape[1:], lambda i: (0, 0))
    feat_major = lambda width: pl.BlockSpec((1, hw, width, tm), lambda i: (i // nb, 0, 0, i % nb))
    return pl.pallas_call(
        _mla_prep_kernel,
        grid=(batch * nb,),
        in_specs=[
            tok(r, Z_CQ), tok(r, Z_CKV), tok(LANES, Z_KR),
            pl.BlockSpec((tm, LANES), lambda i: (i % nb, 0)),
            pl.BlockSpec((LANES, tm), lambda i: (0, i % nb)),
            full(g_q), full(g_kv), full(wq_t), full(wk), full(wv_t),
        ],
        out_specs=[
            feat_major(MLA_QK_PAD),
            pl.BlockSpec((1, hw, tm, MLA_QK_PAD), lambda i: (i // nb, 0, i % nb, 0)),
            feat_major(MLA_V),
        ],
        out_shape=[
            jax.ShapeDtypeStruct((batch, hw, MLA_QK_PAD, seq), BF16),
            jax.ShapeDtypeStruct((batch, hw, seq, MLA_QK_PAD), BF16),
            jax.ShapeDtypeStruct((batch, hw, MLA_V, seq), BF16),
        ],
        compiler_params=_params("arbitrary"),
        name="mla_prep",
    )(z, z, z, table, jnp.transpose(table), g_q, g_kv, wq_t, wk, wv_t)


def _mla_attn_kernel(q_t_ref, k_ref, v_t_ref, o_ref, m_ref, l_ref, acc_ref, *, tq, heads):
    qi = pl.program_id(2)
    m_ref[...] = jnp.full(m_ref.shape, -jnp.inf, F32)
    l_ref[...] = jnp.zeros(l_ref.shape, F32)
    acc_ref[...] = jnp.zeros(acc_ref.shape, F32)

    def step(j, masked):
        start = pl.multiple_of(j * tq, tq)
        scores = [_dot(k_ref[0, g, pl.ds(start, tq), :], q_t_ref[0, g]) for g in range(heads)]
        for g in range(heads):
            s = scores[g]
            if masked:
                key = lax.broadcasted_iota(jnp.int32, s.shape, 0)
                qry = lax.broadcasted_iota(jnp.int32, s.shape, 1)
                s = jnp.where(key <= qry, s, -jnp.inf)
            m_old = m_ref[g]
            m_new = jnp.maximum(m_old, jnp.max(s, axis=0, keepdims=True))
            p = jnp.exp2(s - m_new)
            alpha = jnp.exp2(m_old - m_new)
            l_ref[g] = alpha * l_ref[g] + jnp.sum(p, axis=0, keepdims=True)
            acc_ref[g] = alpha * acc_ref[g] + _dot(v_t_ref[0, g, :, pl.ds(start, tq)], p.astype(BF16))
            m_ref[g] = m_new

    def body(j, carry):
        step(j, False)
        return carry

    lax.fori_loop(0, qi, body, 0)
    step(qi, True)
    for g in range(heads):
        o_t = acc_ref[g] / l_ref[g]
        o_ref[0, :, g * MLA_V:(g + 1) * MLA_V] = jnp.transpose(o_t).astype(o_ref.dtype)


def mla_attn(q_t, k, v_t, *, tq, heads):
    b, h, s, dq = k.shape
    dv = v_t.shape[2]
    assert s % tq == 0 and h % heads == 0
    return pl.pallas_call(
        functools.partial(_mla_attn_kernel, tq=tq, heads=heads),
        grid=(b, h // heads, s // tq),
        in_specs=[
            pl.BlockSpec((1, heads, dq, tq), lambda bi, hi, qi: (bi, hi, 0, qi)),
            pl.BlockSpec((1, heads, s, dq), lambda bi, hi, qi: (bi, hi, 0, 0)),
            pl.BlockSpec((1, heads, dv, s), lambda bi, hi, qi: (bi, hi, 0, 0)),
        ],
        out_specs=pl.BlockSpec((1, tq, heads * dv), lambda bi, hi, qi: (bi, qi, hi)),
        out_shape=jax.ShapeDtypeStruct((b, s, h * dv), BF16),
        scratch_shapes=[pltpu.VMEM((heads, 1, tq), F32), pltpu.VMEM((heads, 1, tq), F32),
                        pltpu.VMEM((heads, dv, tq), F32)],
        compiler_params=_params("arbitrary", "arbitrary", "arbitrary"),
        name="mla_attn",
    )(q_t, k, v_t)


def _gla_kernel(q_ref, k_ref, v_ref, r_ref, al_ref, wa_ref, ba_ref, ng_ref, o_ref,
                state_ref, b_scr, q_scr, k_scr, v_scr, oi_scr, *, tc):
    c = GLA_CHUNK
    pair_w = 2 * GLA_DK
    pair_v = 2 * GLA_DV
    n_pairs = GLA_HEADS // 2

    @pl.when(pl.program_id(1) == 0)
    def _():
        state_ref[...] = jnp.zeros(state_ref.shape, F32)

    row_i = lax.broadcasted_iota(jnp.int32, (c, c), 0)
    col_i = lax.broadcasted_iota(jnp.int32, (c, c), 1)
    tri = jnp.where(row_i >= col_i, 1.0, 0.0).astype(BF16)
    lane_k = lax.broadcasted_iota(jnp.int32, (pair_w, LANES), 0)
    head_sum = [jnp.where((lane_k // GLA_DK) == hh, 1.0, 0.0).astype(BF16) for hh in range(2)]
    sub = GLA_SUBCHUNK
    key_row = lax.broadcasted_iota(jnp.int32, (c, pair_w), 0)
    lane_h = lax.broadcasted_iota(jnp.int32, (1, pair_w), 1) // GLA_DK
    head_lanes = [jnp.where(lane_h == hh, 1.0, 0.0) for hh in range(2)]
    srow = lax.broadcasted_iota(jnp.int32, (pair_w, pair_v), 0) // GLA_DK
    scol = lax.broadcasted_iota(jnp.int32, (pair_w, pair_v), 1) // GLA_DV
    own_head = srow == scol

    def chunk(ci, carry):
        r0 = pl.multiple_of(ci * c, c)
        rows = pl.ds(r0, c)
        x = _dot(al_ref[rows, :].astype(BF16), wa_ref[...]) + ba_ref[...]
        log_a = (jnp.minimum(x, 0.0) - jnp.log(1.0 + jnp.exp(-jnp.abs(x)))) * (1.0 / GLA_TAU)
        p1 = log_a.astype(BF16)
        r1 = log_a - p1.astype(F32)
        p2 = r1.astype(BF16)
        p3 = (r1 - p2.astype(F32)).astype(BF16)
        b_scr[...] = _dot(tri, p1) + _dot(tri, p2) + _dot(tri, p3)
        q_scr[...] = q_ref[rows, :] * (GLA_DK ** -0.5)
        k_scr[...] = k_ref[rows, :]
        v_scr[...] = v_ref[rows, :]

        staged = []
        for p in range(n_pairs):
            kl = slice(p * pair_w, (p + 1) * pair_w)
            vl = slice(p * pair_v, (p + 1) * pair_v)
            bp = b_scr[:, kl]
            qp = q_scr[:, kl]
            kp = k_scr[:, kl]
            vp = v_scr[:, vl]
            b_last = b_scr[c - 1:c, kl]
            st = state_ref[p]
            q_dec = (qp * jnp.exp(bp)).astype(BF16)
            k_dec = (kp * jnp.exp(b_last - bp)).astype(BF16)
            o_inter = _dot(q_dec, st.astype(BF16))
            vp_bf = vp.astype(BF16)
            kv = lax.dot_general(k_dec, vp_bf, (((0,), (0,)), ((), ())), preferred_element_type=F32)
            decay_col = jnp.transpose(jnp.broadcast_to(jnp.exp(b_last), (LANES, pair_w)))
            decay_col = jnp.concatenate([decay_col, decay_col], axis=1)
            state_ref[p] = decay_col * st + jnp.where(own_head, kv, 0.0)
            q_blocks, k_blocks = [], []
            for i in range(1, c // sub):
                r = i * sub
                b_r = b_scr[r:r + 1, kl]
                q_i = q_scr[r:r + sub, kl] * jnp.exp(b_scr[r:r + sub, kl] - b_r)
                pieces = [jnp.zeros((r, pair_w), F32), q_i]
                if c - r - sub:
                    pieces.append(jnp.zeros((c - r - sub, pair_w), F32))
                q_blocks.append(jnp.concatenate(pieces, axis=0))
                k_blocks.append(kp * jnp.exp(jnp.where(key_row < r, b_r - bp, -jnp.inf)))
            q_cat = jnp.concatenate(q_blocks, axis=1)
            k_cat = jnp.concatenate(k_blocks, axis=1).astype(BF16)
            a_far = [_dot_t((q_cat * jnp.concatenate([head_lanes[hh]] * len(q_blocks), axis=1)).astype(BF16), k_cat)
                     for hh in range(2)]
            ys = []
            for i in range(c // sub):
                r = i * sub
                for t0 in (r, r + 8):
                    n = r + sub - t0
                    bt = b_scr[t0:r + sub, kl]
                    qt = q_scr[t0:r + sub, kl]
                    t_idx = t0 + lax.broadcasted_iota(jnp.int32, (n, pair_w), 0)
                    for s in range(t0, t0 + 8):
                        e = jnp.exp(jnp.where(t_idx >= s, bt - b_scr[s:s + 1, kl], -jnp.inf))
                        ys.append(qt * e * k_scr[s:s + 1, kl])
            y = jnp.concatenate(ys, axis=0).astype(BF16)
            a_near = [_dot(y, head_sum[hh]) for hh in range(2)]
            staged.append((o_inter, vp_bf, a_far, a_near))

        for p in range(n_pairs):
            o_inter, vp_bf, a_far, a_near = staged[p]
            for hh in range(2):
                hv = slice(p * pair_v + hh * GLA_DV, p * pair_v + (hh + 1) * GLA_DV)
                o_h = o_inter[:, hh * GLA_DV:(hh + 1) * GLA_DV] + _dot(a_far[hh].astype(BF16),
                                                                       vp_bf[:, hh * GLA_DV:(hh + 1) * GLA_DV])
                a = a_near[hh]
                for i in range(c // sub):
                    r = i * sub
                    base = i * (8 * sub + 8 * 8)
                    upd = a[base:base + sub] * v_scr[r:r + 1, hv]
                    for idx in range(1, 8):
                        upd += a[base + idx * sub:base + (idx + 1) * sub] * v_scr[r + idx:r + idx + 1, hv]
                    base += 8 * sub
                    upd8 = a[base:base + 8] * v_scr[r + 8:r + 9, hv]
                    for idx in range(1, 8):
                        upd8 += a[base + idx * 8:base + (idx + 1) * 8] * v_scr[r + 8 + idx:r + 9 + idx, hv]
                    upd = jnp.concatenate([upd[0:8], upd[8:sub] + upd8], axis=0)
                    oi_scr[r:r + sub, hv] = upd + o_h[r:r + sub]

        o = oi_scr[...]
        gate = r_ref[rows, :]
        gate = gate * jax.nn.sigmoid(gate)
        for h in range(GLA_HEADS):
            hv = slice(h * GLA_DV, (h + 1) * GLA_DV)
            oh = o[:, hv]
            oh = oh * lax.rsqrt(jnp.mean(oh * oh, axis=-1, keepdims=True) + EPS)
            o_ref[rows, hv] = (oh * ng_ref[:, hv] * gate[:, hv]).astype(o_ref.dtype)
        return carry

    lax.fori_loop(0, tc // c, chunk, 0)


def gla(z, w_a2, b_a, norm_g, layer, *, batch, seq, tc):
    assert seq % tc == 0 and tc % GLA_CHUNK == 0
    nb = seq // tc
    kw = GLA_HEADS * GLA_DK
    vw = GLA_HEADS * GLA_DV
    tok = lambda width, col: pl.BlockSpec((tc, width), lambda b, i: (b * nb + i, col // width))
    full = lambda a: _layer(layer, a.shape[1:], lambda b, i: (0, 0))
    c = GLA_CHUNK
    return pl.pallas_call(
        functools.partial(_gla_kernel, tc=tc),
        grid=(batch, nb),
        in_specs=[
            tok(kw, Z_GQ), tok(kw, Z_GK), tok(vw, Z_GV), tok(vw, Z_GR), tok(LANES, Z_AL),
            full(w_a2), full(b_a), full(norm_g),
        ],
        out_specs=pl.BlockSpec((tc, vw), lambda b, i: (b * nb + i, 0)),
        out_shape=jax.ShapeDtypeStruct((batch * seq, vw), BF16),
        scratch_shapes=[
            pltpu.VMEM((GLA_HEADS // 2, 2 * GLA_DK, 2 * GLA_DV), F32),
            pltpu.VMEM((c, kw), F32), pltpu.VMEM((c, kw), F32), pltpu.VMEM((c, kw), F32),
            pltpu.VMEM((c, vw), F32), pltpu.VMEM((c, vw), F32),
        ],
        compiler_params=_params("arbitrary", "arbitrary"),
        name="gla",
    )(z, z, z, z, z, w_a2, b_a, norm_g)


def _conv_kernel(x_ref, halo_ref, w_ref, o_ref, *, blocks_per_seq):
    d = CONV_DIM
    x = x_ref[...]
    z = x[:, d:2 * d] * x[:, 2 * d:3 * d]
    hz = halo_ref[:, d:2 * d] * halo_ref[:, 2 * d:3 * d]
    hz = jnp.where(pl.program_id(0) % blocks_per_seq == 0, 0.0, hz)
    row = lax.broadcasted_iota(jnp.int32, z.shape, 0)
    z1 = jnp.where(row == 0, hz[7:8], pltpu.roll(z, 1, 0))
    z2 = jnp.where(row == 0, hz[6:7], jnp.where(row == 1, hz[7:8], pltpu.roll(z, 2, 0)))
    y = w_ref[2:3, :] * z + w_ref[1:2, :] * z1 + w_ref[0:1, :] * z2
    o_ref[...] = (x[:, 0:d] * y).astype(o_ref.dtype)


def short_conv(z, w, layer, *, seq, tm):
    m = z.shape[0]
    assert seq % tm == 0 and tm % 8 == 0
    width = 3 * CONV_DIM
    return pl.pallas_call(
        functools.partial(_conv_kernel, blocks_per_seq=seq // tm),
        grid=(m // tm,),
        in_specs=[
            pl.BlockSpec((tm, width), lambda i: (i, Z_CONV // width)),
            pl.BlockSpec((8, width), lambda i: (jnp.maximum(i * (tm // 8) - 1, 0), Z_CONV // width)),
            _layer(layer, w.shape[1:], lambda i: (0, 0)),
        ],
        out_specs=pl.BlockSpec((tm, CONV_DIM), lambda i: (i, 0)),
        out_shape=jax.ShapeDtypeStruct((m, CONV_DIM), BF16),
        compiler_params=_params("arbitrary"),
        name="short_conv",
    )(z, z, w)


def _merge_kernel(h_ref, g_ref, a0_ref, a1_ref, a2_ref, wg0_ref, wg1_ref, wg2_ref, b0_ref, b1_ref, b2_ref,
                  wp0_ref, wp1_ref, wp2_ref, wo_ref, o_ref, u_ref):
    j = pl.program_id(1)

    @pl.when(j == 0)
    def _():
        h = h_ref[...]
        u_ref[...] = _rms(h, g_ref[...]).astype(BF16)
        o_ref[...] = h

    u = u_ref[...]
    merged = None
    for a_ref, wg_ref, b_ref, wp_ref in ((a0_ref, wg0_ref, b0_ref, wp0_ref),
                                         (a1_ref, wg1_ref, b1_ref, wp1_ref),
                                         (a2_ref, wg2_ref, b2_ref, wp2_ref)):
        gate = jax.nn.sigmoid(_dot(u, wg_ref[...]) + b_ref[...])
        term = gate * _dot(a_ref[...], wp_ref[...])
        merged = term if merged is None else merged + term
    o_ref[...] += _dot(merged.astype(BF16), wo_ref[...])


def merge(h, g, branches, w_gate, b_gate, w_projs, w_out, layer, *, tm, tn):
    m, d = h.shape
    assert m % tm == 0 and d % tn == 0
    nj = d // tn
    row = lambda a: pl.BlockSpec((tm, a.shape[1]), lambda i, j: (i, 0))
    gate_w = lambda b: _layer(layer, (d, tn), lambda i, j: (0, b * nj + j))
    gate_b = lambda b: _layer(layer, (1, tn), lambda i, j: (0, b * nj + j))
    proj_w = lambda w: _layer(layer, (w.shape[1], tn), lambda i, j: (0, j))
    return pl.pallas_call(
        _merge_kernel,
        grid=(m // tm, nj),
        in_specs=[
            row(h), _vec(layer, d),
            row(branches[0]), row(branches[1]), row(branches[2]),
            gate_w(0), gate_w(1), gate_w(2), gate_b(0), gate_b(1), gate_b(2),
            proj_w(w_projs[0]), proj_w(w_projs[1]), proj_w(w_projs[2]),
            _layer(layer, (tn, d), lambda i, j: (j, 0)),
        ],
        out_specs=pl.BlockSpec((tm, d), lambda i, j: (i, 0)),
        out_shape=jax.ShapeDtypeStruct((m, d), F32),
        scratch_shapes=[pltpu.VMEM((tm, d), BF16)],
        compiler_params=_params("arbitrary", "arbitrary"),
        name="merge",
    )(h, g, *branches, w_gate, w_gate, w_gate, b_gate, b_gate, b_gate, *w_projs, w_out)


def _xattn_kernel(h_ref, g_ref, wq_ref, k_ref, v_ref, wo_ref, o_ref, hn_ref):
    j = pl.program_id(1)

    @pl.when(j == 0)
    def _():
        h = h_ref[...]
        hn_ref[...] = _rms(h, g_ref[...]).astype(BF16)
        o_ref[...] = h

    q = (_dot(hn_ref[...], wq_ref[...]) * (X_HEAD_DIM ** -0.5)).astype(BF16)
    s = _dot_t(q, k_ref[...])
    p = jnp.exp(s - jnp.max(s, axis=-1, keepdims=True))
    o = _dot(p.astype(BF16), v_ref[...]) / jnp.sum(p, axis=-1, keepdims=True)
    o_ref[...] += _dot(o.astype(BF16), wo_ref[...])


def xattn(h, g, w_q, kv, w_o, layer, *, seq, tm):
    m, d = h.shape
    assert seq % tm == 0
    nb = seq // tm
    hd = X_HEAD_DIM
    return pl.pallas_call(
        _xattn_kernel,
        grid=(m // tm, X_HEADS),
        in_specs=[
            pl.BlockSpec((tm, d), lambda i, j: (i, 0)),
            _vec(layer, d),
            _layer(layer, (d, hd), lambda i, j: (0, j)),
            pl.BlockSpec((MEM_LEN, hd), lambda i, j: (i // nb, j)),
            pl.BlockSpec((MEM_LEN, hd), lambda i, j: (i // nb, X_HEADS + j)),
            _layer(layer, (hd, d), lambda i, j: (j, 0)),
        ],
        out_specs=pl.BlockSpec((tm, d), lambda i, j: (i, 0)),
        out_shape=jax.ShapeDtypeStruct((m, d), F32),
        scratch_shapes=[pltpu.VMEM((tm, d), BF16)],
        compiler_params=_params("arbitrary", "arbitrary"),
        name="xattn",
    )(h, g, w_q, kv, kv, w_o)


def _pack_in_proj(w):
    c_q, c_kv, k_rope = w[..., 0:512], w[..., 512:1024], w[..., 1024:1088]
    g_q, g_k, g_v, g_r = w[..., 1088:1344], w[..., 1344:1600], w[..., 1600:2112], w[..., 2112:2624]
    a_low, conv = w[..., 2624:2640], w[..., 2640:4176]
    half = MLA_ROPE // 2
    k_swap = jnp.concatenate([k_rope[..., half:], k_rope[..., :half]], axis=-1)
    pad = jnp.zeros(w.shape[:-1] + (LANES - GLA_RANK,), w.dtype)
    packed = jnp.concatenate([conv, c_q, c_kv, g_v, g_r, g_q, g_k, k_rope, k_swap, a_low, pad], axis=-1)
    return packed.astype(BF16), w[..., 4176:].astype(BF16)


def _pack_uq(w):
    depth = w.shape[0]
    w = w.reshape(depth, MLA_RANK, MLA_HEADS, MLA_NOPE + MLA_ROPE)
    half = MLA_ROPE // 2
    nope, pe = w[..., :MLA_NOPE], w[..., MLA_NOPE:]
    swap = jnp.concatenate([pe[..., half:], pe[..., :half]], axis=-1)
    packed = jnp.concatenate([nope, pe, swap], axis=-1).reshape(depth, MLA_RANK, MLA_HEADS * MLA_QK_PAD)
    return jnp.swapaxes(packed, 1, 2).astype(BF16)


def _pack_ukv(w):
    depth = w.shape[0]
    w = w.reshape(depth, MLA_RANK, MLA_HEADS, MLA_NOPE + MLA_V)
    wk = w[..., :MLA_NOPE].reshape(depth, MLA_RANK, MLA_HEADS * MLA_NOPE)
    wv = w[..., MLA_NOPE:].reshape(depth, MLA_RANK, MLA_HEADS * MLA_V)
    return wk.astype(BF16), jnp.swapaxes(wv, 1, 2).astype(BF16)


def _rope_table(seq):
    inv_freq = 1.0 / (ROPE_THETA ** (jnp.arange(0, MLA_ROPE, 2, dtype=F32) / MLA_ROPE))
    ang = jnp.arange(seq, dtype=F32)[:, None] * inv_freq[None, :]
    cos, sin = jnp.cos(ang), jnp.sin(ang)
    return jnp.concatenate([cos, cos, -sin, sin], axis=1)


def kernel(x, mem, ffn1_norm, ffn1_w_in, ffn1_w_out, mix_norm, mix_w_in, mix_b_gate, mla_q_norm, mla_kv_norm,
           mla_w_uq, mla_w_ukv, mla_w_proj, gla_w_a2, gla_b_a, gla_norm, gla_w_proj, conv_w, conv_w_proj,
           mix_w_out, xattn_norm, mem_norm, xattn_w_q, xattn_w_kv, xattn_w_o, ffn2_norm, ffn2_w_in,
           ffn2_w_out, final_norm):
    batch, seq, d = x.shape
    depth = ffn1_norm.shape[0]
    tokens = batch * seq
    table = _rope_table(seq)
    h = x.reshape(tokens, d)
    mem2 = mem.reshape(batch * mem.shape[1], d)
    bf = lambda w: w.astype(BF16)
    vec = lambda p: p.reshape(depth, 1, -1)

    w_in, w_gate = _pack_in_proj(mix_w_in)
    wq_t = _pack_uq(mla_w_uq)
    wk, wv_t = _pack_ukv(mla_w_ukv)
    w_a2 = bf(jnp.concatenate(
        [gla_w_a2, jnp.zeros((depth, LANES - GLA_RANK, GLA_HEADS * GLA_DK), F32)], axis=1))
    w_projs = (bf(mla_w_proj), bf(gla_w_proj), bf(conv_w_proj))
    w_mix_out, w_xq, w_xo = bf(mix_w_out), bf(xattn_w_q), bf(xattn_w_o)
    conv_taps = conv_w.reshape(depth, 3, CONV_DIM)

    for l in range(depth):
        h = ffn(h, vec(ffn1_norm), ffn1_w_in, ffn1_w_out, final_norm, l, tm=1024, tf=256, final_norm=False)

        z = norm_matmul(h, vec(mix_norm), w_in, l, tm=512, tn=Z_WIDTH // 2, out_dtype=F32)
        q_t, k, v_t = mla_prep(z, table, vec(mla_q_norm), vec(mla_kv_norm), wq_t, wk, wv_t, l,
                               batch=batch, seq=seq, tm=512)
        a_mla = mla_attn(q_t, k, v_t, tq=512, heads=4).reshape(tokens, MLA_HEADS * MLA_V)
        a_gla = gla(z, w_a2, vec(gla_b_a), vec(gla_norm), l, batch=batch, seq=seq, tc=512)
        a_conv = short_conv(z, conv_taps, l, seq=seq, tm=512)
        h = merge(h, vec(mix_norm), (a_mla, a_gla, a_conv), w_gate, vec(mix_b_gate), w_projs, w_mix_out, l,
                  tm=512, tn=512)

        kv = norm_matmul(mem2, vec(mem_norm), xattn_w_kv, l, tm=mem2.shape[0], tn=1024, out_dtype=BF16)
        h = xattn(h, vec(xattn_norm), w_xq, kv, w_xo, l, seq=seq, tm=512)

        h = ffn(h, vec(ffn2_norm), ffn2_w_in, ffn2_w_out, final_norm, l, tm=1024, tf=256,
                final_norm=(l == depth - 1))
    return h.reshape(batch, seq, d)
```

```python
import functools

import jax
import jax.numpy as jnp
from jax import lax
from jax.experimental import pallas as pl
from jax.experimental.pallas import tpu as pltpu

F32 = jnp.float32
BF16 = jnp.bfloat16

EPS = 1e-6
D_MODEL = 2048
D_FF = 5632
MLA_HEADS = 8
MLA_RANK = 512
MLA_NOPE = 128
MLA_ROPE = 64
MLA_V = 128
MLA_QK_PAD = 256
ROPE_THETA = 10000.0
LOG2_E = 1.4426950408889634
GLA_HEADS = 4
GLA_DK = 64
GLA_DV = 128
GLA_RANK = 16
GLA_TAU = 16.0
GLA_CHUNK = 64
GLA_SUBCHUNK = 16
CONV_DIM = 512
X_HEADS = 4
X_HEAD_DIM = D_MODEL // X_HEADS
MEM_LEN = 256

LANES = 128
VMEM_LIMIT_BYTES = 56 * 1024 * 1024

Z_CONV = 0
Z_CQ = 1536
Z_CKV = 2048
Z_GV = 2560
Z_GR = 3072
Z_GQ = 3584
Z_GK = 3840
Z_KR = 4096
Z_AL = 4224
Z_WIDTH = 4352


def _params(*sem):
    return pltpu.CompilerParams(dimension_semantics=sem, vmem_limit_bytes=VMEM_LIMIT_BYTES)


def _rms(x, g):
    return x * lax.rsqrt(jnp.mean(x * x, axis=-1, keepdims=True) + EPS) * g


def _dot(a, b):
    return jnp.dot(a, b, preferred_element_type=F32)


def _dot_t(a, b):
    return lax.dot_general(a, b, (((1,), (1,)), ((), ())), preferred_element_type=F32)


def _norm_matmul_kernel(x_ref, g_ref, w_ref, o_ref):
    xn = _rms(x_ref[...], g_ref[...]).astype(BF16)
    o_ref[...] = _dot(xn, w_ref[...].astype(BF16)).astype(o_ref.dtype)


def _layer(layer, block, index):
    return pl.BlockSpec((None,) + tuple(block), lambda *ids: (layer,) + tuple(index(*ids)))


def _vec(layer, width):
    return _layer(layer, (1, width), lambda *ids: (0, 0))


def norm_matmul(x, g, w, layer, *, tm, tn, out_dtype):
    m, k = x.shape
    n = w.shape[2]
    assert m % tm == 0 and n % tn == 0
    return pl.pallas_call(
        _norm_matmul_kernel,
        grid=(n // tn, m // tm),
        in_specs=[
            pl.BlockSpec((tm, k), lambda j, i: (i, 0)),
            _vec(layer, k),
            _layer(layer, (k, tn), lambda j, i: (0, j)),
        ],
        out_specs=pl.BlockSpec((tm, tn), lambda j, i: (i, j)),
        out_shape=jax.ShapeDtypeStruct((m, n), out_dtype),
        compiler_params=_params("arbitrary", "arbitrary"),
        name="norm_matmul",
    )(x, g, w)


def _ffn_kernel(x_ref, g_ref, wg_ref, wu_ref, wo_ref, gf_ref, o_ref, xn_ref, *, final_norm):
    j = pl.program_id(1)

    @pl.when(j == 0)
    def _():
        x = x_ref[...]
        xn_ref[...] = _rms(x, g_ref[...]).astype(BF16)
        o_ref[...] = x

    xn = xn_ref[...]
    gate = _dot(xn, wg_ref[...].astype(BF16))
    up = _dot(xn, wu_ref[...].astype(BF16))
    act = (0.5 * gate) * jax.nn.sigmoid(gate) * up
    o_ref[...] += _dot(act.astype(BF16), wo_ref[...].astype(BF16))

    if final_norm:
        @pl.when(j == pl.num_programs(1) - 1)
        def _():
            o_ref[...] = _rms(o_ref[...], gf_ref[...])


def ffn(x, g, w_in, w_out, g_final, layer, *, tm, tf, final_norm):
    m, d = x.shape
    f = w_out.shape[1]
    assert m % tm == 0 and f % tf == 0
    nf = f // tf
    return pl.pallas_call(
        functools.partial(_ffn_kernel, final_norm=final_norm),
        grid=(m // tm, nf),
        in_specs=[
            pl.BlockSpec((tm, d), lambda i, j: (i, 0)),
            _vec(layer, d),
            _layer(layer, (d, tf), lambda i, j: (0, j)),
            _layer(layer, (d, tf), lambda i, j: (0, nf + j)),
            _layer(layer, (tf, d), lambda i, j: (j, 0)),
            pl.BlockSpec((1, d), lambda i, j: (0, 0)),
        ],
        out_specs=pl.BlockSpec((tm, d), lambda i, j: (i, 0)),
        out_shape=jax.ShapeDtypeStruct((m, d), F32),
        scratch_shapes=[pltpu.VMEM((tm, d), BF16)],
        compiler_params=_params("arbitrary", "arbitrary"),
        name="ffn",
    )(x, g, w_in, w_in, w_out, g_final.reshape(1, d))


def _rope_pairs(x, table):
    y = x * table
    return y + pltpu.roll(y, MLA_ROPE, 1)


def _mla_prep_kernel(cq_ref, ckv_ref, kr_ref, tab_ref, tab_t_ref, gq_ref, gkv_ref, wq_t_ref, wk_ref, wv_t_ref,
                     q_t_ref, k_ref, v_t_ref):
    scale = (MLA_NOPE + MLA_ROPE) ** -0.5 * LOG2_E
    nq_t = jnp.transpose(_rms(cq_ref[...], gq_ref[...])).astype(BF16)
    q_t = _dot(wq_t_ref[...], nq_t) * scale
    nkv = _rms(ckv_ref[...], gkv_ref[...])
    kn = _dot(nkv.astype(BF16), wk_ref[...])
    v_t = _dot(wv_t_ref[...], jnp.transpose(nkv).astype(BF16))
    tab = tab_ref[...]
    tab_t = tab_t_ref[...]
    lane = lax.broadcasted_iota(jnp.int32, tab.shape, 1)
    k_pe = jnp.where(lane < MLA_ROPE, _rope_pairs(kr_ref[...], tab), 0.0).astype(BF16)
    zeros = jnp.zeros((MLA_QK_PAD - MLA_NOPE - MLA_ROPE, q_t.shape[1]), BF16)
    for h in range(MLA_HEADS):
        q0 = h * MLA_QK_PAD
        q_t_ref[0, h, 0:MLA_NOPE, :] = q_t[q0:q0 + MLA_NOPE].astype(BF16)
        pe = q_t[q0 + MLA_NOPE:q0 + MLA_QK_PAD] * tab_t
        q_t_ref[0, h, MLA_NOPE:MLA_NOPE + MLA_ROPE, :] = (pe[0:MLA_ROPE] + pe[MLA_ROPE:2 * MLA_ROPE]).astype(BF16)
        q_t_ref[0, h, MLA_NOPE + MLA_ROPE:MLA_QK_PAD, :] = zeros
        k_ref[0, h, :, 0:LANES] = kn[:, h * LANES:(h + 1) * LANES].astype(BF16)
        k_ref[0, h, :, LANES:2 * LANES] = k_pe
        v_t_ref[0, h] = v_t[h * MLA_V:(h + 1) * MLA_V].astype(BF16)


def mla_prep(z, table, g_q, g_kv, wq_t, wk, wv_t, layer, *, batch, seq, tm):
    assert seq % tm == 0
    nb = seq // tm
    r = MLA_RANK
    hw = MLA_HEADS
    tok = lambda width, col: pl.BlockSpec((tm, width), lambda i: (i, col // width))
    full = lambda a: _layer(layer, a.shape[1:], lambda i: (0, 0))
    feat_major = lambda width: pl.BlockSpec((1, hw, width, tm), lambda i: (i // nb, 0, 0, i % nb))
    return pl.pallas_call(
        _mla_prep_kernel,
        grid=(batch * nb,),
        in_specs=[
            tok(r, Z_CQ), tok(r, Z_CKV), tok(LANES, Z_KR),
            pl.BlockSpec((tm, LANES), lambda i: (i % nb, 0)),
            pl.BlockSpec((LANES, tm), lambda i: (0, i % nb)),
            full(g_q), full(g_kv), full(wq_t), full(wk), full(wv_t),
        ],
        out_specs=[
            feat_major(MLA_QK_PAD),
            pl.BlockSpec((1, hw, tm, MLA_QK_PAD), lambda i: (i // nb, 0, i % nb, 0)),
            feat_major(MLA_V),
        ],
        out_shape=[
            jax.ShapeDtypeStruct((batch, hw, MLA_QK_PAD, seq), BF16),
            jax.ShapeDtypeStruct((batch, hw, seq, MLA_QK_PAD), BF16),
            jax.ShapeDtypeStruct((batch, hw, MLA_V, seq), BF16),
        ],
        compiler_params=_params("arbitrary"),
        name="mla_prep",
    )(z, z, z, table, jnp.transpose(table), g_q, g_kv, wq_t, wk, wv_t)


def _mla_attn_kernel(q_t_ref, k_ref, v_t_ref, o_ref, m_ref, l_ref, acc_ref, *, tq, heads):
    qi = pl.program_id(2)
    m_ref[...] = jnp.full(m_ref.shape, -jnp.inf, F32)
    l_ref[...] = jnp.zeros(l_ref.shape, F32)
    acc_ref[...] = jnp.zeros(acc_ref.shape, F32)

    def step(j, masked):
        start = pl.multiple_of(j * tq, tq)
        scores = [_dot(k_ref[0, g, pl.ds(start, tq), :], q_t_ref[0, g]) for g in range(heads)]
        for g in range(heads):
            s = scores[g]
            if masked:
                key = lax.broadcasted_iota(jnp.int32, s.shape, 0)
                qry = lax.broadcasted_iota(jnp.int32, s.shape, 1)
                s = jnp.where(key <= qry, s, -jnp.inf)
            m_old = m_ref[g]
            m_new = jnp.maximum(m_old, jnp.max(s, axis=0, keepdims=True))
            p = jnp.exp2(s - m_new)
            alpha = jnp.exp2(m_old - m_new)
            l_ref[g] = alpha * l_ref[g] + jnp.sum(p, axis=0, keepdims=True)
            acc_ref[g] = alpha * acc_ref[g] + _dot(v_t_ref[0, g, :, pl.ds(start, tq)], p.astype(BF16))
            m_ref[g] = m_new

    def body(j, carry):
        step(j, False)
        return carry

    lax.fori_loop(0, qi, body, 0)
    step(qi, True)
    for g in range(heads):
        o_t = acc_ref[g] / l_ref[g]
        o_ref[0, :, g * MLA_V:(g + 1) * MLA_V] = jnp.transpose(o_t).astype(o_ref.dtype)


def mla_attn(q_t, k, v_t, *, tq, heads):
    b, h, s, dq = k.shape
    dv = v_t.shape[2]
    assert s % tq == 0 and h % heads == 0
    return pl.pallas_call(
        functools.partial(_mla_attn_kernel, tq=tq, heads=heads),
        grid=(b, h // heads, s // tq),
        in_specs=[
            pl.BlockSpec((1, heads, dq, tq), lambda bi, hi, qi: (bi, hi, 0, qi)),
            pl.BlockSpec((1, heads, s, dq), lambda bi, hi, qi: (bi, hi, 0, 0)),
            pl.BlockSpec((1, heads, dv, s), lambda bi, hi, qi: (bi, hi, 0, 0)),
        ],
        out_specs=pl.BlockSpec((1, tq, heads * dv), lambda bi, hi, qi: (bi, qi, hi)),
        out_shape=jax.ShapeDtypeStruct((b, s, h * dv), BF16),
        scratch_shapes=[pltpu.VMEM((heads, 1, tq), F32), pltpu.VMEM((heads, 1, tq), F32),
                        pltpu.VMEM((heads, dv, tq), F32)],
        compiler_params=_params("arbitrary", "arbitrary", "arbitrary"),
        name="mla_attn",
    )(q_t, k, v_t)


def _gla_kernel(q_ref, k_ref, v_ref, r_ref, al_ref, wa_ref, ba_ref, ng_ref, o_ref,
                state_ref, b_scr, q_scr, k_scr, v_scr, oi_scr, *, tc):
    c = GLA_CHUNK
    pair_w = 2 * GLA_DK
    pair_v = 2 * GLA_DV
    n_pairs = GLA_HEADS // 2

    @pl.when(pl.program_id(1) == 0)
    def _():
        state_ref[...] = jnp.zeros(state_ref.shape, F32)

    row_i = lax.broadcasted_iota(jnp.int32, (c, c), 0)
    col_i = lax.broadcasted_iota(jnp.int32, (c, c), 1)
    tri = jnp.where(row_i >= col_i, 1.0, 0.0).astype(BF16)
    lane_k = lax.broadcasted_iota(jnp.int32, (pair_w, LANES), 0)
    head_sum = [jnp.where((lane_k // GLA_DK) == hh, 1.0, 0.0).astype(BF16) for hh in range(2)]
    sub = GLA_SUBCHUNK
    key_row = lax.broadcasted_iota(jnp.int32, (c, pair_w), 0)
    lane_h = lax.broadcasted_iota(jnp.int32, (1, pair_w), 1) // GLA_DK
    head_lanes = [jnp.where(lane_h == hh, 1.0, 0.0) for hh in range(2)]
    srow = lax.broadcasted_iota(jnp.int32, (pair_w, pair_v), 0) // GLA_DK
    scol = lax.broadcasted_iota(jnp.int32, (pair_w, pair_v), 1) // GLA_DV
    own_head = srow == scol

    def chunk(ci, carry):
        r0 = pl.multiple_of(ci * c, c)
        rows = pl.ds(r0, c)
        x = _dot(al_ref[rows, :].astype(BF16), wa_ref[...]) + ba_ref[...]
        log_a = (jnp.minimum(x, 0.0) - jnp.log(1.0 + jnp.exp(-jnp.abs(x)))) * (1.0 / GLA_TAU)
        p1 = log_a.astype(BF16)
        r1 = log_a - p1.astype(F32)
        p2 = r1.astype(BF16)
        p3 = (r1 - p2.astype(F32)).astype(BF16)
        b_scr[...] = _dot(tri, p1) + _dot(tri, p2) + _dot(tri, p3)
        q_scr[...] = q_ref[rows, :] * (GLA_DK ** -0.5)
        k_scr[...] = k_ref[rows, :]
        v_scr[...] = v_ref[rows, :]

        staged = []
        for p in range(n_pairs):
            kl = slice(p * pair_w, (p + 1) * pair_w)
            vl = slice(p * pair_v, (p + 1) * pair_v)
            bp = b_scr[:, kl]
            qp = q_scr[:, kl]
            kp = k_scr[:, kl]
            vp = v_scr[:, vl]
            b_last = b_scr[c - 1:c, kl]
            st = state_ref[p]
            q_dec = (qp * jnp.exp(bp)).astype(BF16)
            k_dec = (kp * jnp.exp(b_last - bp)).astype(BF16)
            o_inter = _dot(q_dec, st.astype(BF16))
            vp_bf = vp.astype(BF16)
            kv = lax.dot_general(k_dec, vp_bf, (((0,), (0,)), ((), ())), preferred_element_type=F32)
            decay_col = jnp.transpose(jnp.broadcast_to(jnp.exp(b_last), (LANES, pair_w)))
            decay_col = jnp.concatenate([decay_col, decay_col], axis=1)
            state_ref[p] = decay_col * st + jnp.where(own_head, kv, 0.0)
            q_blocks, k_blocks = [], []
            for i in range(1, c // sub):
                r = i * sub
                b_r = b_scr[r:r + 1, kl]
                q_i = q_scr[r:r + sub, kl] * jnp.exp(b_scr[r:r + sub, kl] - b_r)
                pieces = [jnp.zeros((r, pair_w), F32), q_i]
                if c - r - sub:
                    pieces.append(jnp.zeros((c - r - sub, pair_w), F32))
                q_blocks.append(jnp.concatenate(pieces, axis=0))
                k_blocks.append(kp * jnp.exp(jnp.where(key_row < r, b_r - bp, -jnp.inf)))
            q_cat = jnp.concatenate(q_blocks, axis=1)
            k_cat = jnp.concatenate(k_blocks, axis=1).astype(BF16)
            a_far = [_dot_t((q_cat * jnp.concatenate([head_lanes[hh]] * len(q_blocks), axis=1)).astype(BF16), k_cat)
                     for hh in range(2)]
            ys = []
            for i in range(c // sub):
                r = i * sub
                for t0 in (r, r + 8):
                    n = r + sub - t0
                    bt = b_scr[t0:r + sub, kl]
                    qt = q_scr[t0:r + sub, kl]
                    t_idx = t0 + lax.broadcasted_iota(jnp.int32, (n, pair_w), 0)
                    for s in range(t0, t0 + 8):
                        e = jnp.exp(jnp.where(t_idx >= s, bt - b_scr[s:s + 1, kl], -jnp.inf))
                        ys.append(qt * e * k_scr[s:s + 1, kl])
            y = jnp.concatenate(ys, axis=0).astype(BF16)
            a_near = [_dot(y, head_sum[hh]) for hh in range(2)]
            staged.append((o_inter, vp_bf, a_far, a_near))

        for p in range(n_pairs):
            o_inter, vp_bf, a_far, a_near = staged[p]
            for hh in range(2):
                hv = slice(p * pair_v + hh * GLA_DV, p * pair_v + (hh + 1) * GLA_DV)
                o_h = o_inter[:, hh * GLA_DV:(hh + 1) * GLA_DV] + _dot(a_far[hh].astype(BF16),
                                                                       vp_bf[:, hh * GLA_DV:(hh + 1) * GLA_DV])
                a = a_near[hh]
                for i in range(c // sub):
                    r = i * sub
                    base = i * (8 * sub + 8 * 8)
                    upd = a[base:base + sub] * v_scr[r:r + 1, hv]
                    for idx in range(1, 8):
                        upd += a[base + idx * sub:base + (idx + 1) * sub] * v_scr[r + idx:r + idx + 1, hv]
                    base += 8 * sub
                    upd8 = a[base:base + 8] * v_scr[r + 8:r + 9, hv]
                    for idx in range(1, 8):
                        upd8 += a[base + idx * 8:base + (idx + 1) * 8] * v_scr[r + 8 + idx:r + 9 + idx, hv]
                    upd = jnp.concatenate([upd[0:8], upd[8:sub] + upd8], axis=0)
                    oi_scr[r:r + sub, hv] = upd + o_h[r:r + sub]

        o = oi_scr[...]
        gate = r_ref[rows, :]
        gate = gate * jax.nn.sigmoid(gate)
        for h in range(GLA_HEADS):
            hv = slice(h * GLA_DV, (h + 1) * GLA_DV)
            oh = o[:, hv]
            oh = oh * lax.rsqrt(jnp.mean(oh * oh, axis=-1, keepdims=True) + EPS)
            o_ref[rows, hv] = (oh * ng_ref[:, hv] * gate[:, hv]).astype(o_ref.dtype)
        return carry

    lax.fori_loop(0, tc // c, chunk, 0)


def gla(z, w_a2, b_a, norm_g, layer, *, batch, seq, tc):
    assert seq % tc == 0 and tc % GLA_CHUNK == 0
    nb = seq // tc
    kw = GLA_HEADS * GLA_DK
    vw = GLA_HEADS * GLA_DV
    tok = lambda width, col: pl.BlockSpec((tc, width), lambda b, i: (b * nb + i, col // width))
    full = lambda a: _layer(layer, a.shape[1:], lambda b, i: (0, 0))
    c = GLA_CHUNK
    return pl.pallas_call(
        functools.partial(_gla_kernel, tc=tc),
        grid=(batch, nb),
        in_specs=[
            tok(kw, Z_GQ), tok(kw, Z_GK), tok(vw, Z_GV), tok(vw, Z_GR), tok(LANES, Z_AL),
            full(w_a2), full(b_a), full(norm_g),
        ],
        out_specs=pl.BlockSpec((tc, vw), lambda b, i: (b * nb + i, 0)),
        out_shape=jax.ShapeDtypeStruct((batch * seq, vw), BF16),
        scratch_shapes=[
            pltpu.VMEM((GLA_HEADS // 2, 2 * GLA_DK, 2 * GLA_DV), F32),
            pltpu.VMEM((c, kw), F32), pltpu.VMEM((c, kw), F32), pltpu.VMEM((c, kw), F32),
            pltpu.VMEM((c, vw), F32), pltpu.VMEM((c, vw), F32),
        ],
        compiler_params=_params("arbitrary", "arbitrary"),
        name="gla",
    )(z, z, z, z, z, w_a2, b_a, norm_g)


def _conv_kernel(x_ref, halo_ref, w_ref, o_ref, *, blocks_per_seq):
    d = CONV_DIM
    x = x_ref[...]
    z = x[:, d:2 * d] * x[:, 2 * d:3 * d]
    hz = halo_ref[:, d:2 * d] * halo_ref[:, 2 * d:3 * d]
    hz = jnp.where(pl.program_id(0) % blocks_per_seq == 0, 0.0, hz)
    row = lax.broadcasted_iota(jnp.int32, z.shape, 0)
    z1 = jnp.where(row == 0, hz[7:8], pltpu.roll(z, 1, 0))
    z2 = jnp.where(row == 0, hz[6:7], jnp.where(row == 1, hz[7:8], pltpu.roll(z, 2, 0)))
    y = w_ref[2:3, :] * z + w_ref[1:2, :] * z1 + w_ref[0:1, :] * z2
    o_ref[...] = (x[:, 0:d] * y).astype(o_ref.dtype)


def short_conv(z, w, layer, *, seq, tm):
    m = z.shape[0]
    assert seq % tm == 0 and tm % 8 == 0
    width = 3 * CONV_DIM
    return pl.pallas_call(
        functools.partial(_conv_kernel, blocks_per_seq=seq // tm),
        grid=(m // tm,),
        in_specs=[
            pl.BlockSpec((tm, width), lambda i: (i, Z_CONV // width)),
            pl.BlockSpec((8, width), lambda i: (jnp.maximum(i * (tm // 8) - 1, 0), Z_CONV // width)),
            _layer(layer, w.shape[1:], lambda i: (0, 0)),
        ],
        out_specs=pl.BlockSpec((tm, CONV_DIM), lambda i: (i, 0)),
        out_shape=jax.ShapeDtypeStruct((m, CONV_DIM), BF16),
        compiler_params=_params("arbitrary"),
        name="short_conv",
    )(z, z, w)


def _merge_kernel(h_ref, g_ref, a0_ref, a1_ref, a2_ref, wg0_ref, wg1_ref, wg2_ref, b0_ref, b1_ref, b2_ref,
                  wp0_ref, wp1_ref, wp2_ref, wo_ref, o_ref, u_ref):
    j = pl.program_id(1)

    @pl.when(j == 0)
    def _():
        h = h_ref[...]
        u_ref[...] = _rms(h, g_ref[...]).astype(BF16)
        o_ref[...] = h

    u = u_ref[...]
    merged = None
    for a_ref, wg_ref, b_ref, wp_ref in ((a0_ref, wg0_ref, b0_ref, wp0_ref),
                                         (a1_ref, wg1_ref, b1_ref, wp1_ref),
                                         (a2_ref, wg2_ref, b2_ref, wp2_ref)):
        gate = jax.nn.sigmoid(_dot(u, wg_ref[...]) + b_ref[...])
        term = gate * _dot(a_ref[...], wp_ref[...])
        merged = term if merged is None else merged + term
    o_ref[...] += _dot(merged.astype(BF16), wo_ref[...])


def merge(h, g, branches, w_gate, b_gate, w_projs, w_out, layer, *, tm, tn):
    m, d = h.shape
    assert m % tm == 0 and d % tn == 0
    nj = d // tn
    row = lambda a: pl.BlockSpec((tm, a.shape[1]), lambda i, j: (i, 0))
    gate_w = lambda b: _layer(layer, (d, tn), lambda i, j: (0, b * nj + j))
    gate_b = lambda b: _layer(layer, (1, tn), lambda i, j: (0, b * nj + j))
    proj_w = lambda w: _layer(layer, (w.shape[1], tn), lambda i, j: (0, j))
    return pl.pallas_call(
        _merge_kernel,
        grid=(m // tm, nj),
        in_specs=[
            row(h), _vec(layer, d),
            row(branches[0]), row(branches[1]), row(branches[2]),
            gate_w(0), gate_w(1), gate_w(2), gate_b(0), gate_b(1), gate_b(2),
            proj_w(w_projs[0]), proj_w(w_projs[1]), proj_w(w_projs[2]),
            _layer(layer, (tn, d), lambda i, j: (j, 0)),
        ],
        out_specs=pl.BlockSpec((tm, d), lambda i, j: (i, 0)),
        out_shape=jax.ShapeDtypeStruct((m, d), F32),
        scratch_shapes=[pltpu.VMEM((tm, d), BF16)],
        compiler_params=_params("arbitrary", "arbitrary"),
        name="merge",
    )(h, g, *branches, w_gate, w_gate, w_gate, b_gate, b_gate, b_gate, *w_projs, w_out)


def _xattn_kernel(h_ref, g_ref, wq_ref, k_ref, v_ref, wo_ref, o_ref):
    h = h_ref[...]
    hn = _rms(h, g_ref[...]).astype(BF16)
    q = (_dot(hn, wq_ref[...]) * (X_HEAD_DIM ** -0.5)).astype(BF16)
    heads = [slice(j * X_HEAD_DIM, (j + 1) * X_HEAD_DIM) for j in range(X_HEADS)]
    scores = [_dot_t(q[:, hs], k_ref[:, hs]) for hs in heads]
    outs = []
    for hs, s in zip(heads, scores):
        p = jnp.exp(s - jnp.max(s, axis=-1, keepdims=True))
        o = _dot(p.astype(BF16), v_ref[:, hs]) / jnp.sum(p, axis=-1, keepdims=True)
        outs.append(o.astype(BF16))
    o_ref[...] = h + _dot(jnp.concatenate(outs, axis=1), wo_ref[...])


def xattn(h, g, w_q, kv, w_o, layer, *, seq, tm):
    m, d = h.shape
    assert seq % tm == 0
    nb = seq // tm
    once = pl.Buffered(1)
    return pl.pallas_call(
        _xattn_kernel,
        grid=(m // tm,),
        in_specs=[
            pl.BlockSpec((tm, d), lambda i: (i, 0)),
            _vec(layer, d),
            pl.BlockSpec((None, d, d), lambda i: (layer, 0, 0), pipeline_mode=once),
            pl.BlockSpec((MEM_LEN, d), lambda i: (i // nb, 0)),
            pl.BlockSpec((MEM_LEN, d), lambda i: (i // nb, 1)),
            pl.BlockSpec((None, d, d), lambda i: (layer, 0, 0), pipeline_mode=once),
        ],
        out_specs=pl.BlockSpec((tm, d), lambda i: (i, 0)),
        out_shape=jax.ShapeDtypeStruct((m, d), F32),
        compiler_params=_params("arbitrary"),
        name="xattn",
    )(h, g, w_q, kv, kv, w_o)


def _pack_in_proj_kernel(w_ref, packed_ref, gate_ref):
    def put(dst, src, width):
        packed_ref[:, dst:dst + width] = w_ref[:, src:src + width].astype(BF16)

    put(Z_CQ, 0, 512)
    put(Z_CKV, 512, 512)
    put(Z_GQ, 1088, 256)
    put(Z_GK, 1344, 256)
    put(Z_GV, 1600, 512)
    put(Z_GR, 2112, 512)
    put(Z_CONV, 2640, 1536)
    half = MLA_ROPE // 2
    k_rope = w_ref[:, 1024:1024 + MLA_ROPE]
    packed_ref[:, Z_KR:Z_KR + LANES] = jnp.concatenate(
        [k_rope, k_rope[:, half:], k_rope[:, :half]], axis=1).astype(BF16)
    a_low = w_ref[:, 2624:2624 + GLA_RANK]
    packed_ref[:, Z_AL:Z_AL + LANES] = jnp.concatenate(
        [a_low, jnp.zeros((a_low.shape[0], LANES - GLA_RANK), F32)], axis=1).astype(BF16)
    gate_ref[...] = w_ref[:, 4176:].astype(BF16)


def _pack_in_proj(w, *, tr):
    depth, d, n = w.shape
    n_gate = n - 4176
    assert d % tr == 0
    return pl.pallas_call(
        _pack_in_proj_kernel,
        grid=(depth, d // tr),
        in_specs=[pl.BlockSpec((None, tr, n), lambda l, i: (l, i, 0))],
        out_specs=[pl.BlockSpec((None, tr, Z_WIDTH), lambda l, i: (l, i, 0)),
                   pl.BlockSpec((None, tr, n_gate), lambda l, i: (l, i, 0))],
        out_shape=[jax.ShapeDtypeStruct((depth, d, Z_WIDTH), BF16),
                   jax.ShapeDtypeStruct((depth, d, n_gate), BF16)],
        compiler_params=_params("arbitrary", "arbitrary"),
        name="pack_in_proj",
    )(w)


def _pack_uq(w):
    depth = w.shape[0]
    w = w.reshape(depth, MLA_RANK, MLA_HEADS, MLA_NOPE + MLA_ROPE)
    half = MLA_ROPE // 2
    nope, pe = w[..., :MLA_NOPE], w[..., MLA_NOPE:]
    swap = jnp.concatenate([pe[..., half:], pe[..., :half]], axis=-1)
    packed = jnp.concatenate([nope, pe, swap], axis=-1).reshape(depth, MLA_RANK, MLA_HEADS * MLA_QK_PAD)
    return jnp.swapaxes(packed, 1, 2).astype(BF16)


def _pack_ukv(w):
    depth = w.shape[0]
    w = w.reshape(depth, MLA_RANK, MLA_HEADS, MLA_NOPE + MLA_V)
    wk = w[..., :MLA_NOPE].reshape(depth, MLA_RANK, MLA_HEADS * MLA_NOPE)
    wv = w[..., MLA_NOPE:].reshape(depth, MLA_RANK, MLA_HEADS * MLA_V)
    return wk.astype(BF16), jnp.swapaxes(wv, 1, 2).astype(BF16)


def _rope_table(seq):
    inv_freq = 1.0 / (ROPE_THETA ** (jnp.arange(0, MLA_ROPE, 2, dtype=F32) / MLA_ROPE))
    ang = jnp.arange(seq, dtype=F32)[:, None] * inv_freq[None, :]
    cos, sin = jnp.cos(ang), jnp.sin(ang)
    return jnp.concatenate([cos, cos, -sin, sin], axis=1)


def kernel(x, mem, ffn1_norm, ffn1_w_in, ffn1_w_out, mix_norm, mix_w_in, mix_b_gate, mla_q_norm, mla_kv_norm,
           mla_w_uq, mla_w_ukv, mla_w_proj, gla_w_a2, gla_b_a, gla_norm, gla_w_proj, conv_w, conv_w_proj,
           mix_w_out, xattn_norm, mem_norm, xattn_w_q, xattn_w_kv, xattn_w_o, ffn2_norm, ffn2_w_in,
           ffn2_w_out, final_norm):
    batch, seq, d = x.shape
    depth = ffn1_norm.shape[0]
    tokens = batch * seq
    table = _rope_table(seq)
    h = x.reshape(tokens, d)
    mem2 = mem.reshape(batch * mem.shape[1], d)
    bf = lambda w: w.astype(BF16)
    vec = lambda p: p.reshape(depth, 1, -1)

    w_in, w_gate = _pack_in_proj(mix_w_in, tr=256)
    wq_t = _pack_uq(mla_w_uq)
    wk, wv_t = _pack_ukv(mla_w_ukv)
    w_a2 = bf(jnp.concatenate(
        [gla_w_a2, jnp.zeros((depth, LANES - GLA_RANK, GLA_HEADS * GLA_DK), F32)], axis=1))
    w_projs = (bf(mla_w_proj), bf(gla_w_proj), bf(conv_w_proj))
    w_mix_out, w_xq, w_xo = bf(mix_w_out), bf(xattn_w_q), bf(xattn_w_o)
    conv_taps = conv_w.reshape(depth, 3, CONV_DIM)

    for l in range(depth):
        h = ffn(h, vec(ffn1_norm), ffn1_w_in, ffn1_w_out, final_norm, l, tm=1024, tf=256, final_norm=False)

        z = norm_matmul(h, vec(mix_norm), w_in, l, tm=512, tn=Z_WIDTH // 2, out_dtype=F32)
        q_t, k, v_t = mla_prep(z, table, vec(mla_q_norm), vec(mla_kv_norm), wq_t, wk, wv_t, l,
                               batch=batch, seq=seq, tm=512)
        a_mla = mla_attn(q_t, k, v_t, tq=512, heads=4).reshape(tokens, MLA_HEADS * MLA_V)
        a_gla = gla(z, w_a2, vec(gla_b_a), vec(gla_norm), l, batch=batch, seq=seq, tc=512)
        a_conv = short_conv(z, conv_taps, l, seq=seq, tm=512)
        h = merge(h, vec(mix_norm), (a_mla, a_gla, a_conv), w_gate, vec(mix_b_gate), w_projs, w_mix_out, l,
                  tm=512, tn=512)

        kv = norm_matmul(mem2, vec(mem_norm), xattn_w_kv, l, tm=mem2.shape[0], tn=1024, out_dtype=BF16)
        h = xattn(h, vec(xattn_norm), w_xq, kv, w_xo, l, seq=seq, tm=512)

        h = ffn(h, vec(ffn2_norm), ffn2_w_in, ffn2_w_out, final_norm, l, tm=1024, tf=256,
                final_norm=(l == depth - 1))
    return h.reshape(batch, seq, d)
```

```python
import functools

import jax
import jax.numpy as jnp
from jax import lax
from jax.experimental import pallas as pl
from jax.experimental.pallas import tpu as pltpu

F32 = jnp.float32
BF16 = jnp.bfloat16

EPS = 1e-6
D_MODEL = 2048
D_FF = 5632
MLA_HEADS = 8
MLA_RANK = 512
MLA_NOPE = 128
MLA_ROPE = 64
MLA_V = 128
MLA_QK_PAD = 256
ROPE_THETA = 10000.0
LOG2_E = 1.4426950408889634
GLA_HEADS = 4
GLA_DK = 64
GLA_DV = 128
GLA_RANK = 16
GLA_TAU = 16.0
GLA_CHUNK = 64
GLA_SUBCHUNK = 16
CONV_DIM = 512
X_HEADS = 4
X_HEAD_DIM = D_MODEL // X_HEADS
MEM_LEN = 256

LANES = 128
SUBLANES = 8
VMEM_LIMIT_BYTES = 56 * 1024 * 1024

Z_CONV = 0
Z_CQ = 1536
Z_CKV = 2048
Z_GV = 2560
Z_GR = 3072
Z_GQ = 3584
Z_GK = 3840
Z_KR = 4096
Z_AL = 4224
Z_WIDTH = 4352


def _params(*sem):
    return pltpu.CompilerParams(dimension_semantics=sem, vmem_limit_bytes=VMEM_LIMIT_BYTES)


def _rms(x, g):
    return x * lax.rsqrt(jnp.mean(x * x, axis=-1, keepdims=True) + EPS) * g


def _dot(a, b):
    return jnp.dot(a, b, preferred_element_type=F32)


def _dot_t(a, b):
    return lax.dot_general(a, b, (((1,), (1,)), ((), ())), preferred_element_type=F32)


def _norm_matmul_kernel(x_ref, g_ref, w_ref, o_ref):
    xn = _rms(x_ref[...], g_ref[...]).astype(BF16)
    o_ref[...] = _dot(xn, w_ref[...].astype(BF16)).astype(o_ref.dtype)


def _layer(layer, block, index):
    return pl.BlockSpec((None,) + tuple(block), lambda *ids: (layer,) + tuple(index(*ids)))


def _vec(layer, width):
    return _layer(layer, (1, width), lambda *ids: (0, 0))


def norm_matmul(x, g, w, layer, *, n, tm, tn, out_dtype):
    m, k = x.shape
    assert m % tm == 0 and n % tn == 0 and n <= w.shape[2]
    return pl.pallas_call(
        _norm_matmul_kernel,
        grid=(n // tn, m // tm),
        in_specs=[
            pl.BlockSpec((tm, k), lambda j, i: (i, 0)),
            _vec(layer, k),
            _layer(layer, (k, tn), lambda j, i: (0, j)),
        ],
        out_specs=pl.BlockSpec((tm, tn), lambda j, i: (i, j)),
        out_shape=jax.ShapeDtypeStruct((m, n), out_dtype),
        compiler_params=_params("arbitrary", "arbitrary"),
        name="norm_matmul",
    )(x, g, w)


def _ffn_kernel(x_ref, g_ref, wg_ref, wu_ref, wo_ref, gf_ref, o_ref, xn_ref, *, final_norm):
    j = pl.program_id(1)

    @pl.when(j == 0)
    def _():
        x = x_ref[...]
        xn_ref[...] = _rms(x, g_ref[...]).astype(BF16)
        o_ref[...] = x

    xn = xn_ref[...]
    gate = _dot(xn, wg_ref[...].astype(BF16))
    up = _dot(xn, wu_ref[...].astype(BF16))
    act = (0.5 * gate) * jax.nn.sigmoid(gate) * up
    o_ref[...] += _dot(act.astype(BF16), wo_ref[...].astype(BF16))

    if final_norm:
        @pl.when(j == pl.num_programs(1) - 1)
        def _():
            o_ref[...] = _rms(o_ref[...], gf_ref[...])


def ffn(x, g, w_in, w_out, g_final, layer, *, tm, tf, final_norm):
    m, d = x.shape
    f = w_out.shape[1]
    assert m % tm == 0 and f % tf == 0
    nf = f // tf
    return pl.pallas_call(
        functools.partial(_ffn_kernel, final_norm=final_norm),
        grid=(m // tm, nf),
        in_specs=[
            pl.BlockSpec((tm, d), lambda i, j: (i, 0)),
            _vec(layer, d),
            _layer(layer, (d, tf), lambda i, j: (0, j)),
            _layer(layer, (d, tf), lambda i, j: (0, nf + j)),
            _layer(layer, (tf, d), lambda i, j: (j, 0)),
            pl.BlockSpec((1, d), lambda i, j: (0, 0)),
        ],
        out_specs=pl.BlockSpec((tm, d), lambda i, j: (i, 0)),
        out_shape=jax.ShapeDtypeStruct((m, d), F32),
        scratch_shapes=[pltpu.VMEM((tm, d), BF16)],
        compiler_params=_params("arbitrary", "arbitrary"),
        name="ffn",
    )(x, g, w_in, w_in, w_out, g_final.reshape(1, d))


def _rope_pairs(x, table):
    y = x * table
    return y + pltpu.roll(y, MLA_ROPE, 1)


def _mla_prep_kernel(cq_ref, ckv_ref, kr_ref, tab_ref, tab_t_ref, gq_ref, gkv_ref, wq_t_ref, wk_ref, wv_t_ref,
                     q_t_ref, k_ref, v_t_ref):
    scale = (MLA_NOPE + MLA_ROPE) ** -0.5 * LOG2_E
    nq_t = jnp.transpose(_rms(cq_ref[...], gq_ref[...])).astype(BF16)
    q_t = _dot(wq_t_ref[...], nq_t) * scale
    nkv = _rms(ckv_ref[...], gkv_ref[...])
    kn = _dot(nkv.astype(BF16), wk_ref[...])
    v_t = _dot(wv_t_ref[...], jnp.transpose(nkv).astype(BF16))
    tab = tab_ref[...]
    tab_t = tab_t_ref[...]
    lane = lax.broadcasted_iota(jnp.int32, tab.shape, 1)
    k_pe = jnp.where(lane < MLA_ROPE, _rope_pairs(kr_ref[...], tab), 0.0).astype(BF16)
    zeros = jnp.zeros((MLA_QK_PAD - MLA_NOPE - MLA_ROPE, q_t.shape[1]), BF16)
    for h in range(MLA_HEADS):
        q0 = h * MLA_QK_PAD
        q_t_ref[0, h, 0:MLA_NOPE, :] = q_t[q0:q0 + MLA_NOPE].astype(BF16)
        pe = q_t[q0 + MLA_NOPE:q0 + MLA_QK_PAD] * tab_t
        q_t_ref[0, h, MLA_NOPE:MLA_NOPE + MLA_ROPE, :] = (pe[0:MLA_ROPE] + pe[MLA_ROPE:2 * MLA_ROPE]).astype(BF16)
        q_t_ref[0, h, MLA_NOPE + MLA_ROPE:MLA_QK_PAD, :] = zeros
        k_ref[0, h, :, 0:LANES] = kn[:, h * LANES:(h + 1) * LANES].astype(BF16)
        k_ref[0, h, :, LANES:2 * LANES] = k_pe
        v_t_ref[0, h] = v_t[h * MLA_V:(h + 1) * MLA_V].astype(BF16)


def mla_prep(z, table, g_q, g_kv, wq_t, wk, wv_t, layer, *, batch, seq, tm):
    assert seq % tm == 0
    nb = seq // tm
    r = MLA_RANK
    hw = MLA_HEADS
    tok = lambda width, col: pl.BlockSpec((tm, width), lambda i: (i, col // width))
    full = lambda a: _layer(layer, a.shape[1:], lambda i: (0, 0))
    feat_major = lambda width: pl.BlockSpec((1, hw, width, tm), lambda i: (i // nb, 0, 0, i % nb))
    return pl.pallas_call(
        _mla_prep_kernel,
        grid=(batch * nb,),
        in_specs=[
            tok(r, Z_CQ), tok(r, Z_CKV), tok(LANES, Z_KR),
            pl.BlockSpec((tm, LANES), lambda i: (i % nb, 0)),
            pl.BlockSpec((LANES, tm), lambda i: (0, i % nb)),
            full(g_q), full(g_kv), full(wq_t), full(wk), full(wv_t),
        ],
        out_specs=[
            feat_major(MLA_QK_PAD),
            pl.BlockSpec((1, hw, tm, MLA_QK_PAD), lambda i: (i // nb, 0, i % nb, 0)),
            feat_major(MLA_V),
        ],
        out_shape=[
            jax.ShapeDtypeStruct((batch, hw, MLA_QK_PAD, seq), BF16),
            jax.ShapeDtypeStruct((batch, hw, seq, MLA_QK_PAD), BF16),
            jax.ShapeDtypeStruct((batch, hw, MLA_V, seq), BF16),
        ],
        compiler_params=_params("arbitrary"),
        name="mla_prep",
    )(z, z, z, table, jnp.transpose(table), g_q, g_kv, wq_t, wk, wv_t)


def _mla_attn_kernel(q_t_ref, k_ref, v_t_ref, o_ref, m_ref, l_ref, acc_ref, *, tq, heads):
    qi = pl.program_id(2)
    m_ref[...] = jnp.full(m_ref.shape, -jnp.inf, F32)
    l_ref[...] = jnp.zeros(l_ref.shape, F32)
    acc_ref[...] = jnp.zeros(acc_ref.shape, F32)

    def step(j, masked):
        start = pl.multiple_of(j * tq, tq)
        scores = [_dot(k_ref[0, g, pl.ds(start, tq), :], q_t_ref[0, g]) for g in range(heads)]
        for g in range(heads):
            s = scores[g]
            if masked:
                key = lax.broadcasted_iota(jnp.int32, s.shape, 0)
                qry = lax.broadcasted_iota(jnp.int32, s.shape, 1)
                s = jnp.where(key <= qry, s, -jnp.inf)
            m_old = m_ref[g]
            m_new = jnp.maximum(m_old, jnp.max(s, axis=0, keepdims=True))
            p = jnp.exp2(s - m_new)
            alpha = jnp.exp2(m_old - m_new)
            l_ref[g] = alpha * l_ref[g] + jnp.sum(p, axis=0, keepdims=True)
            acc_ref[g] = alpha * acc_ref[g] + _dot(v_t_ref[0, g, :, pl.ds(start, tq)], p.astype(BF16))
            m_ref[g] = m_new

    def body(j, carry):
        step(j, False)
        return carry

    lax.fori_loop(0, qi, body, 0)
    step(qi, True)
    for g in range(heads):
        o_t = acc_ref[g] / l_ref[g]
        o_ref[0, :, g * MLA_V:(g + 1) * MLA_V] = jnp.transpose(o_t).astype(o_ref.dtype)


def mla_attn(q_t, k, v_t, *, tq, heads):
    b, h, s, dq = k.shape
    dv = v_t.shape[2]
    assert s % tq == 0 and h % heads == 0
    return pl.pallas_call(
        functools.partial(_mla_attn_kernel, tq=tq, heads=heads),
        grid=(b, h // heads, s // tq),
        in_specs=[
            pl.BlockSpec((1, heads, dq, tq), lambda bi, hi, qi: (bi, hi, 0, qi)),
            pl.BlockSpec((1, heads, s, dq), lambda bi, hi, qi: (bi, hi, 0, 0)),
            pl.BlockSpec((1, heads, dv, s), lambda bi, hi, qi: (bi, hi, 0, 0)),
        ],
        out_specs=pl.BlockSpec((1, tq, heads * dv), lambda bi, hi, qi: (bi, qi, hi)),
        out_shape=jax.ShapeDtypeStruct((b, s, h * dv), BF16),
        scratch_shapes=[pltpu.VMEM((heads, 1, tq), F32), pltpu.VMEM((heads, 1, tq), F32),
                        pltpu.VMEM((heads, dv, tq), F32)],
        compiler_params=_params("arbitrary", "arbitrary", "arbitrary"),
        name="mla_attn",
    )(q_t, k, v_t)


def _gla_kernel(q_ref, k_ref, v_ref, r_ref, al_ref, wa_ref, ba_ref, ng_ref, o_ref,
                state_ref, b_scr, q_scr, k_scr, v_scr, oi_scr, *, tc):
    c = GLA_CHUNK
    pair_w = 2 * GLA_DK
    pair_v = 2 * GLA_DV
    n_pairs = GLA_HEADS // 2

    @pl.when(pl.program_id(1) == 0)
    def _():
        state_ref[...] = jnp.zeros(state_ref.shape, F32)

    row_i = lax.broadcasted_iota(jnp.int32, (c, c), 0)
    col_i = lax.broadcasted_iota(jnp.int32, (c, c), 1)
    tri = jnp.where(row_i >= col_i, 1.0, 0.0).astype(BF16)
    lane_k = lax.broadcasted_iota(jnp.int32, (pair_w, LANES), 0)
    head_sum = [jnp.where((lane_k // GLA_DK) == hh, 1.0, 0.0).astype(BF16) for hh in range(2)]
    sub = GLA_SUBCHUNK
    key_row = lax.broadcasted_iota(jnp.int32, (c, pair_w), 0)
    lane_h = lax.broadcasted_iota(jnp.int32, (1, pair_w), 1) // GLA_DK
    head_lanes = [jnp.where(lane_h == hh, 1.0, 0.0) for hh in range(2)]
    srow = lax.broadcasted_iota(jnp.int32, (pair_w, pair_v), 0) // GLA_DK
    scol = lax.broadcasted_iota(jnp.int32, (pair_w, pair_v), 1) // GLA_DV
    own_head = srow == scol

    def chunk(ci, carry):
        r0 = pl.multiple_of(ci * c, c)
        rows = pl.ds(r0, c)
        x = _dot(al_ref[rows, :].astype(BF16), wa_ref[...]) + ba_ref[...]
        log_a = (jnp.minimum(x, 0.0) - jnp.log(1.0 + jnp.exp(-jnp.abs(x)))) * (1.0 / GLA_TAU)
        p1 = log_a.astype(BF16)
        r1 = log_a - p1.astype(F32)
        p2 = r1.astype(BF16)
        p3 = (r1 - p2.astype(F32)).astype(BF16)
        b_scr[...] = _dot(tri, p1) + _dot(tri, p2) + _dot(tri, p3)
        q_scr[...] = q_ref[rows, :] * (GLA_DK ** -0.5)
        k_scr[...] = k_ref[rows, :]
        v_scr[...] = v_ref[rows, :]

        staged = []
        for p in range(n_pairs):
            kl = slice(p * pair_w, (p + 1) * pair_w)
            vl = slice(p * pair_v, (p + 1) * pair_v)
            bp = b_scr[:, kl]
            qp = q_scr[:, kl]
            kp = k_scr[:, kl]
            vp = v_scr[:, vl]
            b_last = b_scr[c - 1:c, kl]
            st = state_ref[p]
            q_dec = (qp * jnp.exp(bp)).astype(BF16)
            k_dec = (kp * jnp.exp(b_last - bp)).astype(BF16)
            o_inter = _dot(q_dec, st.astype(BF16))
            vp_bf = vp.astype(BF16)
            kv = lax.dot_general(k_dec, vp_bf, (((0,), (0,)), ((), ())), preferred_element_type=F32)
            decay_col = jnp.transpose(jnp.broadcast_to(jnp.exp(b_last), (LANES, pair_w)))
            decay_col = jnp.concatenate([decay_col, decay_col], axis=1)
            state_ref[p] = decay_col * st + jnp.where(own_head, kv, 0.0)
            q_blocks, k_blocks = [], []
            for i in range(1, c // sub):
                r = i * sub
                b_r = b_scr[r:r + 1, kl]
                q_i = q_scr[r:r + sub, kl] * jnp.exp(b_scr[r:r + sub, kl] - b_r)
                pieces = [jnp.zeros((r, pair_w), F32), q_i]
                if c - r - sub:
                    pieces.append(jnp.zeros((c - r - sub, pair_w), F32))
                q_blocks.append(jnp.concatenate(pieces, axis=0))
                k_blocks.append(kp * jnp.exp(jnp.where(key_row < r, b_r - bp, -jnp.inf)))
            q_cat = jnp.concatenate(q_blocks, axis=1)
            k_cat = jnp.concatenate(k_blocks, axis=1).astype(BF16)
            a_far = [_dot_t((q_cat * jnp.concatenate([head_lanes[hh]] * len(q_blocks), axis=1)).astype(BF16), k_cat)
                     for hh in range(2)]
            ys = []
            for i in range(c // sub):
                r = i * sub
                for t0 in (r, r + 8):
                    n = r + sub - t0
                    bt = b_scr[t0:r + sub, kl]
                    qt = q_scr[t0:r + sub, kl]
                    t_idx = t0 + lax.broadcasted_iota(jnp.int32, (n, pair_w), 0)
                    for s in range(t0, t0 + 8):
                        e = jnp.exp(jnp.where(t_idx >= s, bt - b_scr[s:s + 1, kl], -jnp.inf))
                        ys.append(qt * e * k_scr[s:s + 1, kl])
            y = jnp.concatenate(ys, axis=0).astype(BF16)
            a_near = [_dot(y, head_sum[hh]) for hh in range(2)]
            staged.append((o_inter, vp_bf, a_far, a_near))

        for p in range(n_pairs):
            o_inter, vp_bf, a_far, a_near = staged[p]
            for hh in range(2):
                hv = slice(p * pair_v + hh * GLA_DV, p * pair_v + (hh + 1) * GLA_DV)
                o_h = o_inter[:, hh * GLA_DV:(hh + 1) * GLA_DV] + _dot(a_far[hh].astype(BF16),
                                                                       vp_bf[:, hh * GLA_DV:(hh + 1) * GLA_DV])
                a = a_near[hh]
                for i in range(c // sub):
                    r = i * sub
                    base = i * (8 * sub + 8 * 8)
                    upd = a[base:base + sub] * v_scr[r:r + 1, hv]
                    for idx in range(1, 8):
                        upd += a[base + idx * sub:base + (idx + 1) * sub] * v_scr[r + idx:r + idx + 1, hv]
                    base += 8 * sub
                    upd8 = a[base:base + 8] * v_scr[r + 8:r + 9, hv]
                    for idx in range(1, 8):
                        upd8 += a[base + idx * 8:base + (idx + 1) * 8] * v_scr[r + 8 + idx:r + 9 + idx, hv]
                    upd = jnp.concatenate([upd[0:8], upd[8:sub] + upd8], axis=0)
                    oi_scr[r:r + sub, hv] = upd + o_h[r:r + sub]

        o = oi_scr[...]
        gate = r_ref[rows, :]
        gate = gate * jax.nn.sigmoid(gate)
        for h in range(GLA_HEADS):
            hv = slice(h * GLA_DV, (h + 1) * GLA_DV)
            oh = o[:, hv]
            oh = oh * lax.rsqrt(jnp.mean(oh * oh, axis=-1, keepdims=True) + EPS)
            o_ref[rows, hv] = (oh * ng_ref[:, hv] * gate[:, hv]).astype(o_ref.dtype)
        return carry

    lax.fori_loop(0, tc // c, chunk, 0)


def gla(z, w_a2, b_a, norm_g, layer, *, batch, seq, tc):
    assert seq % tc == 0 and tc % GLA_CHUNK == 0
    nb = seq // tc
    kw = GLA_HEADS * GLA_DK
    vw = GLA_HEADS * GLA_DV
    tok = lambda width, col: pl.BlockSpec((tc, width), lambda b, i: (b * nb + i, col // width))
    full = lambda a: _layer(layer, a.shape[1:], lambda b, i: (0, 0))
    c = GLA_CHUNK
    return pl.pallas_call(
        functools.partial(_gla_kernel, tc=tc),
        grid=(batch, nb),
        in_specs=[
            tok(kw, Z_GQ), tok(kw, Z_GK), tok(vw, Z_GV), tok(vw, Z_GR), tok(LANES, Z_AL),
            full(w_a2), full(b_a), full(norm_g),
        ],
        out_specs=pl.BlockSpec((tc, vw), lambda b, i: (b * nb + i, 0)),
        out_shape=jax.ShapeDtypeStruct((batch * seq, vw), BF16),
        scratch_shapes=[
            pltpu.VMEM((GLA_HEADS // 2, 2 * GLA_DK, 2 * GLA_DV), F32),
            pltpu.VMEM((c, kw), F32), pltpu.VMEM((c, kw), F32), pltpu.VMEM((c, kw), F32),
            pltpu.VMEM((c, vw), F32), pltpu.VMEM((c, vw), F32),
        ],
        compiler_params=_params("arbitrary", "arbitrary"),
        name="gla",
    )(z, z, z, z, z, w_a2, b_a, norm_g)


def _conv_kernel(x_ref, halo_ref, w_ref, o_ref, *, blocks_per_seq):
    d = CONV_DIM
    x = x_ref[...]
    z = x[:, d:2 * d] * x[:, 2 * d:3 * d]
    hz = halo_ref[:, d:2 * d] * halo_ref[:, 2 * d:3 * d]
    hz = jnp.where(pl.program_id(0) % blocks_per_seq == 0, 0.0, hz)
    row = lax.broadcasted_iota(jnp.int32, z.shape, 0)
    z1 = jnp.where(row == 0, hz[7:8], pltpu.roll(z, 1, 0))
    z2 = jnp.where(row == 0, hz[6:7], jnp.where(row == 1, hz[7:8], pltpu.roll(z, 2, 0)))
    y = w_ref[2:3, :] * z + w_ref[1:2, :] * z1 + w_ref[0:1, :] * z2
    o_ref[...] = (x[:, 0:d] * y).astype(o_ref.dtype)


def short_conv(z, w, layer, *, seq, tm):
    m = z.shape[0]
    assert seq % tm == 0 and tm % 8 == 0
    width = 3 * CONV_DIM
    return pl.pallas_call(
        functools.partial(_conv_kernel, blocks_per_seq=seq // tm),
        grid=(m // tm,),
        in_specs=[
            pl.BlockSpec((tm, width), lambda i: (i, Z_CONV // width)),
            pl.BlockSpec((8, width), lambda i: (jnp.maximum(i * (tm // 8) - 1, 0), Z_CONV // width)),
            _layer(layer, w.shape[1:], lambda i: (0, 0)),
        ],
        out_specs=pl.BlockSpec((tm, CONV_DIM), lambda i: (i, 0)),
        out_shape=jax.ShapeDtypeStruct((m, CONV_DIM), BF16),
        compiler_params=_params("arbitrary"),
        name="short_conv",
    )(z, z, w)


def _merge_kernel(h_ref, g_ref, a0_ref, a1_ref, a2_ref, wg0_ref, wg1_ref, wg2_ref, b0_ref, b1_ref, b2_ref,
                  wp0_ref, wp1_ref, wp2_ref, wo_ref, o_ref, u_ref):
    j = pl.program_id(1)

    @pl.when(j == 0)
    def _():
        h = h_ref[...]
        u_ref[...] = _rms(h, g_ref[...]).astype(BF16)
        o_ref[...] = h

    u = u_ref[...]
    merged = None
    for a_ref, wg_ref, b_ref, wp_ref in ((a0_ref, wg0_ref, b0_ref, wp0_ref),
                                         (a1_ref, wg1_ref, b1_ref, wp1_ref),
                                         (a2_ref, wg2_ref, b2_ref, wp2_ref)):
        gate = jax.nn.sigmoid(_dot(u, wg_ref[...]) + b_ref[...])
        term = gate * _dot(a_ref[...], wp_ref[...])
        merged = term if merged is None else merged + term
    o_ref[...] += _dot(merged.astype(BF16), wo_ref[...])


def merge(h, g, branches, w_gate, gate_col0, b_gate, w_projs, w_out, layer, *, tm, tn):
    m, d = h.shape
    assert m % tm == 0 and d % tn == 0 and gate_col0 % tn == 0
    nj = d // tn
    g0 = gate_col0 // tn
    row = lambda a: pl.BlockSpec((tm, a.shape[1]), lambda i, j: (i, 0))
    gate_w = lambda b: _layer(layer, (d, tn), lambda i, j: (0, g0 + b * nj + j))
    gate_b = lambda b: _layer(layer, (1, tn), lambda i, j: (0, b * nj + j))
    proj_w = lambda w: _layer(layer, (w.shape[1], tn), lambda i, j: (0, j))
    return pl.pallas_call(
        _merge_kernel,
        grid=(m // tm, nj),
        in_specs=[
            row(h), _vec(layer, d),
            row(branches[0]), row(branches[1]), row(branches[2]),
            gate_w(0), gate_w(1), gate_w(2), gate_b(0), gate_b(1), gate_b(2),
            proj_w(w_projs[0]), proj_w(w_projs[1]), proj_w(w_projs[2]),
            _layer(layer, (tn, d), lambda i, j: (j, 0)),
        ],
        out_specs=pl.BlockSpec((tm, d), lambda i, j: (i, 0)),
        out_shape=jax.ShapeDtypeStruct((m, d), F32),
        scratch_shapes=[pltpu.VMEM((tm, d), BF16)],
        compiler_params=_params("arbitrary", "arbitrary"),
        name="merge",
    )(h, g, *branches, w_gate, w_gate, w_gate, b_gate, b_gate, b_gate, *w_projs, w_out)


def _xattn_kernel(h_ref, g_ref, wq_ref, k_ref, v_ref, wo_ref, o_ref):
    h = h_ref[...]
    hn = _rms(h, g_ref[...]).astype(BF16)
    q = (_dot(hn, wq_ref[...]) * (X_HEAD_DIM ** -0.5)).astype(BF16)
    heads = [slice(j * X_HEAD_DIM, (j + 1) * X_HEAD_DIM) for j in range(X_HEADS)]
    scores = [_dot_t(q[:, hs], k_ref[:, hs]) for hs in heads]
    outs = []
    for hs, s in zip(heads, scores):
        p = jnp.exp(s - jnp.max(s, axis=-1, keepdims=True))
        o = _dot(p.astype(BF16), v_ref[:, hs]) / jnp.sum(p, axis=-1, keepdims=True)
        outs.append(o.astype(BF16))
    o_ref[...] = h + _dot(jnp.concatenate(outs, axis=1), wo_ref[...])


def xattn(h, g, w_q, kv, w_o, layer, *, seq, tm):
    m, d = h.shape
    assert seq % tm == 0
    nb = seq // tm
    once = pl.Buffered(1)
    return pl.pallas_call(
        _xattn_kernel,
        grid=(m // tm,),
        in_specs=[
            pl.BlockSpec((tm, d), lambda i: (i, 0)),
            _vec(layer, d),
            pl.BlockSpec((None, d, d), lambda i: (layer, 0, 0), pipeline_mode=once),
            pl.BlockSpec((MEM_LEN, d), lambda i: (i // nb, 0)),
            pl.BlockSpec((MEM_LEN, d), lambda i: (i // nb, 1)),
            pl.BlockSpec((None, d, d), lambda i: (layer, 0, 0), pipeline_mode=once),
        ],
        out_specs=pl.BlockSpec((tm, d), lambda i: (i, 0)),
        out_shape=jax.ShapeDtypeStruct((m, d), F32),
        compiler_params=_params("arbitrary"),
        name="xattn",
    )(h, g, w_q, kv, kv, w_o)


PACK_BLOCK = 512
_PACK_SRC = (2640, 3152, 3664, 0, 512, 1600, 2112, 1088, 1024)
_PACK_SPECIAL = len(_PACK_SRC) - 1
_SRC_A_LOW = 2624
_SRC_GATE = 4176
GATE_COL0 = len(_PACK_SRC) * PACK_BLOCK


def _pack_in_proj_kernel(src_ref, w_t_ref, a_low_ref, o_ref):
    del src_ref
    j = pl.program_id(1)

    @pl.when(j != _PACK_SPECIAL)
    def _():
        o_ref[...] = jnp.transpose(w_t_ref[0]).astype(BF16)

    @pl.when(j == _PACK_SPECIAL)
    def _():
        half = MLA_ROPE // 2
        k_rope = w_t_ref[0, 0:MLA_ROPE, :]
        row = lax.broadcasted_iota(jnp.int32, a_low_ref.shape[1:], 0)
        a_low = jnp.where(row < GLA_RANK, a_low_ref[0], 0.0)
        rest = jnp.zeros((PACK_BLOCK - 2 * LANES, k_rope.shape[1]), F32)
        blk = jnp.concatenate([k_rope, k_rope[half:], k_rope[:half], a_low, rest], axis=0)
        o_ref[...] = jnp.transpose(blk).astype(BF16)


def _pack_in_proj(w):
    depth, d, n = w.shape
    w_t = jnp.swapaxes(w, 1, 2)
    n_gate = n - _SRC_GATE
    assert n_gate % PACK_BLOCK == 0
    rows = _PACK_SRC + tuple(_SRC_GATE + PACK_BLOCK * k for k in range(n_gate // PACK_BLOCK))
    assert all(r % SUBLANES == 0 for r in rows) and _SRC_A_LOW % SUBLANES == 0
    src = jnp.asarray([r // SUBLANES for r in rows], jnp.int32)
    n_blocks = src.shape[0]
    grid_spec = pltpu.PrefetchScalarGridSpec(
        num_scalar_prefetch=1,
        grid=(depth, n_blocks),
        in_specs=[
            pl.BlockSpec((pl.Element(1), pl.Element(PACK_BLOCK), pl.Element(d)),
                         lambda l, j, src: (l, src[j] * SUBLANES, 0)),
            pl.BlockSpec((pl.Element(1), pl.Element(LANES), pl.Element(d)), lambda l, j, src: (l, _SRC_A_LOW, 0)),
        ],
        out_specs=pl.BlockSpec((None, d, PACK_BLOCK), lambda l, j, src: (l, 0, j)),
    )
    return pl.pallas_call(
        _pack_in_proj_kernel,
        grid_spec=grid_spec,
        out_shape=jax.ShapeDtypeStruct((depth, d, n_blocks * PACK_BLOCK), BF16),
        compiler_params=_params("arbitrary", "arbitrary"),
        name="pack_in_proj",
    )(src, w_t, w_t)


def _pack_uq(w):
    depth = w.shape[0]
    w = w.reshape(depth, MLA_RANK, MLA_HEADS, MLA_NOPE + MLA_ROPE)
    half = MLA_ROPE // 2
    nope, pe = w[..., :MLA_NOPE], w[..., MLA_NOPE:]
    swap = jnp.concatenate([pe[..., half:], pe[..., :half]], axis=-1)
    packed = jnp.concatenate([nope, pe, swap], axis=-1).reshape(depth, MLA_RANK, MLA_HEADS * MLA_QK_PAD)
    return jnp.swapaxes(packed, 1, 2).astype(BF16)


def _pack_ukv(w):
    depth = w.shape[0]
    w = w.reshape(depth, MLA_RANK, MLA_HEADS, MLA_NOPE + MLA_V)
    wk = w[..., :MLA_NOPE].reshape(depth, MLA_RANK, MLA_HEADS * MLA_NOPE)
    wv = w[..., MLA_NOPE:].reshape(depth, MLA_RANK, MLA_HEADS * MLA_V)
    return wk.astype(BF16), jnp.swapaxes(wv, 1, 2).astype(BF16)


def _rope_table(seq):
    inv_freq = 1.0 / (ROPE_THETA ** (jnp.arange(0, MLA_ROPE, 2, dtype=F32) / MLA_ROPE))
    ang = jnp.arange(seq, dtype=F32)[:, None] * inv_freq[None, :]
    cos, sin = jnp.cos(ang), jnp.sin(ang)
    return jnp.concatenate([cos, cos, -sin, sin], axis=1)


def kernel(x, mem, ffn1_norm, ffn1_w_in, ffn1_w_out, mix_norm, mix_w_in, mix_b_gate, mla_q_norm, mla_kv_norm,
           mla_w_uq, mla_w_ukv, mla_w_proj, gla_w_a2, gla_b_a, gla_norm, gla_w_proj, conv_w, conv_w_proj,
           mix_w_out, xattn_norm, mem_norm, xattn_w_q, xattn_w_kv, xattn_w_o, ffn2_norm, ffn2_w_in,
           ffn2_w_out, final_norm):
    batch, seq, d = x.shape
    depth = ffn1_norm.shape[0]
    tokens = batch * seq
    table = _rope_table(seq)
    h = x.reshape(tokens, d)
    mem2 = mem.reshape(batch * mem.shape[1], d)
    bf = lambda w: w.astype(BF16)
    vec = lambda p: p.reshape(depth, 1, -1)

    w_mix_in = _pack_in_proj(mix_w_in)
    wq_t = _pack_uq(mla_w_uq)
    wk, wv_t = _pack_ukv(mla_w_ukv)
    w_a2 = bf(jnp.concatenate(
        [gla_w_a2, jnp.zeros((depth, LANES - GLA_RANK, GLA_HEADS * GLA_DK), F32)], axis=1))
    w_projs = (bf(mla_w_proj), bf(gla_w_proj), bf(conv_w_proj))
    w_mix_out, w_xq, w_xo = bf(mix_w_out), bf(xattn_w_q), bf(xattn_w_o)
    conv_taps = conv_w.reshape(depth, 3, CONV_DIM)

    for l in range(depth):
        h = ffn(h, vec(ffn1_norm), ffn1_w_in, ffn1_w_out, final_norm, l, tm=1024, tf=256, final_norm=False)

        z = norm_matmul(h, vec(mix_norm), w_mix_in, l, n=Z_WIDTH, tm=512, tn=Z_WIDTH // 2, out_dtype=F32)
        q_t, k, v_t = mla_prep(z, table, vec(mla_q_norm), vec(mla_kv_norm), wq_t, wk, wv_t, l,
                               batch=batch, seq=seq, tm=512)
        a_mla = mla_attn(q_t, k, v_t, tq=512, heads=4).reshape(tokens, MLA_HEADS * MLA_V)
        a_gla = gla(z, w_a2, vec(gla_b_a), vec(gla_norm), l, batch=batch, seq=seq, tc=512)
        a_conv = short_conv(z, conv_taps, l, seq=seq, tm=512)
        h = merge(h, vec(mix_norm), (a_mla, a_gla, a_conv), w_mix_in, GATE_COL0, vec(mix_b_gate), w_projs,
                  w_mix_out, l, tm=512, tn=512)

        kv = norm_matmul(mem2, vec(mem_norm), xattn_w_kv, l, n=2 * d, tm=mem2.shape[0], tn=1024, out_dtype=BF16)
        h = xattn(h, vec(xattn_norm), w_xq, kv, w_xo, l, seq=seq, tm=512)

        h = ffn(h, vec(ffn2_norm), ffn2_w_in, ffn2_w_out, final_norm, l, tm=1024, tf=256,
                final_norm=(l == depth - 1))
    return h.reshape(batch, seq, d)
```

```python
import functools

import jax
import jax.numpy as jnp
from jax import lax
from jax.experimental import pallas as pl
from jax.experimental.pallas import tpu as pltpu

F32 = jnp.float32
BF16 = jnp.bfloat16

EPS = 1e-6
D_MODEL = 2048
D_FF = 5632
MLA_HEADS = 8
MLA_RANK = 512
MLA_NOPE = 128
MLA_ROPE = 64
MLA_V = 128
MLA_QK_PAD = 256
ROPE_THETA = 10000.0
LOG2_E = 1.4426950408889634
GLA_HEADS = 4
GLA_DK = 64
GLA_DV = 128
GLA_RANK = 16
GLA_TAU = 16.0
GLA_CHUNK = 64
GLA_SUBCHUNK = 16
CONV_DIM = 512
X_HEADS = 4
X_HEAD_DIM = D_MODEL // X_HEADS
MEM_LEN = 256

LANES = 128
SUBLANES = 8
VMEM_LIMIT_BYTES = 56 * 1024 * 1024

Z_CONV = 0
Z_CQ = 1536
Z_CKV = 2048
Z_GV = 2560
Z_GR = 3072
Z_GQ = 3584
Z_GK = 3840
Z_KR = 4096
Z_AL = 4224
Z_WIDTH = 4352


def _params(*sem):
    return pltpu.CompilerParams(dimension_semantics=sem, vmem_limit_bytes=VMEM_LIMIT_BYTES)


def _rms(x, g):
    return x * lax.rsqrt(jnp.mean(x * x, axis=-1, keepdims=True) + EPS) * g


def _dot(a, b):
    return jnp.dot(a, b, preferred_element_type=F32)


def _dot_t(a, b):
    return lax.dot_general(a, b, (((1,), (1,)), ((), ())), preferred_element_type=F32)


def _norm_matmul_kernel(x_ref, g_ref, w_ref, o_ref):
    xn = _rms(x_ref[...], g_ref[...]).astype(BF16)
    o_ref[...] = _dot(xn, w_ref[...].astype(BF16)).astype(o_ref.dtype)


def _layer(layer, block, index):
    return pl.BlockSpec((None,) + tuple(block), lambda *ids: (layer,) + tuple(index(*ids)))


def _vec(layer, width):
    return _layer(layer, (1, width), lambda *ids: (0, 0))


def norm_matmul(x, g, w, layer, *, n, tm, tn, out_dtype):
    m, k = x.shape
    assert m % tm == 0 and n % tn == 0 and n <= w.shape[2]
    return pl.pallas_call(
        _norm_matmul_kernel,
        grid=(n // tn, m // tm),
        in_specs=[
            pl.BlockSpec((tm, k), lambda j, i: (i, 0)),
            _vec(layer, k),
            _layer(layer, (k, tn), lambda j, i: (0, j)),
        ],
        out_specs=pl.BlockSpec((tm, tn), lambda j, i: (i, j)),
        out_shape=jax.ShapeDtypeStruct((m, n), out_dtype),
        compiler_params=_params("arbitrary", "arbitrary"),
        name="norm_matmul",
    )(x, g, w)


def _ffn_kernel(x_ref, g_ref, wg_ref, wu_ref, wo_ref, gf_ref, o_ref, xn_ref, *, final_norm):
    j = pl.program_id(1)

    @pl.when(j == 0)
    def _():
        x = x_ref[...]
        xn_ref[...] = _rms(x, g_ref[...]).astype(BF16)
        o_ref[...] = x

    xn = xn_ref[...]
    gate = _dot(xn, wg_ref[...].astype(BF16))
    up = _dot(xn, wu_ref[...].astype(BF16))
    act = (0.5 * gate) * jax.nn.sigmoid(gate) * up
    o_ref[...] += _dot(act.astype(BF16), wo_ref[...].astype(BF16))

    if final_norm:
        @pl.when(j == pl.num_programs(1) - 1)
        def _():
            o_ref[...] = _rms(o_ref[...], gf_ref[...])


def ffn(x, g, w_in, w_out, g_final, layer, *, tm, tf, final_norm):
    m, d = x.shape
    f = w_out.shape[1]
    assert m % tm == 0 and f % tf == 0
    nf = f // tf
    return pl.pallas_call(
        functools.partial(_ffn_kernel, final_norm=final_norm),
        grid=(m // tm, nf),
        in_specs=[
            pl.BlockSpec((tm, d), lambda i, j: (i, 0)),
            _vec(layer, d),
            _layer(layer, (d, tf), lambda i, j: (0, j)),
            _layer(layer, (d, tf), lambda i, j: (0, nf + j)),
            _layer(layer, (tf, d), lambda i, j: (j, 0)),
            pl.BlockSpec((1, d), lambda i, j: (0, 0)),
        ],
        out_specs=pl.BlockSpec((tm, d), lambda i, j: (i, 0)),
        out_shape=jax.ShapeDtypeStruct((m, d), F32),
        scratch_shapes=[pltpu.VMEM((tm, d), BF16)],
        compiler_params=_params("arbitrary", "arbitrary"),
        name="ffn",
    )(x, g, w_in, w_in, w_out, g_final.reshape(1, d))


def _rope_pairs(x, table):
    y = x * table
    return y + pltpu.roll(y, MLA_ROPE, 1)


def _mla_prep_kernel(cq_ref, ckv_ref, kr_ref, tab_ref, tab_t_ref, gq_ref, gkv_ref, wq_t_ref, wk_ref, wv_t_ref,
                     q_t_ref, k_ref, v_t_ref):
    scale = (MLA_NOPE + MLA_ROPE) ** -0.5 * LOG2_E
    nq_t = jnp.transpose(_rms(cq_ref[...], gq_ref[...])).astype(BF16)
    q_t = _dot(wq_t_ref[...], nq_t) * scale
    nkv = _rms(ckv_ref[...], gkv_ref[...])
    kn = _dot(nkv.astype(BF16), wk_ref[...])
    v_t = _dot(wv_t_ref[...], jnp.transpose(nkv).astype(BF16))
    tab = tab_ref[...]
    tab_t = tab_t_ref[...]
    lane = lax.broadcasted_iota(jnp.int32, tab.shape, 1)
    k_pe = jnp.where(lane < MLA_ROPE, _rope_pairs(kr_ref[...], tab), 0.0).astype(BF16)
    zeros = jnp.zeros((MLA_QK_PAD - MLA_NOPE - MLA_ROPE, q_t.shape[1]), BF16)
    for h in range(MLA_HEADS):
        q0 = h * MLA_QK_PAD
        q_t_ref[0, h, 0:MLA_NOPE, :] = q_t[q0:q0 + MLA_NOPE].astype(BF16)
        pe = q_t[q0 + MLA_NOPE:q0 + MLA_QK_PAD] * tab_t
        q_t_ref[0, h, MLA_NOPE:MLA_NOPE + MLA_ROPE, :] = (pe[0:MLA_ROPE] + pe[MLA_ROPE:2 * MLA_ROPE]).astype(BF16)
        q_t_ref[0, h, MLA_NOPE + MLA_ROPE:MLA_QK_PAD, :] = zeros
        k_ref[0, h, :, 0:LANES] = kn[:, h * LANES:(h + 1) * LANES].astype(BF16)
        k_ref[0, h, :, LANES:2 * LANES] = k_pe
        v_t_ref[0, h] = v_t[h * MLA_V:(h + 1) * MLA_V].astype(BF16)


def mla_prep(z, table, g_q, g_kv, wq_t, wk, wv_t, layer, *, batch, seq, tm):
    assert seq % tm == 0
    nb = seq // tm
    r = MLA_RANK
    hw = MLA_HEADS
    tok = lambda width, col: pl.BlockSpec((tm, width), lambda i: (i, col // width))
    full = lambda a: _layer(layer, a.shape[1:], lambda i: (0, 0))
    feat_major = lambda width: pl.BlockSpec((1, hw, width, tm), lambda i: (i // nb, 0, 0, i % nb))
    return pl.pallas_call(
        _mla_prep_kernel,
        grid=(batch * nb,),
        in_specs=[
            tok(r, Z_CQ), tok(r, Z_CKV), tok(LANES, Z_KR),
            pl.BlockSpec((tm, LANES), lambda i: (i % nb, 0)),
            pl.BlockSpec((LANES, tm), lambda i: (0, i % nb)),
            full(g_q), full(g_kv), full(wq_t), full(wk), full(wv_t),
        ],
        out_specs=[
            feat_major(MLA_QK_PAD),
            pl.BlockSpec((1, hw, tm, MLA_QK_PAD), lambda i: (i // nb, 0, i % nb, 0)),
            feat_major(MLA_V),
        ],
        out_shape=[
            jax.ShapeDtypeStruct((batch, hw, MLA_QK_PAD, seq), BF16),
            jax.ShapeDtypeStruct((batch, hw, seq, MLA_QK_PAD), BF16),
            jax.ShapeDtypeStruct((batch, hw, MLA_V, seq), BF16),
        ],
        compiler_params=_params("arbitrary"),
        name="mla_prep",
    )(z, z, z, table, jnp.transpose(table), g_q, g_kv, wq_t, wk, wv_t)


def _mla_attn_kernel(q_t_ref, k_ref, v_t_ref, o_ref, m_ref, l_ref, acc_ref, *, tq, heads):
    qi = pl.program_id(2)
    m_ref[...] = jnp.full(m_ref.shape, -jnp.inf, F32)
    l_ref[...] = jnp.zeros(l_ref.shape, F32)
    acc_ref[...] = jnp.zeros(acc_ref.shape, F32)

    def step(j, masked):
        start = pl.multiple_of(j * tq, tq)
        scores = [_dot(k_ref[0, g, pl.ds(start, tq), :], q_t_ref[0, g]) for g in range(heads)]
        for g in range(heads):
            s = scores[g]
            if masked:
                key = lax.broadcasted_iota(jnp.int32, s.shape, 0)
                qry = lax.broadcasted_iota(jnp.int32, s.shape, 1)
                s = jnp.where(key <= qry, s, -jnp.inf)
            m_old = m_ref[g]
            m_new = jnp.maximum(m_old, jnp.max(s, axis=0, keepdims=True))
            p = jnp.exp2(s - m_new)
            alpha = jnp.exp2(m_old - m_new)
            l_ref[g] = alpha * l_ref[g] + jnp.sum(p, axis=0, keepdims=True)
            acc_ref[g] = alpha * acc_ref[g] + _dot(v_t_ref[0, g, :, pl.ds(start, tq)], p.astype(BF16))
            m_ref[g] = m_new

    def body(j, carry):
        step(j, False)
        return carry

    lax.fori_loop(0, qi, body, 0)
    step(qi, True)
    for g in range(heads):
        o_t = acc_ref[g] / l_ref[g]
        o_ref[0, :, g * MLA_V:(g + 1) * MLA_V] = jnp.transpose(o_t).astype(o_ref.dtype)


def mla_attn(q_t, k, v_t, *, tq, heads):
    b, h, s, dq = k.shape
    dv = v_t.shape[2]
    assert s % tq == 0 and h % heads == 0
    return pl.pallas_call(
        functools.partial(_mla_attn_kernel, tq=tq, heads=heads),
        grid=(b, h // heads, s // tq),
        in_specs=[
            pl.BlockSpec((1, heads, dq, tq), lambda bi, hi, qi: (bi, hi, 0, qi)),
            pl.BlockSpec((1, heads, s, dq), lambda bi, hi, qi: (bi, hi, 0, 0), pipeline_mode=pl.Buffered(1)),
            pl.BlockSpec((1, heads, dv, s), lambda bi, hi, qi: (bi, hi, 0, 0), pipeline_mode=pl.Buffered(1)),
        ],
        out_specs=pl.BlockSpec((1, tq, heads * dv), lambda bi, hi, qi: (bi, qi, hi)),
        out_shape=jax.ShapeDtypeStruct((b, s, h * dv), BF16),
        scratch_shapes=[pltpu.VMEM((heads, 1, tq), F32), pltpu.VMEM((heads, 1, tq), F32),
                        pltpu.VMEM((heads, dv, tq), F32)],
        compiler_params=_params("arbitrary", "arbitrary", "arbitrary"),
        name="mla_attn",
    )(q_t, k, v_t)


def _gla_kernel(q_ref, k_ref, v_ref, r_ref, al_ref, wa_ref, ba_ref, ng_ref, o_ref,
                state_ref, b_scr, q_scr, k_scr, v_scr, oi_scr, *, tc):
    c = GLA_CHUNK
    pair_w = 2 * GLA_DK
    pair_v = 2 * GLA_DV
    n_pairs = GLA_HEADS // 2

    @pl.when(pl.program_id(1) == 0)
    def _():
        state_ref[...] = jnp.zeros(state_ref.shape, F32)

    row_i = lax.broadcasted_iota(jnp.int32, (c, c), 0)
    col_i = lax.broadcasted_iota(jnp.int32, (c, c), 1)
    tri = jnp.where(row_i >= col_i, 1.0, 0.0).astype(BF16)
    lane_k = lax.broadcasted_iota(jnp.int32, (pair_w, LANES), 0)
    head_sum = [jnp.where((lane_k // GLA_DK) == hh, 1.0, 0.0).astype(BF16) for hh in range(2)]
    sub = GLA_SUBCHUNK
    key_row = lax.broadcasted_iota(jnp.int32, (c, pair_w), 0)
    lane_h = lax.broadcasted_iota(jnp.int32, (1, pair_w), 1) // GLA_DK
    head_lanes = [jnp.where(lane_h == hh, 1.0, 0.0) for hh in range(2)]
    srow = lax.broadcasted_iota(jnp.int32, (pair_w, pair_v), 0) // GLA_DK
    scol = lax.broadcasted_iota(jnp.int32, (pair_w, pair_v), 1) // GLA_DV
    own_head = srow == scol

    def chunk(ci, carry):
        r0 = pl.multiple_of(ci * c, c)
        rows = pl.ds(r0, c)
        x = _dot(al_ref[rows, :].astype(BF16), wa_ref[...]) + ba_ref[...]
        log_a = (jnp.minimum(x, 0.0) - jnp.log(1.0 + jnp.exp(-jnp.abs(x)))) * (1.0 / GLA_TAU)
        p1 = log_a.astype(BF16)
        r1 = log_a - p1.astype(F32)
        p2 = r1.astype(BF16)
        p3 = (r1 - p2.astype(F32)).astype(BF16)
        b_scr[...] = _dot(tri, p1) + _dot(tri, p2) + _dot(tri, p3)
        q_scr[...] = q_ref[rows, :] * (GLA_DK ** -0.5)
        k_scr[...] = k_ref[rows, :]
        v_scr[...] = v_ref[rows, :]

        staged = []
        for p in range(n_pairs):
            kl = slice(p * pair_w, (p + 1) * pair_w)
            vl = slice(p * pair_v, (p + 1) * pair_v)
            bp = b_scr[:, kl]
            qp = q_scr[:, kl]
            kp = k_scr[:, kl]
            vp = v_scr[:, vl]
            b_last = b_scr[c - 1:c, kl]
            st = state_ref[p]
            q_dec = (qp * jnp.exp(bp)).astype(BF16)
            k_dec = (kp * jnp.exp(b_last - bp)).astype(BF16)
            o_inter = _dot(q_dec, st.astype(BF16))
            vp_bf = vp.astype(BF16)
            kv = lax.dot_general(k_dec, vp_bf, (((0,), (0,)), ((), ())), preferred_element_type=F32)
            decay_col = jnp.transpose(jnp.broadcast_to(jnp.exp(b_last), (LANES, pair_w)))
            decay_col = jnp.concatenate([decay_col, decay_col], axis=1)
            state_ref[p] = decay_col * st + jnp.where(own_head, kv, 0.0)
            q_blocks, k_blocks = [], []
            for i in range(1, c // sub):
                r = i * sub
                b_r = b_scr[r:r + 1, kl]
                q_i = q_scr[r:r + sub, kl] * jnp.exp(b_scr[r:r + sub, kl] - b_r)
                pieces = [jnp.zeros((r, pair_w), F32), q_i]
                if c - r - sub:
                    pieces.append(jnp.zeros((c - r - sub, pair_w), F32))
                q_blocks.append(jnp.concatenate(pieces, axis=0))
                k_blocks.append(kp * jnp.exp(jnp.where(key_row < r, b_r - bp, -jnp.inf)))
            q_cat = jnp.concatenate(q_blocks, axis=1)
            k_cat = jnp.concatenate(k_blocks, axis=1).astype(BF16)
            a_far = [_dot_t((q_cat * jnp.concatenate([head_lanes[hh]] * len(q_blocks), axis=1)).astype(BF16), k_cat)
                     for hh in range(2)]
            ys = []
            for i in range(c // sub):
                r = i * sub
                for t0 in (r, r + 8):
                    n = r + sub - t0
                    bt = b_scr[t0:r + sub, kl]
                    qt = q_scr[t0:r + sub, kl]
                    t_idx = t0 + lax.broadcasted_iota(jnp.int32, (n, pair_w), 0)
                    for s in range(t0, t0 + 8):
                        e = jnp.exp(jnp.where(t_idx >= s, bt - b_scr[s:s + 1, kl], -jnp.inf))
                        ys.append(qt * e * k_scr[s:s + 1, kl])
            y = jnp.concatenate(ys, axis=0).astype(BF16)
            a_near = [_dot(y, head_sum[hh]) for hh in range(2)]
            staged.append((o_inter, vp_bf, a_far, a_near))

        for p in range(n_pairs):
            o_inter, vp_bf, a_far, a_near = staged[p]
            for hh in range(2):
                hv = slice(p * pair_v + hh * GLA_DV, p * pair_v + (hh + 1) * GLA_DV)
                o_h = o_inter[:, hh * GLA_DV:(hh + 1) * GLA_DV] + _dot(a_far[hh].astype(BF16),
                                                                       vp_bf[:, hh * GLA_DV:(hh + 1) * GLA_DV])
                a = a_near[hh]
                for i in range(c // sub):
                    r = i * sub
                    base = i * (8 * sub + 8 * 8)
                    upd = a[base:base + sub] * v_scr[r:r + 1, hv]
                    for idx in range(1, 8):
                        upd += a[base + idx * sub:base + (idx + 1) * sub] * v_scr[r + idx:r + idx + 1, hv]
                    base += 8 * sub
                    upd8 = a[base:base + 8] * v_scr[r + 8:r + 9, hv]
                    for idx in range(1, 8):
                        upd8 += a[base + idx * 8:base + (idx + 1) * 8] * v_scr[r + 8 + idx:r + 9 + idx, hv]
                    upd = jnp.concatenate([upd[0:8], upd[8:sub] + upd8], axis=0)
                    oi_scr[r:r + sub, hv] = upd + o_h[r:r + sub]

        o = oi_scr[...]
        gate = r_ref[rows, :]
        gate = gate * jax.nn.sigmoid(gate)
        for h in range(GLA_HEADS):
            hv = slice(h * GLA_DV, (h + 1) * GLA_DV)
            oh = o[:, hv]
            oh = oh * lax.rsqrt(jnp.mean(oh * oh, axis=-1, keepdims=True) + EPS)
            o_ref[rows, hv] = (oh * ng_ref[:, hv] * gate[:, hv]).astype(o_ref.dtype)
        return carry

    lax.fori_loop(0, tc // c, chunk, 0, unroll=2)


def gla(z, w_a2, b_a, norm_g, layer, *, batch, seq, tc):
    assert seq % tc == 0 and tc % GLA_CHUNK == 0
    nb = seq // tc
    kw = GLA_HEADS * GLA_DK
    vw = GLA_HEADS * GLA_DV
    tok = lambda width, col: pl.BlockSpec((tc, width), lambda b, i: (b * nb + i, col // width))
    full = lambda a: _layer(layer, a.shape[1:], lambda b, i: (0, 0))
    c = GLA_CHUNK
    return pl.pallas_call(
        functools.partial(_gla_kernel, tc=tc),
        grid=(batch, nb),
        in_specs=[
            tok(kw, Z_GQ), tok(kw, Z_GK), tok(vw, Z_GV), tok(vw, Z_GR), tok(LANES, Z_AL),
            full(w_a2), full(b_a), full(norm_g),
        ],
        out_specs=pl.BlockSpec((tc, vw), lambda b, i: (b * nb + i, 0)),
        out_shape=jax.ShapeDtypeStruct((batch * seq, vw), BF16),
        scratch_shapes=[
            pltpu.VMEM((GLA_HEADS // 2, 2 * GLA_DK, 2 * GLA_DV), F32),
            pltpu.VMEM((c, kw), F32), pltpu.VMEM((c, kw), F32), pltpu.VMEM((c, kw), F32),
            pltpu.VMEM((c, vw), F32), pltpu.VMEM((c, vw), F32),
        ],
        compiler_params=_params("arbitrary", "arbitrary"),
        name="gla",
    )(z, z, z, z, z, w_a2, b_a, norm_g)


def _conv_kernel(x_ref, halo_ref, w_ref, o_ref, *, blocks_per_seq):
    d = CONV_DIM
    x = x_ref[...]
    z = x[:, d:2 * d] * x[:, 2 * d:3 * d]
    hz = halo_ref[:, d:2 * d] * halo_ref[:, 2 * d:3 * d]
    hz = jnp.where(pl.program_id(0) % blocks_per_seq == 0, 0.0, hz)
    row = lax.broadcasted_iota(jnp.int32, z.shape, 0)
    z1 = jnp.where(row == 0, hz[7:8], pltpu.roll(z, 1, 0))
    z2 = jnp.where(row == 0, hz[6:7], jnp.where(row == 1, hz[7:8], pltpu.roll(z, 2, 0)))
    y = w_ref[2:3, :] * z + w_ref[1:2, :] * z1 + w_ref[0:1, :] * z2
    o_ref[...] = (x[:, 0:d] * y).astype(o_ref.dtype)


def short_conv(z, w, layer, *, seq, tm):
    m = z.shape[0]
    assert seq % tm == 0 and tm % 8 == 0
    width = 3 * CONV_DIM
    return pl.pallas_call(
        functools.partial(_conv_kernel, blocks_per_seq=seq // tm),
        grid=(m // tm,),
        in_specs=[
            pl.BlockSpec((tm, width), lambda i: (i, Z_CONV // width)),
            pl.BlockSpec((8, width), lambda i: (jnp.maximum(i * (tm // 8) - 1, 0), Z_CONV // width)),
            _layer(layer, w.shape[1:], lambda i: (0, 0)),
        ],
        out_specs=pl.BlockSpec((tm, CONV_DIM), lambda i: (i, 0)),
        out_shape=jax.ShapeDtypeStruct((m, CONV_DIM), BF16),
        compiler_params=_params("arbitrary"),
        name="short_conv",
    )(z, z, w)


def _merge_kernel(h_ref, g_ref, a0_ref, a1_ref, a2_ref, wg0_ref, wg1_ref, wg2_ref, b0_ref, b1_ref, b2_ref,
                  wp0_ref, wp1_ref, wp2_ref, wo_ref, o_ref, u_ref):
    j = pl.program_id(1)

    @pl.when(j == 0)
    def _():
        h = h_ref[...]
        u_ref[...] = _rms(h, g_ref[...]).astype(BF16)
        o_ref[...] = h

    u = u_ref[...]
    merged = None
    for a_ref, wg_ref, b_ref, wp_ref in ((a0_ref, wg0_ref, b0_ref, wp0_ref),
                                         (a1_ref, wg1_ref, b1_ref, wp1_ref),
                                         (a2_ref, wg2_ref, b2_ref, wp2_ref)):
        gate = jax.nn.sigmoid(_dot(u, wg_ref[...]) + b_ref[...])
        term = gate * _dot(a_ref[...], wp_ref[...])
        merged = term if merged is None else merged + term
    o_ref[...] += _dot(merged.astype(BF16), wo_ref[...])


def merge(h, g, branches, w_gate, gate_col0, b_gate, w_projs, w_out, layer, *, tm, tn):
    m, d = h.shape
    assert m % tm == 0 and d % tn == 0 and gate_col0 % tn == 0
    nj = d // tn
    g0 = gate_col0 // tn
    row = lambda a: pl.BlockSpec((tm, a.shape[1]), lambda i, j: (i, 0))
    gate_w = lambda b: _layer(layer, (d, tn), lambda i, j: (0, g0 + b * nj + j))
    gate_b = lambda b: _layer(layer, (1, tn), lambda i, j: (0, b * nj + j))
    proj_w = lambda w: _layer(layer, (w.shape[1], tn), lambda i, j: (0, j))
    return pl.pallas_call(
        _merge_kernel,
        grid=(m // tm, nj),
        in_specs=[
            row(h), _vec(layer, d),
            row(branches[0]), row(branches[1]), row(branches[2]),
            gate_w(0), gate_w(1), gate_w(2), gate_b(0), gate_b(1), gate_b(2),
            proj_w(w_projs[0]), proj_w(w_projs[1]), proj_w(w_projs[2]),
            _layer(layer, (tn, d), lambda i, j: (j, 0)),
        ],
        out_specs=pl.BlockSpec((tm, d), lambda i, j: (i, 0)),
        out_shape=jax.ShapeDtypeStruct((m, d), F32),
        scratch_shapes=[pltpu.VMEM((tm, d), BF16)],
        compiler_params=_params("arbitrary", "arbitrary"),
        name="merge",
    )(h, g, *branches, w_gate, w_gate, w_gate, b_gate, b_gate, b_gate, *w_projs, w_out)


def _xattn_kernel(h_ref, g_ref, wq_ref, k_ref, v_ref, wo_ref, o_ref):
    h = h_ref[...]
    hn = _rms(h, g_ref[...]).astype(BF16)
    q = (_dot(hn, wq_ref[...]) * (X_HEAD_DIM ** -0.5)).astype(BF16)
    heads = [slice(j * X_HEAD_DIM, (j + 1) * X_HEAD_DIM) for j in range(X_HEADS)]
    scores = [_dot_t(q[:, hs], k_ref[:, hs]) for hs in heads]
    outs = []
    for hs, s in zip(heads, scores):
        p = jnp.exp(s - jnp.max(s, axis=-1, keepdims=True))
        o = _dot(p.astype(BF16), v_ref[:, hs]) / jnp.sum(p, axis=-1, keepdims=True)
        outs.append(o.astype(BF16))
    o_ref[...] = h + _dot(jnp.concatenate(outs, axis=1), wo_ref[...])


def xattn(h, g, w_q, kv, w_o, layer, *, seq, tm):
    m, d = h.shape
    assert seq % tm == 0
    nb = seq // tm
    once = pl.Buffered(1)
    return pl.pallas_call(
        _xattn_kernel,
        grid=(m // tm,),
        in_specs=[
            pl.BlockSpec((tm, d), lambda i: (i, 0)),
            _vec(layer, d),
            pl.BlockSpec((None, d, d), lambda i: (layer, 0, 0), pipeline_mode=once),
            pl.BlockSpec((MEM_LEN, d), lambda i: (i // nb, 0)),
            pl.BlockSpec((MEM_LEN, d), lambda i: (i // nb, 1)),
            pl.BlockSpec((None, d, d), lambda i: (layer, 0, 0), pipeline_mode=once),
        ],
        out_specs=pl.BlockSpec((tm, d), lambda i: (i, 0)),
        out_shape=jax.ShapeDtypeStruct((m, d), F32),
        compiler_params=_params("arbitrary"),
        name="xattn",
    )(h, g, w_q, kv, kv, w_o)


PACK_BLOCK = 512
_PACK_SRC = (2640, 3152, 3664, 0, 512, 1600, 2112, 1088, 1024)
_PACK_SPECIAL = len(_PACK_SRC) - 1
_SRC_A_LOW = 2624
_SRC_GATE = 4176
GATE_COL0 = len(_PACK_SRC) * PACK_BLOCK


def _pack_in_proj_kernel(src_ref, w_t_ref, a_low_ref, o_ref):
    del src_ref
    j = pl.program_id(1)

    @pl.when(j != _PACK_SPECIAL)
    def _():
        o_ref[...] = jnp.transpose(w_t_ref[0]).astype(BF16)

    @pl.when(j == _PACK_SPECIAL)
    def _():
        half = MLA_ROPE // 2
        k_rope = w_t_ref[0, 0:MLA_ROPE, :]
        row = lax.broadcasted_iota(jnp.int32, a_low_ref.shape[1:], 0)
        a_low = jnp.where(row < GLA_RANK, a_low_ref[0], 0.0)
        rest = jnp.zeros((PACK_BLOCK - 2 * LANES, k_rope.shape[1]), F32)
        blk = jnp.concatenate([k_rope, k_rope[half:], k_rope[:half], a_low, rest], axis=0)
        o_ref[...] = jnp.transpose(blk).astype(BF16)


def _pack_in_proj(w):
    depth, d, n = w.shape
    w_t = jnp.swapaxes(w, 1, 2)
    n_gate = n - _SRC_GATE
    assert n_gate % PACK_BLOCK == 0
    rows = _PACK_SRC + tuple(_SRC_GATE + PACK_BLOCK * k for k in range(n_gate // PACK_BLOCK))
    assert all(r % SUBLANES == 0 for r in rows) and _SRC_A_LOW % SUBLANES == 0
    src = jnp.asarray([r // SUBLANES for r in rows], jnp.int32)
    n_blocks = src.shape[0]
    grid_spec = pltpu.PrefetchScalarGridSpec(
        num_scalar_prefetch=1,
        grid=(depth, n_blocks),
        in_specs=[
            pl.BlockSpec((pl.Element(1), pl.Element(PACK_BLOCK), pl.Element(d)),
                         lambda l, j, src: (l, src[j] * SUBLANES, 0)),
            pl.BlockSpec((pl.Element(1), pl.Element(LANES), pl.Element(d)), lambda l, j, src: (l, _SRC_A_LOW, 0)),
        ],
        out_specs=pl.BlockSpec((None, d, PACK_BLOCK), lambda l, j, src: (l, 0, j)),
    )
    return pl.pallas_call(
        _pack_in_proj_kernel,
        grid_spec=grid_spec,
        out_shape=jax.ShapeDtypeStruct((depth, d, n_blocks * PACK_BLOCK), BF16),
        compiler_params=_params("arbitrary", "arbitrary"),
        name="pack_in_proj",
    )(src, w_t, w_t)


def _pack_uq(w):
    depth = w.shape[0]
    w = w.reshape(depth, MLA_RANK, MLA_HEADS, MLA_NOPE + MLA_ROPE)
    half = MLA_ROPE // 2
    nope, pe = w[..., :MLA_NOPE], w[..., MLA_NOPE:]
    swap = jnp.concatenate([pe[..., half:], pe[..., :half]], axis=-1)
    packed = jnp.concatenate([nope, pe, swap], axis=-1).reshape(depth, MLA_RANK, MLA_HEADS * MLA_QK_PAD)
    return jnp.swapaxes(packed, 1, 2).astype(BF16)


def _pack_ukv(w):
    depth = w.shape[0]
    w = w.reshape(depth, MLA_RANK, MLA_HEADS, MLA_NOPE + MLA_V)
    wk = w[..., :MLA_NOPE].reshape(depth, MLA_RANK, MLA_HEADS * MLA_NOPE)
    wv = w[..., MLA_NOPE:].reshape(depth, MLA_RANK, MLA_HEADS * MLA_V)
    return wk.astype(BF16), jnp.swapaxes(wv, 1, 2).astype(BF16)


def _rope_table(seq):
    inv_freq = 1.0 / (ROPE_THETA ** (jnp.arange(0, MLA_ROPE, 2, dtype=F32) / MLA_ROPE))
    ang = jnp.arange(seq, dtype=F32)[:, None] * inv_freq[None, :]
    cos, sin = jnp.cos(ang), jnp.sin(ang)
    return jnp.concatenate([cos, cos, -sin, sin], axis=1)


def kernel(x, mem, ffn1_norm, ffn1_w_in, ffn1_w_out, mix_norm, mix_w_in, mix_b_gate, mla_q_norm, mla_kv_norm,
           mla_w_uq, mla_w_ukv, mla_w_proj, gla_w_a2, gla_b_a, gla_norm, gla_w_proj, conv_w, conv_w_proj,
           mix_w_out, xattn_norm, mem_norm, xattn_w_q, xattn_w_kv, xattn_w_o, ffn2_norm, ffn2_w_in,
           ffn2_w_out, final_norm):
    batch, seq, d = x.shape
    depth = ffn1_norm.shape[0]
    tokens = batch * seq
    table = _rope_table(seq)
    h = x.reshape(tokens, d)
    mem2 = mem.reshape(batch * mem.shape[1], d)
    bf = lambda w: w.astype(BF16)
    vec = lambda p: p.reshape(depth, 1, -1)

    w_mix_in = _pack_in_proj(mix_w_in)
    wq_t = _pack_uq(mla_w_uq)
    wk, wv_t = _pack_ukv(mla_w_ukv)
    w_a2 = bf(jnp.concatenate(
        [gla_w_a2, jnp.zeros((depth, LANES - GLA_RANK, GLA_HEADS * GLA_DK), F32)], axis=1))
    w_projs = (bf(mla_w_proj), bf(gla_w_proj), bf(conv_w_proj))
    w_mix_out, w_xq, w_xo = bf(mix_w_out), bf(xattn_w_q), bf(xattn_w_o)
    conv_taps = conv_w.reshape(depth, 3, CONV_DIM)

    for l in range(depth):
        h = ffn(h, vec(ffn1_norm), ffn1_w_in, ffn1_w_out, final_norm, l, tm=1024, tf=256, final_norm=False)

        z = norm_matmul(h, vec(mix_norm), w_mix_in, l, n=Z_WIDTH, tm=512, tn=Z_WIDTH // 2, out_dtype=F32)
        q_t, k, v_t = mla_prep(z, table, vec(mla_q_norm), vec(mla_kv_norm), wq_t, wk, wv_t, l,
                               batch=batch, seq=seq, tm=512)
        a_mla = mla_attn(q_t, k, v_t, tq=512, heads=8).reshape(tokens, MLA_HEADS * MLA_V)
        a_gla = gla(z, w_a2, vec(gla_b_a), vec(gla_norm), l, batch=batch, seq=seq, tc=512)
        a_conv = short_conv(z, conv_taps, l, seq=seq, tm=512)
        h = merge(h, vec(mix_norm), (a_mla, a_gla, a_conv), w_mix_in, GATE_COL0, vec(mix_b_gate), w_projs,
                  w_mix_out, l, tm=512, tn=512)

        kv = norm_matmul(mem2, vec(mem_norm), xattn_w_kv, l, n=2 * d, tm=mem2.shape[0], tn=1024, out_dtype=BF16)
        h = xattn(h, vec(xattn_norm), w_xq, kv, w_xo, l, seq=seq, tm=512)

        h = ffn(h, vec(ffn2_norm), ffn2_w_in, ffn2_w_out, final_norm, l, tm=1024, tf=256,
                final_norm=(l == depth - 1))
    return h.reshape(batch, seq, d)
```

```python
import functools
from typing import NamedTuple

import jax
import jax.numpy as jnp
from jax import lax
from jax.experimental import pallas as pl
from jax.experimental.pallas import tpu as pltpu

F32 = jnp.float32
BF16 = jnp.bfloat16

EPS = 1e-6
D_MODEL = 2048
D_FF = 5632
MLA_HEADS = 8
MLA_RANK = 512
MLA_NOPE = 128
MLA_ROPE = 64
MLA_V = 128
MLA_QK_PAD = 256
ROPE_THETA = 10000.0
LOG2_E = 1.4426950408889634
GLA_HEADS = 4
GLA_DK = 64
GLA_DV = 128
GLA_RANK = 16
GLA_TAU = 16.0
GLA_CHUNK = 64
GLA_SUBCHUNK = 16
CONV_DIM = 512
X_HEADS = 4
X_HEAD_DIM = D_MODEL // X_HEADS
MEM_LEN = 256

LANES = 128
SUBLANES = 8
VMEM_LIMIT_BYTES = 56 * 1024 * 1024


class Tiles(NamedTuple):
    rows: int = 512
    ffn_rows: int = 1024
    ffn_cols: int = 256
    merge_cols: int = 512
    kv_cols: int = 1024
    attn_heads: int = 8


TILES = Tiles()

Z_CONV = 0
Z_CQ = 1536
Z_CKV = 2048
Z_GV = 2560
Z_GR = 3072
Z_GQ = 3584
Z_GK = 3840
Z_KR = 4096
Z_AL = 4224
Z_WIDTH = 4352


def _params(*sem):
    return pltpu.CompilerParams(dimension_semantics=sem, vmem_limit_bytes=VMEM_LIMIT_BYTES)


def _rms(x, g):
    return x * lax.rsqrt(jnp.mean(x * x, axis=-1, keepdims=True) + EPS) * g


def _dot(a, b):
    return jnp.dot(a, b, preferred_element_type=F32)


def _dot_t(a, b):
    return lax.dot_general(a, b, (((1,), (1,)), ((), ())), preferred_element_type=F32)


def _norm_matmul_kernel(x_ref, g_ref, w_ref, o_ref):
    xn = _rms(x_ref[...], g_ref[...]).astype(BF16)
    o_ref[...] = _dot(xn, w_ref[...].astype(BF16)).astype(o_ref.dtype)


def _layer(layer, block, index):
    return pl.BlockSpec((None,) + tuple(block), lambda *ids: (layer,) + tuple(index(*ids)))


def _vec(layer, width):
    return _layer(layer, (1, width), lambda *ids: (0, 0))


def norm_matmul(x, g, w, layer, *, n, tm, tn, out_dtype):
    m, k = x.shape
    assert m % tm == 0 and n % tn == 0 and n <= w.shape[2]
    w_mode = pl.Buffered(1) if n == tn else None
    return pl.pallas_call(
        _norm_matmul_kernel,
        grid=(n // tn, m // tm),
        in_specs=[
            pl.BlockSpec((tm, k), lambda j, i: (i, 0)),
            _vec(layer, k),
            pl.BlockSpec((None, k, tn), lambda j, i: (layer, 0, j), pipeline_mode=w_mode),
        ],
        out_specs=pl.BlockSpec((tm, tn), lambda j, i: (i, j)),
        out_shape=jax.ShapeDtypeStruct((m, n), out_dtype),
        compiler_params=_params("arbitrary", "arbitrary"),
        name="norm_matmul",
    )(x, g, w)


def _ffn_kernel(x_ref, g_ref, wg_ref, wu_ref, wo_ref, gf_ref, o_ref, xn_ref, *, final_norm):
    j = pl.program_id(1)

    @pl.when(j == 0)
    def _():
        x = x_ref[...]
        xn_ref[...] = _rms(x, g_ref[...]).astype(BF16)
        o_ref[...] = x

    xn = xn_ref[...]
    gate = _dot(xn, wg_ref[...].astype(BF16))
    up = _dot(xn, wu_ref[...].astype(BF16))
    act = (0.5 * gate) * jax.nn.sigmoid(gate) * up
    o_ref[...] += _dot(act.astype(BF16), wo_ref[...].astype(BF16))

    if final_norm:
        @pl.when(j == pl.num_programs(1) - 1)
        def _():
            o_ref[...] = _rms(o_ref[...], gf_ref[...])


def ffn(x, g, w_in, w_out, g_final, layer, *, tm, tf, final_norm):
    m, d = x.shape
    f = w_out.shape[1]
    assert m % tm == 0 and f % tf == 0
    nf = f // tf
    return pl.pallas_call(
        functools.partial(_ffn_kernel, final_norm=final_norm),
        grid=(m // tm, nf),
        in_specs=[
            pl.BlockSpec((tm, d), lambda i, j: (i, 0)),
            _vec(layer, d),
            _layer(layer, (d, tf), lambda i, j: (0, j)),
            _layer(layer, (d, tf), lambda i, j: (0, nf + j)),
            _layer(layer, (tf, d), lambda i, j: (j, 0)),
            pl.BlockSpec((1, d), lambda i, j: (0, 0)),
        ],
        out_specs=pl.BlockSpec((tm, d), lambda i, j: (i, 0)),
        out_shape=jax.ShapeDtypeStruct((m, d), F32),
        scratch_shapes=[pltpu.VMEM((tm, d), BF16)],
        compiler_params=_params("arbitrary", "arbitrary"),
        name="ffn",
    )(x, g, w_in, w_in, w_out, g_final.reshape(1, d))


def _rope_pairs(x, table):
    y = x * table
    return y + pltpu.roll(y, MLA_ROPE, 1)


def _mla_prep_kernel(cq_ref, ckv_ref, kr_ref, tab_ref, tab_t_ref, gq_ref, gkv_ref, wq_t_ref, wk_ref, wv_t_ref,
                     q_t_ref, k_ref, v_t_ref):
    scale = (MLA_NOPE + MLA_ROPE) ** -0.5 * LOG2_E
    nq_t = jnp.transpose(_rms(cq_ref[...], gq_ref[...])).astype(BF16)
    q_t = _dot(wq_t_ref[...], nq_t) * scale
    nkv = _rms(ckv_ref[...], gkv_ref[...])
    kn = _dot(nkv.astype(BF16), wk_ref[...])
    v_t = _dot(wv_t_ref[...], jnp.transpose(nkv).astype(BF16))
    tab = tab_ref[...]
    tab_t = tab_t_ref[...]
    lane = lax.broadcasted_iota(jnp.int32, tab.shape, 1)
    k_pe = jnp.where(lane < MLA_ROPE, _rope_pairs(kr_ref[...], tab), 0.0).astype(BF16)
    zeros = jnp.zeros((MLA_QK_PAD - MLA_NOPE - MLA_ROPE, q_t.shape[1]), BF16)
    for h in range(MLA_HEADS):
        q0 = h * MLA_QK_PAD
        q_t_ref[0, h, 0:MLA_NOPE, :] = q_t[q0:q0 + MLA_NOPE].astype(BF16)
        pe = q_t[q0 + MLA_NOPE:q0 + MLA_QK_PAD] * tab_t
        q_t_ref[0, h, MLA_NOPE:MLA_NOPE + MLA_ROPE, :] = (pe[0:MLA_ROPE] + pe[MLA_ROPE:2 * MLA_ROPE]).astype(BF16)
        q_t_ref[0, h, MLA_NOPE + MLA_ROPE:MLA_QK_PAD, :] = zeros
        k_ref[0, h, :, 0:LANES] = kn[:, h * LANES:(h + 1) * LANES].astype(BF16)
        k_ref[0, h, :, LANES:2 * LANES] = k_pe
        v_t_ref[0, h] = v_t[h * MLA_V:(h + 1) * MLA_V].astype(BF16)


def mla_prep(z, table, g_q, g_kv, wq_t, wk, wv_t, layer, *, batch, seq, tm):
    assert seq % tm == 0
    nb = seq // tm
    r = MLA_RANK
    hw = MLA_HEADS
    tok = lambda width, col: pl.BlockSpec((tm, width), lambda i: (i, col // width))
    full = lambda a: _layer(layer, a.shape[1:], lambda i: (0, 0))
    feat_major = lambda width: pl.BlockSpec((1, hw, width, tm), lambda i: (i // nb, 0, 0, i % nb))
    return pl.pallas_call(
        _mla_prep_kernel,
        grid=(batch * nb,),
        in_specs=[
            tok(r, Z_CQ), tok(r, Z_CKV), tok(LANES, Z_KR),
            pl.BlockSpec((tm, LANES), lambda i: (i % nb, 0)),
            pl.BlockSpec((LANES, tm), lambda i: (0, i % nb)),
            full(g_q), full(g_kv), full(wq_t), full(wk), full(wv_t),
        ],
        out_specs=[
            feat_major(MLA_QK_PAD),
            pl.BlockSpec((1, hw, tm, MLA_QK_PAD), lambda i: (i // nb, 0, i % nb, 0)),
            feat_major(MLA_V),
        ],
        out_shape=[
            jax.ShapeDtypeStruct((batch, hw, MLA_QK_PAD, seq), BF16),
            jax.ShapeDtypeStruct((batch, hw, seq, MLA_QK_PAD), BF16),
            jax.ShapeDtypeStruct((batch, hw, MLA_V, seq), BF16),
        ],
        compiler_params=_params("arbitrary"),
        name="mla_prep",
    )(z, z, z, table, jnp.transpose(table), g_q, g_kv, wq_t, wk, wv_t)


def _mla_attn_kernel(q_t_ref, k_ref, v_t_ref, o_ref, m_ref, l_ref, acc_ref, *, tq, heads):
    qi = pl.program_id(2)
    m_ref[...] = jnp.full(m_ref.shape, -jnp.inf, F32)
    l_ref[...] = jnp.zeros(l_ref.shape, F32)
    acc_ref[...] = jnp.zeros(acc_ref.shape, F32)

    def step(j, masked):
        start = pl.multiple_of(j * tq, tq)
        scores = [_dot(k_ref[0, g, pl.ds(start, tq), :], q_t_ref[0, g]) for g in range(heads)]
        for g in range(heads):
            s = scores[g]
            if masked:
                key = lax.broadcasted_iota(jnp.int32, s.shape, 0)
                qry = lax.broadcasted_iota(jnp.int32, s.shape, 1)
                s = jnp.where(key <= qry, s, -jnp.inf)
            m_old = m_ref[g]
            m_new = jnp.maximum(m_old, jnp.max(s, axis=0, keepdims=True))
            p = jnp.exp2(s - m_new)
            alpha = jnp.exp2(m_old - m_new)
            l_ref[g] = alpha * l_ref[g] + jnp.sum(p, axis=0, keepdims=True)
            acc_ref[g] = alpha * acc_ref[g] + _dot(v_t_ref[0, g, :, pl.ds(start, tq)], p.astype(BF16))
            m_ref[g] = m_new

    def body(j, carry):
        step(j, False)
        return carry

    lax.fori_loop(0, qi, body, 0)
    step(qi, True)
    for g in range(heads):
        o_t = acc_ref[g] / l_ref[g]
        o_ref[0, :, g * MLA_V:(g + 1) * MLA_V] = jnp.transpose(o_t).astype(o_ref.dtype)


def mla_attn(q_t, k, v_t, *, tq, heads):
    b, h, s, dq = k.shape
    dv = v_t.shape[2]
    assert s % tq == 0 and h % heads == 0
    return pl.pallas_call(
        functools.partial(_mla_attn_kernel, tq=tq, heads=heads),
        grid=(b, h // heads, s // tq),
        in_specs=[
            pl.BlockSpec((1, heads, dq, tq), lambda bi, hi, qi: (bi, hi, 0, qi)),
            pl.BlockSpec((1, heads, s, dq), lambda bi, hi, qi: (bi, hi, 0, 0), pipeline_mode=pl.Buffered(1)),
            pl.BlockSpec((1, heads, dv, s), lambda bi, hi, qi: (bi, hi, 0, 0), pipeline_mode=pl.Buffered(1)),
        ],
        out_specs=pl.BlockSpec((1, tq, heads * dv), lambda bi, hi, qi: (bi, qi, hi)),
        out_shape=jax.ShapeDtypeStruct((b, s, h * dv), BF16),
        scratch_shapes=[pltpu.VMEM((heads, 1, tq), F32), pltpu.VMEM((heads, 1, tq), F32),
                        pltpu.VMEM((heads, dv, tq), F32)],
        compiler_params=_params("arbitrary", "arbitrary", "arbitrary"),
        name="mla_attn",
    )(q_t, k, v_t)


def _gla_kernel(q_ref, k_ref, v_ref, r_ref, al_ref, cx_ref, halo_ref, wa_ref, ba_ref, ng_ref, taps_ref,
                o_ref, oc_ref, state_ref, b_scr, q_scr, k_scr, v_scr, oi_scr, *, tc):
    c = GLA_CHUNK
    pair_w = 2 * GLA_DK
    pair_v = 2 * GLA_DV
    n_pairs = GLA_HEADS // 2
    sequence_start = pl.program_id(1) == 0

    @pl.when(sequence_start)
    def _():
        state_ref[...] = jnp.zeros(state_ref.shape, F32)

    oc_ref[...] = _short_conv(cx_ref[...], halo_ref[...], taps_ref[...], sequence_start).astype(oc_ref.dtype)

    row_i = lax.broadcasted_iota(jnp.int32, (c, c), 0)
    col_i = lax.broadcasted_iota(jnp.int32, (c, c), 1)
    tri = jnp.where(row_i >= col_i, 1.0, 0.0).astype(BF16)
    lane_k = lax.broadcasted_iota(jnp.int32, (pair_w, LANES), 0)
    head_sum = [jnp.where((lane_k // GLA_DK) == hh, 1.0, 0.0).astype(BF16) for hh in range(2)]
    sub = GLA_SUBCHUNK
    key_row = lax.broadcasted_iota(jnp.int32, (c, pair_w), 0)
    lane_h = lax.broadcasted_iota(jnp.int32, (1, pair_w), 1) // GLA_DK
    head_lanes = [jnp.where(lane_h == hh, 1.0, 0.0) for hh in range(2)]
    srow = lax.broadcasted_iota(jnp.int32, (pair_w, pair_v), 0) // GLA_DK
    scol = lax.broadcasted_iota(jnp.int32, (pair_w, pair_v), 1) // GLA_DV
    own_head = srow == scol

    def chunk(ci, carry):
        r0 = pl.multiple_of(ci * c, c)
        rows = pl.ds(r0, c)
        x = _dot(al_ref[rows, :].astype(BF16), wa_ref[...]) + ba_ref[...]
        log_a = (jnp.minimum(x, 0.0) - jnp.log(1.0 + jnp.exp(-jnp.abs(x)))) * (1.0 / GLA_TAU)
        p1 = log_a.astype(BF16)
        r1 = log_a - p1.astype(F32)
        p2 = r1.astype(BF16)
        p3 = (r1 - p2.astype(F32)).astype(BF16)
        b_scr[...] = _dot(tri, p1) + _dot(tri, p2) + _dot(tri, p3)
        q_scr[...] = q_ref[rows, :] * (GLA_DK ** -0.5)
        k_scr[...] = k_ref[rows, :]
        v_scr[...] = v_ref[rows, :]

        staged = []
        for p in range(n_pairs):
            kl = slice(p * pair_w, (p + 1) * pair_w)
            vl = slice(p * pair_v, (p + 1) * pair_v)
            bp = b_scr[:, kl]
            qp = q_scr[:, kl]
            kp = k_scr[:, kl]
            vp = v_scr[:, vl]
            b_last = b_scr[c - 1:c, kl]
            st = state_ref[p]
            q_dec = (qp * jnp.exp(bp)).astype(BF16)
            k_dec = (kp * jnp.exp(b_last - bp)).astype(BF16)
            o_inter = _dot(q_dec, st.astype(BF16))
            vp_bf = vp.astype(BF16)
            kv = lax.dot_general(k_dec, vp_bf, (((0,), (0,)), ((), ())), preferred_element_type=F32)
            decay_col = jnp.transpose(jnp.broadcast_to(jnp.exp(b_last), (LANES, pair_w)))
            decay_col = jnp.concatenate([decay_col, decay_col], axis=1)
            state_ref[p] = decay_col * st + jnp.where(own_head, kv, 0.0)
            q_blocks, k_blocks = [], []
            for i in range(1, c // sub):
                r = i * sub
                b_r = b_scr[r:r + 1, kl]
                q_i = q_scr[r:r + sub, kl] * jnp.exp(b_scr[r:r + sub, kl] - b_r)
                pieces = [jnp.zeros((r, pair_w), F32), q_i]
                if c - r - sub:
                    pieces.append(jnp.zeros((c - r - sub, pair_w), F32))
                q_blocks.append(jnp.concatenate(pieces, axis=0))
                k_blocks.append(kp * jnp.exp(jnp.where(key_row < r, b_r - bp, -jnp.inf)))
            q_cat = jnp.concatenate(q_blocks, axis=1)
            k_cat = jnp.concatenate(k_blocks, axis=1).astype(BF16)
            a_far = [_dot_t((q_cat * jnp.concatenate([head_lanes[hh]] * len(q_blocks), axis=1)).astype(BF16), k_cat)
                     for hh in range(2)]
            ys = []
            for i in range(c // sub):
                r = i * sub
                for t0 in (r, r + 8):
                    n = r + sub - t0
                    bt = b_scr[t0:r + sub, kl]
                    qt = q_scr[t0:r + sub, kl]
                    t_idx = t0 + lax.broadcasted_iota(jnp.int32, (n, pair_w), 0)
                    for s in range(t0, t0 + 8):
                        e = jnp.exp(jnp.where(t_idx >= s, bt - b_scr[s:s + 1, kl], -jnp.inf))
                        ys.append(qt * e * k_scr[s:s + 1, kl])
            y = jnp.concatenate(ys, axis=0).astype(BF16)
            a_near = [_dot(y, head_sum[hh]) for hh in range(2)]
            staged.append((o_inter, vp_bf, a_far, a_near))

        for p in range(n_pairs):
            o_inter, vp_bf, a_far, a_near = staged[p]
            for hh in range(2):
                hv = slice(p * pair_v + hh * GLA_DV, p * pair_v + (hh + 1) * GLA_DV)
                o_h = o_inter[:, hh * GLA_DV:(hh + 1) * GLA_DV] + _dot(a_far[hh].astype(BF16),
                                                                       vp_bf[:, hh * GLA_DV:(hh + 1) * GLA_DV])
                a = a_near[hh]
                for i in range(c // sub):
                    r = i * sub
                    base = i * (8 * sub + 8 * 8)
                    upd = a[base:base + sub] * v_scr[r:r + 1, hv]
                    for idx in range(1, 8):
                        upd += a[base + idx * sub:base + (idx + 1) * sub] * v_scr[r + idx:r + idx + 1, hv]
                    base += 8 * sub
                    upd8 = a[base:base + 8] * v_scr[r + 8:r + 9, hv]
                    for idx in range(1, 8):
                        upd8 += a[base + idx * 8:base + (idx + 1) * 8] * v_scr[r + 8 + idx:r + 9 + idx, hv]
                    upd = jnp.concatenate([upd[0:8], upd[8:sub] + upd8], axis=0)
                    oi_scr[r:r + sub, hv] = upd + o_h[r:r + sub]

        o = oi_scr[...]
        gate = r_ref[rows, :]
        gate = gate * jax.nn.sigmoid(gate)
        for h in range(GLA_HEADS):
            hv = slice(h * GLA_DV, (h + 1) * GLA_DV)
            oh = o[:, hv]
            oh = oh * lax.rsqrt(jnp.mean(oh * oh, axis=-1, keepdims=True) + EPS)
            o_ref[rows, hv] = (oh * ng_ref[:, hv] * gate[:, hv]).astype(o_ref.dtype)
        return carry

    lax.fori_loop(0, tc // c, chunk, 0, unroll=2)


def gla_and_conv(z, w_a2, b_a, norm_g, conv_taps, layer, *, batch, seq, tc):
    assert seq % tc == 0 and tc % GLA_CHUNK == 0
    nb = seq // tc
    kw = GLA_HEADS * GLA_DK
    vw = GLA_HEADS * GLA_DV
    cw = 3 * CONV_DIM
    tok = lambda width, col: pl.BlockSpec((tc, width), lambda b, i: (b * nb + i, col // width))
    full = lambda a: _layer(layer, a.shape[1:], lambda b, i: (0, 0))
    c = GLA_CHUNK
    return pl.pallas_call(
        functools.partial(_gla_kernel, tc=tc),
        grid=(batch, nb),
        in_specs=[
            tok(kw, Z_GQ), tok(kw, Z_GK), tok(vw, Z_GV), tok(vw, Z_GR), tok(LANES, Z_AL), tok(cw, Z_CONV),
            pl.BlockSpec((SUBLANES, cw),
                         lambda b, i: (jnp.maximum((b * nb + i) * (tc // SUBLANES) - 1, 0), Z_CONV // cw)),
            full(w_a2), full(b_a), full(norm_g), full(conv_taps),
        ],
        out_specs=[pl.BlockSpec((tc, vw), lambda b, i: (b * nb + i, 0)),
                   pl.BlockSpec((tc, CONV_DIM), lambda b, i: (b * nb + i, 0))],
        out_shape=[jax.ShapeDtypeStruct((batch * seq, vw), BF16),
                   jax.ShapeDtypeStruct((batch * seq, CONV_DIM), BF16)],
        scratch_shapes=[
            pltpu.VMEM((GLA_HEADS // 2, 2 * GLA_DK, 2 * GLA_DV), F32),
            pltpu.VMEM((c, kw), F32), pltpu.VMEM((c, kw), F32), pltpu.VMEM((c, kw), F32),
            pltpu.VMEM((c, vw), F32), pltpu.VMEM((c, vw), F32),
        ],
        compiler_params=_params("arbitrary", "arbitrary"),
        name="gla",
    )(z, z, z, z, z, z, z, w_a2, b_a, norm_g, conv_taps)


def _short_conv(x, halo, w, sequence_start):
    d = CONV_DIM
    z = x[:, d:2 * d] * x[:, 2 * d:3 * d]
    hz = halo[:, d:2 * d] * halo[:, 2 * d:3 * d]
    hz = jnp.where(sequence_start, 0.0, hz)
    row = lax.broadcasted_iota(jnp.int32, z.shape, 0)
    z1 = jnp.where(row == 0, hz[7:8], pltpu.roll(z, 1, 0))
    z2 = jnp.where(row == 0, hz[6:7], jnp.where(row == 1, hz[7:8], pltpu.roll(z, 2, 0)))
    y = w[2:3, :] * z + w[1:2, :] * z1 + w[0:1, :] * z2
    return x[:, 0:d] * y


def _merge_kernel(h_ref, g_ref, a0_ref, a1_ref, a2_ref, wg0_ref, wg1_ref, wg2_ref, b0_ref, b1_ref, b2_ref,
                  wp0_ref, wp1_ref, wp2_ref, wo_ref, o_ref, u_ref):
    j = pl.program_id(1)

    @pl.when(j == 0)
    def _():
        h = h_ref[...]
        u_ref[...] = _rms(h, g_ref[...]).astype(BF16)
        o_ref[...] = h

    u = u_ref[...]
    merged = None
    for a_ref, wg_ref, b_ref, wp_ref in ((a0_ref, wg0_ref, b0_ref, wp0_ref),
                                         (a1_ref, wg1_ref, b1_ref, wp1_ref),
                                         (a2_ref, wg2_ref, b2_ref, wp2_ref)):
        gate = jax.nn.sigmoid(_dot(u, wg_ref[...]) + b_ref[...])
        term = gate * _dot(a_ref[...], wp_ref[...])
        merged = term if merged is None else merged + term
    o_ref[...] += _dot(merged.astype(BF16), wo_ref[...])


def merge(h, g, branches, w_gate, gate_col0, b_gate, w_projs, w_out, layer, *, tm, tn):
    m, d = h.shape
    assert m % tm == 0 and d % tn == 0 and gate_col0 % tn == 0
    nj = d // tn
    g0 = gate_col0 // tn
    row = lambda a: pl.BlockSpec((tm, a.shape[1]), lambda i, j: (i, 0))
    gate_w = lambda b: _layer(layer, (d, tn), lambda i, j: (0, g0 + b * nj + j))
    gate_b = lambda b: _layer(layer, (1, tn), lambda i, j: (0, b * nj + j))
    proj_w = lambda w: _layer(layer, (w.shape[1], tn), lambda i, j: (0, j))
    return pl.pallas_call(
        _merge_kernel,
        grid=(m // tm, nj),
        in_specs=[
            row(h), _vec(layer, d),
            row(branches[0]), row(branches[1]), row(branches[2]),
            gate_w(0), gate_w(1), gate_w(2), gate_b(0), gate_b(1), gate_b(2),
            proj_w(w_projs[0]), proj_w(w_projs[1]), proj_w(w_projs[2]),
            _layer(layer, (tn, d), lambda i, j: (j, 0)),
        ],
        out_specs=pl.BlockSpec((tm, d), lambda i, j: (i, 0)),
        out_shape=jax.ShapeDtypeStruct((m, d), F32),
        scratch_shapes=[pltpu.VMEM((tm, d), BF16)],
        compiler_params=_params("arbitrary", "arbitrary"),
        name="merge",
    )(h, g, *branches, w_gate, w_gate, w_gate, b_gate, b_gate, b_gate, *w_projs, w_out)


def _xattn_kernel(h_ref, g_ref, wq_ref, k_ref, v_ref, wo_ref, o_ref):
    h = h_ref[...]
    hn = _rms(h, g_ref[...]).astype(BF16)
    q = (_dot(hn, wq_ref[...]) * (X_HEAD_DIM ** -0.5)).astype(BF16)
    heads = [slice(j * X_HEAD_DIM, (j + 1) * X_HEAD_DIM) for j in range(X_HEADS)]
    scores = [_dot_t(q[:, hs], k_ref[:, hs]) for hs in heads]
    outs = []
    for hs, s in zip(heads, scores):
        p = jnp.exp(s - jnp.max(s, axis=-1, keepdims=True))
        o = _dot(p.astype(BF16), v_ref[:, hs]) / jnp.sum(p, axis=-1, keepdims=True)
        outs.append(o.astype(BF16))
    o_ref[...] = h + _dot(jnp.concatenate(outs, axis=1), wo_ref[...])


def xattn(h, g, w_q, kv, w_o, layer, *, seq, tm):
    m, d = h.shape
    assert seq % tm == 0
    nb = seq // tm
    once = pl.Buffered(1)
    return pl.pallas_call(
        _xattn_kernel,
        grid=(m // tm,),
        in_specs=[
            pl.BlockSpec((tm, d), lambda i: (i, 0)),
            _vec(layer, d),
            pl.BlockSpec((None, d, d), lambda i: (layer, 0, 0), pipeline_mode=once),
            pl.BlockSpec((MEM_LEN, d), lambda i: (i // nb, 0)),
            pl.BlockSpec((MEM_LEN, d), lambda i: (i // nb, 1)),
            pl.BlockSpec((None, d, d), lambda i: (layer, 0, 0), pipeline_mode=once),
        ],
        out_specs=pl.BlockSpec((tm, d), lambda i: (i, 0)),
        out_shape=jax.ShapeDtypeStruct((m, d), F32),
        compiler_params=_params("arbitrary"),
        name="xattn",
    )(h, g, w_q, kv, kv, w_o)


PACK_BLOCK = 512
_PACK_SRC = (2640, 3152, 3664, 0, 512, 1600, 2112, 1088, 1024)
_PACK_SPECIAL = len(_PACK_SRC) - 1
_SRC_A_LOW = 2624
_SRC_GATE = 4176
GATE_COL0 = len(_PACK_SRC) * PACK_BLOCK


def _pack_in_proj_kernel(src_ref, w_t_ref, a_low_ref, o_ref):
    del src_ref
    j = pl.program_id(1)

    @pl.when(j != _PACK_SPECIAL)
    def _():
        o_ref[...] = jnp.transpose(w_t_ref[0]).astype(BF16)

    @pl.when(j == _PACK_SPECIAL)
    def _():
        half = MLA_ROPE // 2
        k_rope = w_t_ref[0, 0:MLA_ROPE, :]
        row = lax.broadcasted_iota(jnp.int32, a_low_ref.shape[1:], 0)
        a_low = jnp.where(row < GLA_RANK, a_low_ref[0], 0.0)
        rest = jnp.zeros((PACK_BLOCK - 2 * LANES, k_rope.shape[1]), F32)
        blk = jnp.concatenate([k_rope, k_rope[half:], k_rope[:half], a_low, rest], axis=0)
        o_ref[...] = jnp.transpose(blk).astype(BF16)


def _pack_in_proj(w):
    depth, d, n = w.shape
    w_t = jnp.swapaxes(w, 1, 2)
    n_gate = n - _SRC_GATE
    assert n_gate % PACK_BLOCK == 0
    rows = _PACK_SRC + tuple(_SRC_GATE + PACK_BLOCK * k for k in range(n_gate // PACK_BLOCK))
    assert all(r % SUBLANES == 0 for r in rows) and _SRC_A_LOW % SUBLANES == 0
    src = jnp.asarray([r // SUBLANES for r in rows], jnp.int32)
    n_blocks = src.shape[0]
    grid_spec = pltpu.PrefetchScalarGridSpec(
        num_scalar_prefetch=1,
        grid=(depth, n_blocks),
        in_specs=[
            pl.BlockSpec((pl.Element(1), pl.Element(PACK_BLOCK), pl.Element(d)),
                         lambda l, j, src: (l, src[j] * SUBLANES, 0)),
            pl.BlockSpec((pl.Element(1), pl.Element(LANES), pl.Element(d)), lambda l, j, src: (l, _SRC_A_LOW, 0)),
        ],
        out_specs=pl.BlockSpec((None, d, PACK_BLOCK), lambda l, j, src: (l, 0, j)),
    )
    return pl.pallas_call(
        _pack_in_proj_kernel,
        grid_spec=grid_spec,
        out_shape=jax.ShapeDtypeStruct((depth, d, n_blocks * PACK_BLOCK), BF16),
        compiler_params=_params("arbitrary", "arbitrary"),
        name="pack_in_proj",
    )(src, w_t, w_t)


def _pack_uq(w):
    depth = w.shape[0]
    w = w.reshape(depth, MLA_RANK, MLA_HEADS, MLA_NOPE + MLA_ROPE)
    half = MLA_ROPE // 2
    nope, pe = w[..., :MLA_NOPE], w[..., MLA_NOPE:]
    swap = jnp.concatenate([pe[..., half:], pe[..., :half]], axis=-1)
    packed = jnp.concatenate([nope, pe, swap], axis=-1).reshape(depth, MLA_RANK, MLA_HEADS * MLA_QK_PAD)
    return jnp.swapaxes(packed, 1, 2).astype(BF16)


def _pack_ukv(w):
    depth = w.shape[0]
    w = w.reshape(depth, MLA_RANK, MLA_HEADS, MLA_NOPE + MLA_V)
    wk = w[..., :MLA_NOPE].reshape(depth, MLA_RANK, MLA_HEADS * MLA_NOPE)
    wv = w[..., MLA_NOPE:].reshape(depth, MLA_RANK, MLA_HEADS * MLA_V)
    return wk.astype(BF16), jnp.swapaxes(wv, 1, 2).astype(BF16)


def _rope_table(seq):
    inv_freq = 1.0 / (ROPE_THETA ** (jnp.arange(0, MLA_ROPE, 2, dtype=F32) / MLA_ROPE))
    ang = jnp.arange(seq, dtype=F32)[:, None] * inv_freq[None, :]
    cos, sin = jnp.cos(ang), jnp.sin(ang)
    return jnp.concatenate([cos, cos, -sin, sin], axis=1)


def kernel(x, mem, ffn1_norm, ffn1_w_in, ffn1_w_out, mix_norm, mix_w_in, mix_b_gate, mla_q_norm, mla_kv_norm,
           mla_w_uq, mla_w_ukv, mla_w_proj, gla_w_a2, gla_b_a, gla_norm, gla_w_proj, conv_w, conv_w_proj,
           mix_w_out, xattn_norm, mem_norm, xattn_w_q, xattn_w_kv, xattn_w_o, ffn2_norm, ffn2_w_in,
           ffn2_w_out, final_norm):
    batch, seq, d = x.shape
    depth = ffn1_norm.shape[0]
    tokens = batch * seq
    table = _rope_table(seq)
    h = x.reshape(tokens, d)
    mem2 = mem.reshape(batch * mem.shape[1], d)
    bf = lambda w: w.astype(BF16)
    vec = lambda p: p.reshape(depth, 1, -1)

    w_mix_in = _pack_in_proj(mix_w_in)
    wq_t = _pack_uq(mla_w_uq)
    wk, wv_t = _pack_ukv(mla_w_ukv)
    w_a2 = bf(jnp.concatenate(
        [gla_w_a2, jnp.zeros((depth, LANES - GLA_RANK, GLA_HEADS * GLA_DK), F32)], axis=1))
    w_projs = (bf(mla_w_proj), bf(gla_w_proj), bf(conv_w_proj))
    w_mix_out, w_xq, w_xo = bf(mix_w_out), bf(xattn_w_q), bf(xattn_w_o)
    conv_taps = conv_w.reshape(depth, 3, CONV_DIM)

    t = TILES
    for l in range(depth):
        h = ffn(h, vec(ffn1_norm), ffn1_w_in, ffn1_w_out, final_norm, l, tm=t.ffn_rows, tf=t.ffn_cols,
                final_norm=False)

        z = norm_matmul(h, vec(mix_norm), w_mix_in, l, n=Z_WIDTH, tm=t.rows, tn=Z_WIDTH, out_dtype=F32)
        q_t, k, v_t = mla_prep(z, table, vec(mla_q_norm), vec(mla_kv_norm), wq_t, wk, wv_t, l,
                               batch=batch, seq=seq, tm=t.rows)
        a_mla = mla_attn(q_t, k, v_t, tq=t.rows, heads=t.attn_heads).reshape(tokens, MLA_HEADS * MLA_V)
        a_gla, a_conv = gla_and_conv(z, w_a2, vec(gla_b_a), vec(gla_norm), conv_taps, l,
                                     batch=batch, seq=seq, tc=t.rows)
        h = merge(h, vec(mix_norm), (a_mla, a_gla, a_conv), w_mix_in, GATE_COL0, vec(mix_b_gate), w_projs,
                  w_mix_out, l, tm=t.rows, tn=t.merge_cols)

        kv = norm_matmul(mem2, vec(mem_norm), xattn_w_kv, l, n=2 * d, tm=mem2.shape[0], tn=t.kv_cols,
                         out_dtype=BF16)
        h = xattn(h, vec(xattn_norm), w_xq, kv, w_xo, l, seq=seq, tm=t.rows)

        h = ffn(h, vec(ffn2_norm), ffn2_w_in, ffn2_w_out, final_norm, l, tm=t.ffn_rows, tf=t.ffn_cols,
                final_norm=(l == depth - 1))
    return h.reshape(batch, seq, d)
```

```python
import functools
from typing import NamedTuple

import jax
import jax.numpy as jnp
from jax import lax
from jax.experimental import pallas as pl
from jax.experimental.pallas import tpu as pltpu

F32 = jnp.float32
BF16 = jnp.bfloat16

EPS = 1e-6
D_MODEL = 2048
D_FF = 5632
MLA_HEADS = 8
MLA_RANK = 512
MLA_NOPE = 128
MLA_ROPE = 64
MLA_V = 128
MLA_QK_PAD = 256
ROPE_THETA = 10000.0
LOG2_E = 1.4426950408889634
GLA_HEADS = 4
GLA_DK = 64
GLA_DV = 128
GLA_RANK = 16
GLA_TAU = 16.0
GLA_CHUNK = 64
GLA_SUBCHUNK = 16
CONV_DIM = 512
X_HEADS = 4
X_HEAD_DIM = D_MODEL // X_HEADS
MEM_LEN = 256

LANES = 128
SUBLANES = 8
VMEM_LIMIT_BYTES = 56 * 1024 * 1024


class Tiles(NamedTuple):
    rows: int = 512
    ffn_rows: int = 1024
    ffn_cols: int = 256
    merge_cols: int = 512
    kv_cols: int = 1024
    attn_heads: int = 8


TILES = Tiles()

Z_CONV = 0
Z_CQ = 1536
Z_CKV = 2048
Z_GV = 2560
Z_GR = 3072
Z_GQ = 3584
Z_GK = 3840
Z_KR = 4096
Z_AL = 4224
Z_WIDTH = 4352


def _params(*sem):
    return pltpu.CompilerParams(dimension_semantics=sem, vmem_limit_bytes=VMEM_LIMIT_BYTES)


def _rms(x, g):
    return x * lax.rsqrt(jnp.mean(x * x, axis=-1, keepdims=True) + EPS) * g


def _dot(a, b):
    return jnp.dot(a, b, preferred_element_type=F32)


def _dot_t(a, b):
    return lax.dot_general(a, b, (((1,), (1,)), ((), ())), preferred_element_type=F32)


def _norm_matmul_kernel(x_ref, g_ref, w_ref, o_ref):
    xn = _rms(x_ref[...], g_ref[...]).astype(BF16)
    o_ref[...] = _dot(xn, w_ref[...].astype(BF16)).astype(o_ref.dtype)


def _layer(layer, block, index):
    return pl.BlockSpec((None,) + tuple(block), lambda *ids: (layer,) + tuple(index(*ids)))


def _vec(layer, width):
    return _layer(layer, (1, width), lambda *ids: (0, 0))


def norm_matmul(x, g, w, layer, *, n, tm, tn, out_dtype):
    m, k = x.shape
    assert m % tm == 0 and n % tn == 0 and n <= w.shape[2]
    w_mode = pl.Buffered(1) if n == tn else None
    return pl.pallas_call(
        _norm_matmul_kernel,
        grid=(n // tn, m // tm),
        in_specs=[
            pl.BlockSpec((tm, k), lambda j, i: (i, 0)),
            _vec(layer, k),
            pl.BlockSpec((None, k, tn), lambda j, i: (layer, 0, j), pipeline_mode=w_mode),
        ],
        out_specs=pl.BlockSpec((tm, tn), lambda j, i: (i, j)),
        out_shape=jax.ShapeDtypeStruct((m, n), out_dtype),
        compiler_params=_params("arbitrary", "arbitrary"),
        name="norm_matmul",
    )(x, g, w)


def _ffn_kernel(x_ref, g_ref, w_in_hbm, w_out_hbm, gf_ref, o_ref, xn_ref, wg_buf, wu_buf, wo_buf, sem,
                *, layer, tf, nf, final_norm):
    i = pl.program_id(0)
    f = nf * tf

    def chunk_copies(j, slot):
        c0 = pl.multiple_of(j * tf, tf)
        return (
            pltpu.make_async_copy(w_in_hbm.at[layer, :, pl.ds(c0, tf)], wg_buf.at[slot], sem.at[0, slot]),
            pltpu.make_async_copy(w_in_hbm.at[layer, :, pl.ds(f + c0, tf)], wu_buf.at[slot], sem.at[1, slot]),
            pltpu.make_async_copy(w_out_hbm.at[layer, pl.ds(c0, tf), :], wo_buf.at[slot], sem.at[2, slot]),
        )

    def start(j, slot):
        for copy in chunk_copies(j, slot):
            copy.start()

    def wait(j, slot):
        for copy in chunk_copies(j, slot):
            copy.wait()

    @pl.when(i == 0)
    def _():
        start(0, 0)

    x = x_ref[...]
    xn_ref[...] = _rms(x, g_ref[...]).astype(BF16)
    o_ref[...] = x

    def chunk(j, slot, start_next):
        wait(j, slot)
        start_next()
        xn = xn_ref[...]
        gate = _dot(xn, wg_buf[slot].astype(BF16))
        up = _dot(xn, wu_buf[slot].astype(BF16))
        act = (0.5 * gate) * jax.nn.sigmoid(gate) * up
        o_ref[...] += _dot(act.astype(BF16), wo_buf[slot].astype(BF16))

    def pair(jj, carry):
        j0 = 2 * jj
        chunk(j0, 0, lambda: start(j0 + 1, 1))

        def start_after_odd():
            @pl.when(j0 + 2 < nf)
            def _():
                start(j0 + 2, 0)

            @pl.when(jnp.logical_and(j0 + 2 == nf, i + 1 < pl.num_programs(0)))
            def _():
                start(0, 0)

        chunk(j0 + 1, 1, start_after_odd)
        return carry

    lax.fori_loop(0, nf // 2, pair, 0)

    if final_norm:
        o_ref[...] = _rms(o_ref[...], gf_ref[...])


def ffn(x, g, w_in, w_out, g_final, layer, *, tm, tf, final_norm):
    m, d = x.shape
    f = w_out.shape[1]
    assert m % tm == 0 and f % (2 * tf) == 0
    nf = f // tf
    return pl.pallas_call(
        functools.partial(_ffn_kernel, layer=layer, tf=tf, nf=nf, final_norm=final_norm),
        grid=(m // tm,),
        in_specs=[
            pl.BlockSpec((tm, d), lambda i: (i, 0)),
            _vec(layer, d),
            pl.BlockSpec(memory_space=pl.ANY),
            pl.BlockSpec(memory_space=pl.ANY),
            pl.BlockSpec((1, d), lambda i: (0, 0)),
        ],
        out_specs=pl.BlockSpec((tm, d), lambda i: (i, 0)),
        out_shape=jax.ShapeDtypeStruct((m, d), F32),
        scratch_shapes=[
            pltpu.VMEM((tm, d), BF16),
            pltpu.VMEM((2, d, tf), F32), pltpu.VMEM((2, d, tf), F32), pltpu.VMEM((2, tf, d), F32),
            pltpu.SemaphoreType.DMA((3, 2)),
        ],
        compiler_params=_params("arbitrary"),
        name="ffn",
    )(x, g, w_in, w_out, g_final.reshape(1, d))


def _rope_pairs(x, table):
    y = x * table
    return y + pltpu.roll(y, MLA_ROPE, 1)


def _mla_prep_kernel(cq_ref, ckv_ref, kr_ref, tab_ref, tab_t_ref, gq_ref, gkv_ref, wq_t_ref, wk_ref, wv_t_ref,
                     q_t_ref, k_ref, v_t_ref):
    scale = (MLA_NOPE + MLA_ROPE) ** -0.5 * LOG2_E
    nq_t = jnp.transpose(_rms(cq_ref[...], gq_ref[...])).astype(BF16)
    q_t = _dot(wq_t_ref[...], nq_t) * scale
    nkv = _rms(ckv_ref[...], gkv_ref[...])
    kn = _dot(nkv.astype(BF16), wk_ref[...])
    v_t = _dot(wv_t_ref[...], jnp.transpose(nkv).astype(BF16))
    tab = tab_ref[...]
    tab_t = tab_t_ref[...]
    lane = lax.broadcasted_iota(jnp.int32, tab.shape, 1)
    k_pe = jnp.where(lane < MLA_ROPE, _rope_pairs(kr_ref[...], tab), 0.0).astype(BF16)
    zeros = jnp.zeros((MLA_QK_PAD - MLA_NOPE - MLA_ROPE, q_t.shape[1]), BF16)
    for h in range(MLA_HEADS):
        q0 = h * MLA_QK_PAD
        q_t_ref[0, h, 0:MLA_NOPE, :] = q_t[q0:q0 + MLA_NOPE].astype(BF16)
        pe = q_t[q0 + MLA_NOPE:q0 + MLA_QK_PAD] * tab_t
        q_t_ref[0, h, MLA_NOPE:MLA_NOPE + MLA_ROPE, :] = (pe[0:MLA_ROPE] + pe[MLA_ROPE:2 * MLA_ROPE]).astype(BF16)
        q_t_ref[0, h, MLA_NOPE + MLA_ROPE:MLA_QK_PAD, :] = zeros
        k_ref[0, h, :, 0:LANES] = kn[:, h * LANES:(h + 1) * LANES].astype(BF16)
        k_ref[0, h, :, LANES:2 * LANES] = k_pe
        v_t_ref[0, h] = v_t[h * MLA_V:(h + 1) * MLA_V].astype(BF16)


def mla_prep(z, table, g_q, g_kv, wq_t, wk, wv_t, layer, *, batch, seq, tm):
    assert seq % tm == 0
    nb = seq // tm
    r = MLA_RANK
    hw = MLA_HEADS
    tok = lambda width, col: pl.BlockSpec((tm, width), lambda i: (i, col // width))
    full = lambda a: _layer(layer, a.shape[1:], lambda i: (0, 0))
    feat_major = lambda width: pl.BlockSpec((1, hw, width, tm), lambda i: (i // nb, 0, 0, i % nb))
    return pl.pallas_call(
        _mla_prep_kernel,
        grid=(batch * nb,),
        in_specs=[
            tok(r, Z_CQ), tok(r, Z_CKV), tok(LANES, Z_KR),
            pl.BlockSpec((tm, LANES), lambda i: (i % nb, 0)),
            pl.BlockSpec((LANES, tm), lambda i: (0, i % nb)),
            full(g_q), full(g_kv), full(wq_t), full(wk), full(wv_t),
        ],
        out_specs=[
            feat_major(MLA_QK_PAD),
            pl.BlockSpec((1, hw, tm, MLA_QK_PAD), lambda i: (i // nb, 0, i % nb, 0)),
            feat_major(MLA_V),
        ],
        out_shape=[
            jax.ShapeDtypeStruct((batch, hw, MLA_QK_PAD, seq), BF16),
            jax.ShapeDtypeStruct((batch, hw, seq, MLA_QK_PAD), BF16),
            jax.ShapeDtypeStruct((batch, hw, MLA_V, seq), BF16),
        ],
        compiler_params=_params("arbitrary"),
        name="mla_prep",
    )(z, z, z, table, jnp.transpose(table), g_q, g_kv, wq_t, wk, wv_t)


def _mla_attn_kernel(q_t_ref, k_ref, v_t_ref, o_ref, m_ref, l_ref, acc_ref, *, tq, heads):
    qi = pl.program_id(2)
    m_ref[...] = jnp.full(m_ref.shape, -jnp.inf, F32)
    l_ref[...] = jnp.zeros(l_ref.shape, F32)
    acc_ref[...] = jnp.zeros(acc_ref.shape, F32)

    def step(j, masked):
        start = pl.multiple_of(j * tq, tq)
        scores = [_dot(k_ref[0, g, pl.ds(start, tq), :], q_t_ref[0, g]) for g in range(heads)]
        for g in range(heads):
            s = scores[g]
            if masked:
                key = lax.broadcasted_iota(jnp.int32, s.shape, 0)
                qry = lax.broadcasted_iota(jnp.int32, s.shape, 1)
                s = jnp.where(key <= qry, s, -jnp.inf)
            m_old = m_ref[g]
            m_new = jnp.maximum(m_old, jnp.max(s, axis=0, keepdims=True))
            p = jnp.exp2(s - m_new)
            alpha = jnp.exp2(m_old - m_new)
            l_ref[g] = alpha * l_ref[g] + jnp.sum(p, axis=0, keepdims=True)
            acc_ref[g] = alpha * acc_ref[g] + _dot(v_t_ref[0, g, :, pl.ds(start, tq)], p.astype(BF16))
            m_ref[g] = m_new

    def body(j, carry):
        step(j, False)
        return carry

    lax.fori_loop(0, qi, body, 0)
    step(qi, True)
    for g in range(heads):
        o_t = acc_ref[g] / l_ref[g]
        o_ref[0, :, g * MLA_V:(g + 1) * MLA_V] = jnp.transpose(o_t).astype(o_ref.dtype)


def mla_attn(q_t, k, v_t, *, tq, heads):
    b, h, s, dq = k.shape
    dv = v_t.shape[2]
    assert s % tq == 0 and h % heads == 0
    return pl.pallas_call(
        functools.partial(_mla_attn_kernel, tq=tq, heads=heads),
        grid=(b, h // heads, s // tq),
        in_specs=[
            pl.BlockSpec((1, heads, dq, tq), lambda bi, hi, qi: (bi, hi, 0, qi)),
            pl.BlockSpec((1, heads, s, dq), lambda bi, hi, qi: (bi, hi, 0, 0), pipeline_mode=pl.Buffered(1)),
            pl.BlockSpec((1, heads, dv, s), lambda bi, hi, qi: (bi, hi, 0, 0), pipeline_mode=pl.Buffered(1)),
        ],
        out_specs=pl.BlockSpec((1, tq, heads * dv), lambda bi, hi, qi: (bi, qi, hi)),
        out_shape=jax.ShapeDtypeStruct((b, s, h * dv), BF16),
        scratch_shapes=[pltpu.VMEM((heads, 1, tq), F32), pltpu.VMEM((heads, 1, tq), F32),
                        pltpu.VMEM((heads, dv, tq), F32)],
        compiler_params=_params("arbitrary", "arbitrary", "arbitrary"),
        name="mla_attn",
    )(q_t, k, v_t)


def _gla_kernel(q_ref, k_ref, v_ref, r_ref, al_ref, cx_ref, halo_ref, wa_ref, ba_ref, ng_ref, taps_ref,
                o_ref, oc_ref, state_ref, b_scr, q_scr, k_scr, v_scr, oi_scr, *, tc):
    c = GLA_CHUNK
    pair_w = 2 * GLA_DK
    pair_v = 2 * GLA_DV
    n_pairs = GLA_HEADS // 2
    sequence_start = pl.program_id(1) == 0

    @pl.when(sequence_start)
    def _():
        state_ref[...] = jnp.zeros(state_ref.shape, F32)

    oc_ref[...] = _short_conv(cx_ref[...], halo_ref[...], taps_ref[...], sequence_start).astype(oc_ref.dtype)

    row_i = lax.broadcasted_iota(jnp.int32, (c, c), 0)
    col_i = lax.broadcasted_iota(jnp.int32, (c, c), 1)
    tri = jnp.where(row_i >= col_i, 1.0, 0.0).astype(BF16)
    lane_k = lax.broadcasted_iota(jnp.int32, (pair_w, LANES), 0)
    head_sum = [jnp.where((lane_k // GLA_DK) == hh, 1.0, 0.0).astype(BF16) for hh in range(2)]
    sub = GLA_SUBCHUNK
    key_row = lax.broadcasted_iota(jnp.int32, (c, pair_w), 0)
    lane_h = lax.broadcasted_iota(jnp.int32, (1, pair_w), 1) // GLA_DK
    head_lanes = [jnp.where(lane_h == hh, 1.0, 0.0) for hh in range(2)]
    srow = lax.broadcasted_iota(jnp.int32, (pair_w, pair_v), 0) // GLA_DK
    scol = lax.broadcasted_iota(jnp.int32, (pair_w, pair_v), 1) // GLA_DV
    own_head = srow == scol

    def chunk(ci, carry):
        r0 = pl.multiple_of(ci * c, c)
        rows = pl.ds(r0, c)
        x = _dot(al_ref[rows, :].astype(BF16), wa_ref[...]) + ba_ref[...]
        log_a = (jnp.minimum(x, 0.0) - jnp.log(1.0 + jnp.exp(-jnp.abs(x)))) * (1.0 / GLA_TAU)
        p1 = log_a.astype(BF16)
        r1 = log_a - p1.astype(F32)
        p2 = r1.astype(BF16)
        p3 = (r1 - p2.astype(F32)).astype(BF16)
        b_scr[...] = _dot(tri, p1) + _dot(tri, p2) + _dot(tri, p3)
        q_scr[...] = q_ref[rows, :] * (GLA_DK ** -0.5)
        k_scr[...] = k_ref[rows, :]
        v_scr[...] = v_ref[rows, :]

        staged = []
        for p in range(n_pairs):
            kl = slice(p * pair_w, (p + 1) * pair_w)
            vl = slice(p * pair_v, (p + 1) * pair_v)
            bp = b_scr[:, kl]
            qp = q_scr[:, kl]
            kp = k_scr[:, kl]
            vp = v_scr[:, vl]
            b_last = b_scr[c - 1:c, kl]
            st = state_ref[p]
            q_dec = (qp * jnp.exp(bp)).astype(BF16)
            k_dec = (kp * jnp.exp(b_last - bp)).astype(BF16)
            o_inter = _dot(q_dec, st.astype(BF16))
            vp_bf = vp.astype(BF16)
            kv = lax.dot_general(k_dec, vp_bf, (((0,), (0,)), ((), ())), preferred_element_type=F32)
            decay_col = jnp.transpose(jnp.broadcast_to(jnp.exp(b_last), (LANES, pair_w)))
            decay_col = jnp.concatenate([decay_col, decay_col], axis=1)
            state_ref[p] = decay_col * st + jnp.where(own_head, kv, 0.0)
            q_blocks, k_blocks = [], []
            for i in range(1, c // sub):
                r = i * sub
                b_r = b_scr[r:r + 1, kl]
                q_i = q_scr[r:r + sub, kl] * jnp.exp(b_scr[r:r + sub, kl] - b_r)
                pieces = [jnp.zeros((r, pair_w), F32), q_i]
                if c - r - sub:
                    pieces.append(jnp.zeros((c - r - sub, pair_w), F32))
                q_blocks.append(jnp.concatenate(pieces, axis=0))
                k_blocks.append(kp * jnp.exp(jnp.where(key_row < r, b_r - bp, -jnp.inf)))
            q_cat = jnp.concatenate(q_blocks, axis=1)
            k_cat = jnp.concatenate(k_blocks, axis=1).astype(BF16)
            a_far = [_dot_t((q_cat * jnp.concatenate([head_lanes[hh]] * len(q_blocks), axis=1)).astype(BF16), k_cat)
                     for hh in range(2)]
            ys = []
            for i in range(c // sub):
                r = i * sub
                for t0 in (r, r + 8):
                    n = r + sub - t0
                    bt = b_scr[t0:r + sub, kl]
                    qt = q_scr[t0:r + sub, kl]
                    t_idx = t0 + lax.broadcasted_iota(jnp.int32, (n, pair_w), 0)
                    for s in range(t0, t0 + 8):
                        e = jnp.exp(jnp.where(t_idx >= s, bt - b_scr[s:s + 1, kl], -jnp.inf))
                        ys.append(qt * e * k_scr[s:s + 1, kl])
            y = jnp.concatenate(ys, axis=0).astype(BF16)
            a_near = [_dot(y, head_sum[hh]) for hh in range(2)]
            staged.append((o_inter, vp_bf, a_far, a_near))

        for p in range(n_pairs):
            o_inter, vp_bf, a_far, a_near = staged[p]
            for hh in range(2):
                hv = slice(p * pair_v + hh * GLA_DV, p * pair_v + (hh + 1) * GLA_DV)
                o_h = o_inter[:, hh * GLA_DV:(hh + 1) * GLA_DV] + _dot(a_far[hh].astype(BF16),
                                                                       vp_bf[:, hh * GLA_DV:(hh + 1) * GLA_DV])
                a = a_near[hh]
                for i in range(c // sub):
                    r = i * sub
                    base = i * (8 * sub + 8 * 8)
                    upd = a[base:base + sub] * v_scr[r:r + 1, hv]
                    for idx in range(1, 8):
                        upd += a[base + idx * sub:base + (idx + 1) * sub] * v_scr[r + idx:r + idx + 1, hv]
                    base += 8 * sub
                    upd8 = a[base:base + 8] * v_scr[r + 8:r + 9, hv]
                    for idx in range(1, 8):
                        upd8 += a[base + idx * 8:base + (idx + 1) * 8] * v_scr[r + 8 + idx:r + 9 + idx, hv]
                    upd = jnp.concatenate([upd[0:8], upd[8:sub] + upd8], axis=0)
                    oi_scr[r:r + sub, hv] = upd + o_h[r:r + sub]

        o = oi_scr[...]
        gate = r_ref[rows, :]
        gate = gate * jax.nn.sigmoid(gate)
        for h in range(GLA_HEADS):
            hv = slice(h * GLA_DV, (h + 1) * GLA_DV)
            oh = o[:, hv]
            oh = oh * lax.rsqrt(jnp.mean(oh * oh, axis=-1, keepdims=True) + EPS)
            o_ref[rows, hv] = (oh * ng_ref[:, hv] * gate[:, hv]).astype(o_ref.dtype)
        return carry

    lax.fori_loop(0, tc // c, chunk, 0, unroll=2)


def gla_and_conv(z, w_a2, b_a, norm_g, conv_taps, layer, *, batch, seq, tc):
    assert seq % tc == 0 and tc % GLA_CHUNK == 0
    nb = seq // tc
    kw = GLA_HEADS * GLA_DK
    vw = GLA_HEADS * GLA_DV
    cw = 3 * CONV_DIM
    tok = lambda width, col: pl.BlockSpec((tc, width), lambda b, i: (b * nb + i, col // width))
    full = lambda a: _layer(layer, a.shape[1:], lambda b, i: (0, 0))
    c = GLA_CHUNK
    return pl.pallas_call(
        functools.partial(_gla_kernel, tc=tc),
        grid=(batch, nb),
        in_specs=[
            tok(kw, Z_GQ), tok(kw, Z_GK), tok(vw, Z_GV), tok(vw, Z_GR), tok(LANES, Z_AL), tok(cw, Z_CONV),
            pl.BlockSpec((SUBLANES, cw),
                         lambda b, i: (jnp.maximum((b * nb + i) * (tc // SUBLANES) - 1, 0), Z_CONV // cw)),
            full(w_a2), full(b_a), full(norm_g), full(conv_taps),
        ],
        out_specs=[pl.BlockSpec((tc, vw), lambda b, i: (b * nb + i, 0)),
                   pl.BlockSpec((tc, CONV_DIM), lambda b, i: (b * nb + i, 0))],
        out_shape=[jax.ShapeDtypeStruct((batch * seq, vw), BF16),
                   jax.ShapeDtypeStruct((batch * seq, CONV_DIM), BF16)],
        scratch_shapes=[
            pltpu.VMEM((GLA_HEADS // 2, 2 * GLA_DK, 2 * GLA_DV), F32),
            pltpu.VMEM((c, kw), F32), pltpu.VMEM((c, kw), F32), pltpu.VMEM((c, kw), F32),
            pltpu.VMEM((c, vw), F32), pltpu.VMEM((c, vw), F32),
        ],
        compiler_params=_params("arbitrary", "arbitrary"),
        name="gla",
    )(z, z, z, z, z, z, z, w_a2, b_a, norm_g, conv_taps)


def _short_conv(x, halo, w, sequence_start):
    d = CONV_DIM
    z = x[:, d:2 * d] * x[:, 2 * d:3 * d]
    hz = halo[:, d:2 * d] * halo[:, 2 * d:3 * d]
    hz = jnp.where(sequence_start, 0.0, hz)
    row = lax.broadcasted_iota(jnp.int32, z.shape, 0)
    z1 = jnp.where(row == 0, hz[7:8], pltpu.roll(z, 1, 0))
    z2 = jnp.where(row == 0, hz[6:7], jnp.where(row == 1, hz[7:8], pltpu.roll(z, 2, 0)))
    y = w[2:3, :] * z + w[1:2, :] * z1 + w[0:1, :] * z2
    return x[:, 0:d] * y


def _merge_kernel(h_ref, g_ref, a0_ref, a1_ref, a2_ref, wg0_ref, wg1_ref, wg2_ref, b0_ref, b1_ref, b2_ref,
                  wp0_ref, wp1_ref, wp2_ref, wo_ref, o_ref, u_ref):
    j = pl.program_id(1)

    @pl.when(j == 0)
    def _():
        h = h_ref[...]
        u_ref[...] = _rms(h, g_ref[...]).astype(BF16)
        o_ref[...] = h

    u = u_ref[...]
    merged = None
    for a_ref, wg_ref, b_ref, wp_ref in ((a0_ref, wg0_ref, b0_ref, wp0_ref),
                                         (a1_ref, wg1_ref, b1_ref, wp1_ref),
                                         (a2_ref, wg2_ref, b2_ref, wp2_ref)):
        gate = jax.nn.sigmoid(_dot(u, wg_ref[...]) + b_ref[...])
        term = gate * _dot(a_ref[...], wp_ref[...])
        merged = term if merged is None else merged + term
    o_ref[...] += _dot(merged.astype(BF16), wo_ref[...])


def merge(h, g, branches, w_gate, gate_col0, b_gate, w_projs, w_out, layer, *, tm, tn):
    m, d = h.shape
    assert m % tm == 0 and d % tn == 0 and gate_col0 % tn == 0
    nj = d // tn
    g0 = gate_col0 // tn
    row = lambda a: pl.BlockSpec((tm, a.shape[1]), lambda i, j: (i, 0))
    gate_w = lambda b: _layer(layer, (d, tn), lambda i, j: (0, g0 + b * nj + j))
    gate_b = lambda b: _layer(layer, (1, tn), lambda i, j: (0, b * nj + j))
    proj_w = lambda w: _layer(layer, (w.shape[1], tn), lambda i, j: (0, j))
    return pl.pallas_call(
        _merge_kernel,
        grid=(m // tm, nj),
        in_specs=[
            row(h), _vec(layer, d),
            row(branches[0]), row(branches[1]), row(branches[2]),
            gate_w(0), gate_w(1), gate_w(2), gate_b(0), gate_b(1), gate_b(2),
            proj_w(w_projs[0]), proj_w(w_projs[1]), proj_w(w_projs[2]),
            _layer(layer, (tn, d), lambda i, j: (j, 0)),
        ],
        out_specs=pl.BlockSpec((tm, d), lambda i, j: (i, 0)),
        out_shape=jax.ShapeDtypeStruct((m, d), F32),
        scratch_shapes=[pltpu.VMEM((tm, d), BF16)],
        compiler_params=_params("arbitrary", "arbitrary"),
        name="merge",
    )(h, g, *branches, w_gate, w_gate, w_gate, b_gate, b_gate, b_gate, *w_projs, w_out)


def _xattn_kernel(h_ref, g_ref, wq_ref, k_ref, v_ref, wo_ref, o_ref):
    h = h_ref[...]
    hn = _rms(h, g_ref[...]).astype(BF16)
    q = (_dot(hn, wq_ref[...]) * (X_HEAD_DIM ** -0.5)).astype(BF16)
    heads = [slice(j * X_HEAD_DIM, (j + 1) * X_HEAD_DIM) for j in range(X_HEADS)]
    scores = [_dot_t(q[:, hs], k_ref[:, hs]) for hs in heads]
    outs = []
    for hs, s in zip(heads, scores):
        p = jnp.exp(s - jnp.max(s, axis=-1, keepdims=True))
        o = _dot(p.astype(BF16), v_ref[:, hs]) / jnp.sum(p, axis=-1, keepdims=True)
        outs.append(o.astype(BF16))
    o_ref[...] = h + _dot(jnp.concatenate(outs, axis=1), wo_ref[...])


def xattn(h, g, w_q, kv, w_o, layer, *, seq, tm):
    m, d = h.shape
    assert seq % tm == 0
    nb = seq // tm
    once = pl.Buffered(1)
    return pl.pallas_call(
        _xattn_kernel,
        grid=(m // tm,),
        in_specs=[
            pl.BlockSpec((tm, d), lambda i: (i, 0)),
            _vec(layer, d),
            pl.BlockSpec((None, d, d), lambda i: (layer, 0, 0), pipeline_mode=once),
            pl.BlockSpec((MEM_LEN, d), lambda i: (i // nb, 0)),
            pl.BlockSpec((MEM_LEN, d), lambda i: (i // nb, 1)),
            pl.BlockSpec((None, d, d), lambda i: (layer, 0, 0), pipeline_mode=once),
        ],
        out_specs=pl.BlockSpec((tm, d), lambda i: (i, 0)),
        out_shape=jax.ShapeDtypeStruct((m, d), F32),
        compiler_params=_params("arbitrary"),
        name="xattn",
    )(h, g, w_q, kv, kv, w_o)


PACK_BLOCK = 512
_PACK_SRC = (2640, 3152, 3664, 0, 512, 1600, 2112, 1088, 1024)
_PACK_SPECIAL = len(_PACK_SRC) - 1
_SRC_A_LOW = 2624
_SRC_GATE = 4176
GATE_COL0 = len(_PACK_SRC) * PACK_BLOCK


def _pack_in_proj_kernel(src_ref, w_t_ref, a_low_ref, o_ref):
    del src_ref
    j = pl.program_id(1)

    @pl.when(j != _PACK_SPECIAL)
    def _():
        o_ref[...] = jnp.transpose(w_t_ref[0]).astype(BF16)

    @pl.when(j == _PACK_SPECIAL)
    def _():
        half = MLA_ROPE // 2
        k_rope = w_t_ref[0, 0:MLA_ROPE, :]
        row = lax.broadcasted_iota(jnp.int32, a_low_ref.shape[1:], 0)
        a_low = jnp.where(row < GLA_RANK, a_low_ref[0], 0.0)
        rest = jnp.zeros((PACK_BLOCK - 2 * LANES, k_rope.shape[1]), F32)
        blk = jnp.concatenate([k_rope, k_rope[half:], k_rope[:half], a_low, rest], axis=0)
        o_ref[...] = jnp.transpose(blk).astype(BF16)


def _pack_in_proj(w):
    depth, d, n = w.shape
    w_t = jnp.swapaxes(w, 1, 2)
    n_gate = n - _SRC_GATE
    assert n_gate % PACK_BLOCK == 0
    rows = _PACK_SRC + tuple(_SRC_GATE + PACK_BLOCK * k for k in range(n_gate // PACK_BLOCK))
    assert all(r % SUBLANES == 0 for r in rows) and _SRC_A_LOW % SUBLANES == 0
    src = jnp.asarray([r // SUBLANES for r in rows], jnp.int32)
    n_blocks = src.shape[0]
    grid_spec = pltpu.PrefetchScalarGridSpec(
        num_scalar_prefetch=1,
        grid=(depth, n_blocks),
        in_specs=[
            pl.BlockSpec((pl.Element(1), pl.Element(PACK_BLOCK), pl.Element(d)),
                         lambda l, j, src: (l, src[j] * SUBLANES, 0)),
            pl.BlockSpec((pl.Element(1), pl.Element(LANES), pl.Element(d)), lambda l, j, src: (l, _SRC_A_LOW, 0)),
        ],
        out_specs=pl.BlockSpec((None, d, PACK_BLOCK), lambda l, j, src: (l, 0, j)),
    )
    return pl.pallas_call(
        _pack_in_proj_kernel,
        grid_spec=grid_spec,
        out_shape=jax.ShapeDtypeStruct((depth, d, n_blocks * PACK_BLOCK), BF16),
        compiler_params=_params("arbitrary", "arbitrary"),
        name="pack_in_proj",
    )(src, w_t, w_t)


def _pack_uq(w):
    depth = w.shape[0]
    w = w.reshape(depth, MLA_RANK, MLA_HEADS, MLA_NOPE + MLA_ROPE)
    half = MLA_ROPE // 2
    nope, pe = w[..., :MLA_NOPE], w[..., MLA_NOPE:]
    swap = jnp.concatenate([pe[..., half:], pe[..., :half]], axis=-1)
    packed = jnp.concatenate([nope, pe, swap], axis=-1).reshape(depth, MLA_RANK, MLA_HEADS * MLA_QK_PAD)
    return jnp.swapaxes(packed, 1, 2).astype(BF16)


def _pack_ukv(w):
    depth = w.shape[0]
    w = w.reshape(depth, MLA_RANK, MLA_HEADS, MLA_NOPE + MLA_V)
    wk = w[..., :MLA_NOPE].reshape(depth, MLA_RANK, MLA_HEADS * MLA_NOPE)
    wv = w[..., MLA_NOPE:].reshape(depth, MLA_RANK, MLA_HEADS * MLA_V)
    return wk.astype(BF16), jnp.swapaxes(wv, 1, 2).astype(BF16)


def _rope_table(seq):
    inv_freq = 1.0 / (ROPE_THETA ** (jnp.arange(0, MLA_ROPE, 2, dtype=F32) / MLA_ROPE))
    ang = jnp.arange(seq, dtype=F32)[:, None] * inv_freq[None, :]
    cos, sin = jnp.cos(ang), jnp.sin(ang)
    return jnp.concatenate([cos, cos, -sin, sin], axis=1)


def kernel(x, mem, ffn1_norm, ffn1_w_in, ffn1_w_out, mix_norm, mix_w_in, mix_b_gate, mla_q_norm, mla_kv_norm,
           mla_w_uq, mla_w_ukv, mla_w_proj, gla_w_a2, gla_b_a, gla_norm, gla_w_proj, conv_w, conv_w_proj,
           mix_w_out, xattn_norm, mem_norm, xattn_w_q, xattn_w_kv, xattn_w_o, ffn2_norm, ffn2_w_in,
           ffn2_w_out, final_norm):
    batch, seq, d = x.shape
    depth = ffn1_norm.shape[0]
    tokens = batch * seq
    table = _rope_table(seq)
    h = x.reshape(tokens, d)
    mem2 = mem.reshape(batch * mem.shape[1], d)
    bf = lambda w: w.astype(BF16)
    vec = lambda p: p.reshape(depth, 1, -1)

    w_mix_in = _pack_in_proj(mix_w_in)
    wq_t = _pack_uq(mla_w_uq)
    wk, wv_t = _pack_ukv(mla_w_ukv)
    w_a2 = bf(jnp.concatenate(
        [gla_w_a2, jnp.zeros((depth, LANES - GLA_RANK, GLA_HEADS * GLA_DK), F32)], axis=1))
    w_projs = (bf(mla_w_proj), bf(gla_w_proj), bf(conv_w_proj))
    w_mix_out, w_xq, w_xo = bf(mix_w_out), bf(xattn_w_q), bf(xattn_w_o)
    conv_taps = conv_w.reshape(depth, 3, CONV_DIM)

    t = TILES
    for l in range(depth):
        h = ffn(h, vec(ffn1_norm), ffn1_w_in, ffn1_w_out, final_norm, l, tm=t.ffn_rows, tf=t.ffn_cols,
                final_norm=False)

        z = norm_matmul(h, vec(mix_norm), w_mix_in, l, n=Z_WIDTH, tm=t.rows, tn=Z_WIDTH, out_dtype=F32)
        q_t, k, v_t = mla_prep(z, table, vec(mla_q_norm), vec(mla_kv_norm), wq_t, wk, wv_t, l,
                               batch=batch, seq=seq, tm=t.rows)
        a_mla = mla_attn(q_t, k, v_t, tq=t.rows, heads=t.attn_heads).reshape(tokens, MLA_HEADS * MLA_V)
        a_gla, a_conv = gla_and_conv(z, w_a2, vec(gla_b_a), vec(gla_norm), conv_taps, l,
                                     batch=batch, seq=seq, tc=t.rows)
        h = merge(h, vec(mix_norm), (a_mla, a_gla, a_conv), w_mix_in, GATE_COL0, vec(mix_b_gate), w_projs,
                  w_mix_out, l, tm=t.rows, tn=t.merge_cols)

        kv = norm_matmul(mem2, vec(mem_norm), xattn_w_kv, l, n=2 * d, tm=mem2.shape[0], tn=t.kv_cols,
                         out_dtype=BF16)
        h = xattn(h, vec(xattn_norm), w_xq, kv, w_xo, l, seq=seq, tm=t.rows)

        h = ffn(h, vec(ffn2_norm), ffn2_w_in, ffn2_w_out, final_norm, l, tm=t.ffn_rows, tf=t.ffn_cols,
                final_norm=(l == depth - 1))
    return h.reshape(batch, seq, d)
```

```python
import functools
from typing import NamedTuple

import jax
import jax.numpy as jnp
from jax import lax
from jax.experimental import pallas as pl
from jax.experimental.pallas import tpu as pltpu

F32 = jnp.float32
BF16 = jnp.bfloat16

EPS = 1e-6
D_MODEL = 2048
D_FF = 5632
MLA_HEADS = 8
MLA_RANK = 512
MLA_NOPE = 128
MLA_ROPE = 64
MLA_V = 128
MLA_QK_PAD = 256
ROPE_THETA = 10000.0
LOG2_E = 1.4426950408889634
GLA_HEADS = 4
GLA_DK = 64
GLA_DV = 128
GLA_RANK = 16
GLA_TAU = 16.0
GLA_CHUNK = 64
GLA_SUBCHUNK = 16
CONV_DIM = 512
X_HEADS = 4
X_HEAD_DIM = D_MODEL // X_HEADS
MEM_LEN = 256

LANES = 128
SUBLANES = 8
VMEM_LIMIT_BYTES = 56 * 1024 * 1024


class Tiles(NamedTuple):
    rows: int = 512
    ffn_rows: int = 1024
    ffn_cols: int = 256
    merge_cols: int = 512
    kv_cols: int = 1024
    attn_heads: int = 8


TILES = Tiles()

Z_CONV = 0
Z_CQ = 1536
Z_CKV = 2048
Z_GV = 2560
Z_GR = 3072
Z_GQ = 3584
Z_GK = 3840
Z_KR = 4096
Z_AL = 4224
Z_WIDTH = 4352


def _params(*sem):
    return pltpu.CompilerParams(dimension_semantics=sem, vmem_limit_bytes=VMEM_LIMIT_BYTES)


def _rms(x, g):
    return x * lax.rsqrt(jnp.mean(x * x, axis=-1, keepdims=True) + EPS) * g


def _dot(a, b):
    return jnp.dot(a, b, preferred_element_type=F32)


def _dot_t(a, b):
    return lax.dot_general(a, b, (((1,), (1,)), ((), ())), preferred_element_type=F32)


def _norm_matmul_kernel(x_ref, g_ref, w_ref, o_ref):
    xn = _rms(x_ref[...], g_ref[...]).astype(BF16)
    o_ref[...] = _dot(xn, w_ref[...].astype(BF16)).astype(o_ref.dtype)


def _layer(layer, block, index):
    return pl.BlockSpec((None,) + tuple(block), lambda *ids: (layer,) + tuple(index(*ids)))


def _vec(layer, width):
    return _layer(layer, (1, width), lambda *ids: (0, 0))


def norm_matmul(x, g, w, layer, *, n, tm, tn, out_dtype):
    m, k = x.shape
    assert m % tm == 0 and n % tn == 0 and n <= w.shape[2]
    w_mode = pl.Buffered(1) if n == tn else None
    return pl.pallas_call(
        _norm_matmul_kernel,
        grid=(n // tn, m // tm),
        in_specs=[
            pl.BlockSpec((tm, k), lambda j, i: (i, 0)),
            _vec(layer, k),
            pl.BlockSpec((None, k, tn), lambda j, i: (layer, 0, j), pipeline_mode=w_mode),
        ],
        out_specs=pl.BlockSpec((tm, tn), lambda j, i: (i, j)),
        out_shape=jax.ShapeDtypeStruct((m, n), out_dtype),
        compiler_params=_params("arbitrary", "arbitrary"),
        name="norm_matmul",
    )(x, g, w)


def _ffn_kernel(x_ref, g_ref, w_in_hbm, w_out_hbm, gf_ref, o_ref, xn_ref, wg_buf, wu_buf, wo_buf, sem,
                *, layer, tf, nf, final_norm):
    i = pl.program_id(0)
    f = nf * tf

    def chunk_copies(j, slot):
        c0 = pl.multiple_of(j * tf, tf)
        return (
            pltpu.make_async_copy(w_in_hbm.at[layer, :, pl.ds(c0, tf)], wg_buf.at[slot], sem.at[0, slot]),
            pltpu.make_async_copy(w_in_hbm.at[layer, :, pl.ds(f + c0, tf)], wu_buf.at[slot], sem.at[1, slot]),
            pltpu.make_async_copy(w_out_hbm.at[layer, pl.ds(c0, tf), :], wo_buf.at[slot], sem.at[2, slot]),
        )

    def start(j, slot):
        for copy in chunk_copies(j, slot):
            copy.start()

    def wait(j, slot):
        for copy in chunk_copies(j, slot):
            copy.wait()

    @pl.when(i == 0)
    def _():
        start(0, 0)

    x = x_ref[...]
    xn_ref[...] = _rms(x, g_ref[...]).astype(BF16)
    o_ref[...] = x

    def chunk(j, slot, start_next):
        wait(j, slot)
        start_next()
        xn = xn_ref[...]
        gate = _dot(xn, wg_buf[slot].astype(BF16))
        up = _dot(xn, wu_buf[slot].astype(BF16))
        act = (0.5 * gate) * jax.nn.sigmoid(gate) * up
        o_ref[...] += _dot(act.astype(BF16), wo_buf[slot].astype(BF16))

    def pair(jj, carry):
        j0 = 2 * jj
        chunk(j0, 0, lambda: start(j0 + 1, 1))

        def start_after_odd():
            @pl.when(j0 + 2 < nf)
            def _():
                start(j0 + 2, 0)

            @pl.when(jnp.logical_and(j0 + 2 == nf, i + 1 < pl.num_programs(0)))
            def _():
                start(0, 0)

        chunk(j0 + 1, 1, start_after_odd)
        return carry

    lax.fori_loop(0, nf // 2, pair, 0)

    if final_norm:
        o_ref[...] = _rms(o_ref[...], gf_ref[...])


def ffn(x, g, w_in, w_out, g_final, layer, *, tm, tf, final_norm):
    m, d = x.shape
    f = w_out.shape[1]
    assert m % tm == 0 and f % (2 * tf) == 0
    nf = f // tf
    return pl.pallas_call(
        functools.partial(_ffn_kernel, layer=layer, tf=tf, nf=nf, final_norm=final_norm),
        grid=(m // tm,),
        in_specs=[
            pl.BlockSpec((tm, d), lambda i: (i, 0)),
            _vec(layer, d),
            pl.BlockSpec(memory_space=pl.ANY),
            pl.BlockSpec(memory_space=pl.ANY),
            pl.BlockSpec((1, d), lambda i: (0, 0)),
        ],
        out_specs=pl.BlockSpec((tm, d), lambda i: (i, 0)),
        out_shape=jax.ShapeDtypeStruct((m, d), F32),
        scratch_shapes=[
            pltpu.VMEM((tm, d), BF16),
            pltpu.VMEM((2, d, tf), F32), pltpu.VMEM((2, d, tf), F32), pltpu.VMEM((2, tf, d), F32),
            pltpu.SemaphoreType.DMA((3, 2)),
        ],
        compiler_params=_params("arbitrary"),
        name="ffn",
    )(x, g, w_in, w_out, g_final.reshape(1, d))


def _rope_pairs(x, table):
    y = x * table
    return y + pltpu.roll(y, MLA_ROPE, 1)


def _mla_prep_kernel(cq_ref, ckv_ref, kr_ref, tab_ref, tab_t_ref, gq_ref, gkv_ref, wq_t_ref, wk_ref, wv_t_ref,
                     q_t_ref, k_ref, v_t_ref):
    scale = (MLA_NOPE + MLA_ROPE) ** -0.5 * LOG2_E
    nq_t = jnp.transpose(_rms(cq_ref[...], gq_ref[...])).astype(BF16)
    q_t = _dot(wq_t_ref[...], nq_t) * scale
    nkv = _rms(ckv_ref[...], gkv_ref[...])
    kn = _dot(nkv.astype(BF16), wk_ref[...])
    v_t = _dot(wv_t_ref[...], jnp.transpose(nkv).astype(BF16))
    tab = tab_ref[...]
    tab_t = tab_t_ref[...]
    lane = lax.broadcasted_iota(jnp.int32, tab.shape, 1)
    k_pe = jnp.where(lane < MLA_ROPE, _rope_pairs(kr_ref[...], tab), 0.0).astype(BF16)
    zeros = jnp.zeros((MLA_QK_PAD - MLA_NOPE - MLA_ROPE, q_t.shape[1]), BF16)
    for h in range(MLA_HEADS):
        q0 = h * MLA_QK_PAD
        q_t_ref[0, h, 0:MLA_NOPE, :] = q_t[q0:q0 + MLA_NOPE].astype(BF16)
        pe = q_t[q0 + MLA_NOPE:q0 + MLA_QK_PAD] * tab_t
        q_t_ref[0, h, MLA_NOPE:MLA_NOPE + MLA_ROPE, :] = (pe[0:MLA_ROPE] + pe[MLA_ROPE:2 * MLA_ROPE]).astype(BF16)
        q_t_ref[0, h, MLA_NOPE + MLA_ROPE:MLA_QK_PAD, :] = zeros
        k_ref[0, h, :, 0:LANES] = kn[:, h * LANES:(h + 1) * LANES].astype(BF16)
        k_ref[0, h, :, LANES:2 * LANES] = k_pe
        v_t_ref[0, h] = v_t[h * MLA_V:(h + 1) * MLA_V].astype(BF16)


def mla_prep(z, table, g_q, g_kv, wq_t, wk, wv_t, layer, *, batch, seq, tm):
    assert seq % tm == 0
    nb = seq // tm
    r = MLA_RANK
    hw = MLA_HEADS
    tok = lambda width, col: pl.BlockSpec((tm, width), lambda i: (i, col // width))
    full = lambda a: _layer(layer, a.shape[1:], lambda i: (0, 0))
    feat_major = lambda width: pl.BlockSpec((1, hw, width, tm), lambda i: (i // nb, 0, 0, i % nb))
    return pl.pallas_call(
        _mla_prep_kernel,
        grid=(batch * nb,),
        in_specs=[
            tok(r, Z_CQ), tok(r, Z_CKV), tok(LANES, Z_KR),
            pl.BlockSpec((tm, LANES), lambda i: (i % nb, 0)),
            pl.BlockSpec((LANES, tm), lambda i: (0, i % nb)),
            full(g_q), full(g_kv), full(wq_t), full(wk), full(wv_t),
        ],
        out_specs=[
            feat_major(MLA_QK_PAD),
            pl.BlockSpec((1, hw, tm, MLA_QK_PAD), lambda i: (i // nb, 0, i % nb, 0)),
            feat_major(MLA_V),
        ],
        out_shape=[
            jax.ShapeDtypeStruct((batch, hw, MLA_QK_PAD, seq), BF16),
            jax.ShapeDtypeStruct((batch, hw, seq, MLA_QK_PAD), BF16),
            jax.ShapeDtypeStruct((batch, hw, MLA_V, seq), BF16),
        ],
        compiler_params=_params("arbitrary"),
        name="mla_prep",
    )(z, z, z, table, jnp.transpose(table), g_q, g_kv, wq_t, wk, wv_t)


def _mla_attn_kernel(q_t_ref, k_ref, v_t_ref, o_ref, m_ref, l_ref, acc_ref, *, tq, heads):
    qi = pl.program_id(2)
    m_ref[...] = jnp.full(m_ref.shape, -jnp.inf, F32)
    l_ref[...] = jnp.zeros(l_ref.shape, F32)
    acc_ref[...] = jnp.zeros(acc_ref.shape, F32)

    def step(j, masked):
        start = pl.multiple_of(j * tq, tq)
        scores = [_dot(k_ref[0, g, pl.ds(start, tq), :], q_t_ref[0, g]) for g in range(heads)]
        for g in range(heads):
            s = scores[g]
            if masked:
                key = lax.broadcasted_iota(jnp.int32, s.shape, 0)
                qry = lax.broadcasted_iota(jnp.int32, s.shape, 1)
                s = jnp.where(key <= qry, s, -jnp.inf)
            m_old = m_ref[g]
            m_new = jnp.maximum(m_old, jnp.max(s, axis=0, keepdims=True))
            p = jnp.exp2(s - m_new)
            alpha = jnp.exp2(m_old - m_new)
            l_ref[g] = alpha * l_ref[g] + jnp.sum(p, axis=0, keepdims=True)
            acc_ref[g] = alpha * acc_ref[g] + _dot(v_t_ref[0, g, :, pl.ds(start, tq)], p.astype(BF16))
            m_ref[g] = m_new

    def body(j, carry):
        step(j, False)
        return carry

    lax.fori_loop(0, qi, body, 0)
    step(qi, True)
    for g in range(heads):
        o_t = acc_ref[g] / l_ref[g]
        o_ref[0, :, g * MLA_V:(g + 1) * MLA_V] = jnp.transpose(o_t).astype(o_ref.dtype)


def mla_attn(q_t, k, v_t, *, tq, heads):
    b, h, s, dq = k.shape
    dv = v_t.shape[2]
    assert s % tq == 0 and h % heads == 0
    return pl.pallas_call(
        functools.partial(_mla_attn_kernel, tq=tq, heads=heads),
        grid=(b, h // heads, s // tq),
        in_specs=[
            pl.BlockSpec((1, heads, dq, tq), lambda bi, hi, qi: (bi, hi, 0, qi)),
            pl.BlockSpec((1, heads, s, dq), lambda bi, hi, qi: (bi, hi, 0, 0), pipeline_mode=pl.Buffered(1)),
            pl.BlockSpec((1, heads, dv, s), lambda bi, hi, qi: (bi, hi, 0, 0), pipeline_mode=pl.Buffered(1)),
        ],
        out_specs=pl.BlockSpec((1, tq, heads * dv), lambda bi, hi, qi: (bi, qi, hi)),
        out_shape=jax.ShapeDtypeStruct((b, s, h * dv), BF16),
        scratch_shapes=[pltpu.VMEM((heads, 1, tq), F32), pltpu.VMEM((heads, 1, tq), F32),
                        pltpu.VMEM((heads, dv, tq), F32)],
        compiler_params=_params("arbitrary", "arbitrary", "arbitrary"),
        name="mla_attn",
    )(q_t, k, v_t)


def _gla_kernel(q_ref, k_ref, v_ref, r_ref, al_ref, cx_ref, halo_ref, wa_ref, ba_ref, ng_ref, taps_ref,
                o_ref, oc_ref, state_ref, b_scr, q_scr, k_scr, v_scr, oi_scr, *, tc):
    c = GLA_CHUNK
    pair_w = 2 * GLA_DK
    pair_v = 2 * GLA_DV
    n_pairs = GLA_HEADS // 2
    sequence_start = pl.program_id(1) == 0

    @pl.when(sequence_start)
    def _():
        state_ref[...] = jnp.zeros(state_ref.shape, F32)

    oc_ref[...] = _short_conv(cx_ref[...], halo_ref[...], taps_ref[...], sequence_start).astype(oc_ref.dtype)

    row_i = lax.broadcasted_iota(jnp.int32, (c, c), 0)
    col_i = lax.broadcasted_iota(jnp.int32, (c, c), 1)
    tri = jnp.where(row_i >= col_i, 1.0, 0.0).astype(BF16)
    lane_k = lax.broadcasted_iota(jnp.int32, (pair_w, LANES), 0)
    head_sum = [jnp.where((lane_k // GLA_DK) == hh, 1.0, 0.0).astype(BF16) for hh in range(2)]
    sub = GLA_SUBCHUNK
    key_row = lax.broadcasted_iota(jnp.int32, (c, pair_w), 0)
    lane_h = lax.broadcasted_iota(jnp.int32, (1, pair_w), 1) // GLA_DK
    head_lanes = [jnp.where(lane_h == hh, 1.0, 0.0) for hh in range(2)]
    srow = lax.broadcasted_iota(jnp.int32, (pair_w, pair_v), 0) // GLA_DK
    scol = lax.broadcasted_iota(jnp.int32, (pair_w, pair_v), 1) // GLA_DV
    own_head = srow == scol

    def chunk(ci, carry):
        r0 = pl.multiple_of(ci * c, c)
        rows = pl.ds(r0, c)
        x = _dot(al_ref[rows, :].astype(BF16), wa_ref[...]) + ba_ref[...]
        log_a = (jnp.minimum(x, 0.0) - jnp.log(1.0 + jnp.exp(-jnp.abs(x)))) * (1.0 / GLA_TAU)
        p1 = log_a.astype(BF16)
        r1 = log_a - p1.astype(F32)
        p2 = r1.astype(BF16)
        p3 = (r1 - p2.astype(F32)).astype(BF16)
        b_scr[...] = _dot(tri, p1) + _dot(tri, p2) + _dot(tri, p3)
        q_scr[...] = q_ref[rows, :] * (GLA_DK ** -0.5)
        k_scr[...] = k_ref[rows, :]
        v_scr[...] = v_ref[rows, :]

        staged = []
        for p in range(n_pairs):
            kl = slice(p * pair_w, (p + 1) * pair_w)
            vl = slice(p * pair_v, (p + 1) * pair_v)
            bp = b_scr[:, kl]
            qp = q_scr[:, kl]
            kp = k_scr[:, kl]
            vp = v_scr[:, vl]
            b_last = b_scr[c - 1:c, kl]
            st = state_ref[p]
            q_dec = (qp * jnp.exp(bp)).astype(BF16)
            k_dec = (kp * jnp.exp(b_last - bp)).astype(BF16)
            o_inter = _dot(q_dec, st.astype(BF16))
            vp_bf = vp.astype(BF16)
            kv = lax.dot_general(k_dec, vp_bf, (((0,), (0,)), ((), ())), preferred_element_type=F32)
            decay_col = jnp.transpose(jnp.broadcast_to(jnp.exp(b_last), (LANES, pair_w)))
            decay_col = jnp.concatenate([decay_col, decay_col], axis=1)
            state_ref[p] = decay_col * st + jnp.where(own_head, kv, 0.0)
            q_blocks, k_blocks = [], []
            for i in range(1, c // sub):
                r = i * sub
                b_r = b_scr[r:r + 1, kl]
                q_i = q_scr[r:r + sub, kl] * jnp.exp(b_scr[r:r + sub, kl] - b_r)
                pieces = [jnp.zeros((r, pair_w), F32), q_i]
                if c - r - sub:
                    pieces.append(jnp.zeros((c - r - sub, pair_w), F32))
                q_blocks.append(jnp.concatenate(pieces, axis=0))
                k_blocks.append(kp * jnp.exp(jnp.where(key_row < r, b_r - bp, -jnp.inf)))
            q_cat = jnp.concatenate(q_blocks, axis=1)
            k_cat = jnp.concatenate(k_blocks, axis=1).astype(BF16)
            a_far = [_dot_t((q_cat * jnp.concatenate([head_lanes[hh]] * len(q_blocks), axis=1)).astype(BF16), k_cat)
                     for hh in range(2)]
            ys = []
            for i in range(c // sub):
                r = i * sub
                for t0 in (r, r + 8):
                    n = r + sub - t0
                    bt = b_scr[t0:r + sub, kl]
                    qt = q_scr[t0:r + sub, kl]
                    t_idx = t0 + lax.broadcasted_iota(jnp.int32, (n, pair_w), 0)
                    for s in range(t0, t0 + 8):
                        e = jnp.exp(jnp.where(t_idx >= s, bt - b_scr[s:s + 1, kl], -jnp.inf))
                        ys.append(qt * e * k_scr[s:s + 1, kl])
            y = jnp.concatenate(ys, axis=0).astype(BF16)
            a_near = [_dot(y, head_sum[hh]) for hh in range(2)]
            staged.append((o_inter, vp_bf, a_far, a_near))

        for p in range(n_pairs):
            o_inter, vp_bf, a_far, a_near = staged[p]
            for hh in range(2):
                hv = slice(p * pair_v + hh * GLA_DV, p * pair_v + (hh + 1) * GLA_DV)
                o_h = o_inter[:, hh * GLA_DV:(hh + 1) * GLA_DV] + _dot(a_far[hh].astype(BF16),
                                                                       vp_bf[:, hh * GLA_DV:(hh + 1) * GLA_DV])
                a = a_near[hh]
                for i in range(c // sub):
                    r = i * sub
                    base = i * (8 * sub + 8 * 8)
                    upd = a[base:base + sub] * v_scr[r:r + 1, hv]
                    for idx in range(1, 8):
                        upd += a[base + idx * sub:base + (idx + 1) * sub] * v_scr[r + idx:r + idx + 1, hv]
                    base += 8 * sub
                    upd8 = a[base:base + 8] * v_scr[r + 8:r + 9, hv]
                    for idx in range(1, 8):
                        upd8 += a[base + idx * 8:base + (idx + 1) * 8] * v_scr[r + 8 + idx:r + 9 + idx, hv]
                    upd = jnp.concatenate([upd[0:8], upd[8:sub] + upd8], axis=0)
                    oi_scr[r:r + sub, hv] = upd + o_h[r:r + sub]

        o = oi_scr[...]
        gate = r_ref[rows, :]
        gate = gate * jax.nn.sigmoid(gate)
        for h in range(GLA_HEADS):
            hv = slice(h * GLA_DV, (h + 1) * GLA_DV)
            oh = o[:, hv]
            oh = oh * lax.rsqrt(jnp.mean(oh * oh, axis=-1, keepdims=True) + EPS)
            o_ref[rows, hv] = (oh * ng_ref[:, hv] * gate[:, hv]).astype(o_ref.dtype)
        return carry

    lax.fori_loop(0, tc // c, chunk, 0, unroll=4)


def gla_and_conv(z, w_a2, b_a, norm_g, conv_taps, layer, *, batch, seq, tc):
    assert seq % tc == 0 and tc % GLA_CHUNK == 0
    nb = seq // tc
    kw = GLA_HEADS * GLA_DK
    vw = GLA_HEADS * GLA_DV
    cw = 3 * CONV_DIM
    tok = lambda width, col: pl.BlockSpec((tc, width), lambda b, i: (b * nb + i, col // width))
    full = lambda a: _layer(layer, a.shape[1:], lambda b, i: (0, 0))
    c = GLA_CHUNK
    return pl.pallas_call(
        functools.partial(_gla_kernel, tc=tc),
        grid=(batch, nb),
        in_specs=[
            tok(kw, Z_GQ), tok(kw, Z_GK), tok(vw, Z_GV), tok(vw, Z_GR), tok(LANES, Z_AL), tok(cw, Z_CONV),
            pl.BlockSpec((SUBLANES, cw),
                         lambda b, i: (jnp.maximum((b * nb + i) * (tc // SUBLANES) - 1, 0), Z_CONV // cw)),
            full(w_a2), full(b_a), full(norm_g), full(conv_taps),
        ],
        out_specs=[pl.BlockSpec((tc, vw), lambda b, i: (b * nb + i, 0)),
                   pl.BlockSpec((tc, CONV_DIM), lambda b, i: (b * nb + i, 0))],
        out_shape=[jax.ShapeDtypeStruct((batch * seq, vw), BF16),
                   jax.ShapeDtypeStruct((batch * seq, CONV_DIM), BF16)],
        scratch_shapes=[
            pltpu.VMEM((GLA_HEADS // 2, 2 * GLA_DK, 2 * GLA_DV), F32),
            pltpu.VMEM((c, kw), F32), pltpu.VMEM((c, kw), F32), pltpu.VMEM((c, kw), F32),
            pltpu.VMEM((c, vw), F32), pltpu.VMEM((c, vw), F32),
        ],
        compiler_params=_params("arbitrary", "arbitrary"),
        name="gla",
    )(z, z, z, z, z, z, z, w_a2, b_a, norm_g, conv_taps)


def _short_conv(x, halo, w, sequence_start):
    d = CONV_DIM
    z = x[:, d:2 * d] * x[:, 2 * d:3 * d]
    hz = halo[:, d:2 * d] * halo[:, 2 * d:3 * d]
    hz = jnp.where(sequence_start, 0.0, hz)
    row = lax.broadcasted_iota(jnp.int32, z.shape, 0)
    z1 = jnp.where(row == 0, hz[7:8], pltpu.roll(z, 1, 0))
    z2 = jnp.where(row == 0, hz[6:7], jnp.where(row == 1, hz[7:8], pltpu.roll(z, 2, 0)))
    y = w[2:3, :] * z + w[1:2, :] * z1 + w[0:1, :] * z2
    return x[:, 0:d] * y


def _merge_kernel(h_ref, g_ref, a0_ref, a1_ref, a2_ref, wg0_ref, wg1_ref, wg2_ref, b0_ref, b1_ref, b2_ref,
                  wp0_ref, wp1_ref, wp2_ref, wo_ref, o_ref, u_ref):
    j = pl.program_id(1)

    @pl.when(j == 0)
    def _():
        h = h_ref[...]
        u_ref[...] = _rms(h, g_ref[...]).astype(BF16)
        o_ref[...] = h

    u = u_ref[...]
    merged = None
    for a_ref, wg_ref, b_ref, wp_ref in ((a0_ref, wg0_ref, b0_ref, wp0_ref),
                                         (a1_ref, wg1_ref, b1_ref, wp1_ref),
                                         (a2_ref, wg2_ref, b2_ref, wp2_ref)):
        gate = jax.nn.sigmoid(_dot(u, wg_ref[...]) + b_ref[...])
        term = gate * _dot(a_ref[...], wp_ref[...])
        merged = term if merged is None else merged + term
    o_ref[...] += _dot(merged.astype(BF16), wo_ref[...])


def merge(h, g, branches, w_gate, gate_col0, b_gate, w_projs, w_out, layer, *, tm, tn):
    m, d = h.shape
    assert m % tm == 0 and d % tn == 0 and gate_col0 % tn == 0
    nj = d // tn
    g0 = gate_col0 // tn
    row = lambda a: pl.BlockSpec((tm, a.shape[1]), lambda i, j: (i, 0))
    gate_w = lambda b: _layer(layer, (d, tn), lambda i, j: (0, g0 + b * nj + j))
    gate_b = lambda b: _layer(layer, (1, tn), lambda i, j: (0, b * nj + j))
    proj_w = lambda w: _layer(layer, (w.shape[1], tn), lambda i, j: (0, j))
    return pl.pallas_call(
        _merge_kernel,
        grid=(m // tm, nj),
        in_specs=[
            row(h), _vec(layer, d),
            row(branches[0]), row(branches[1]), row(branches[2]),
            gate_w(0), gate_w(1), gate_w(2), gate_b(0), gate_b(1), gate_b(2),
            proj_w(w_projs[0]), proj_w(w_projs[1]), proj_w(w_projs[2]),
            _layer(layer, (tn, d), lambda i, j: (j, 0)),
        ],
        out_specs=pl.BlockSpec((tm, d), lambda i, j: (i, 0)),
        out_shape=jax.ShapeDtypeStruct((m, d), F32),
        scratch_shapes=[pltpu.VMEM((tm, d), BF16)],
        compiler_params=_params("arbitrary", "arbitrary"),
        name="merge",
    )(h, g, *branches, w_gate, w_gate, w_gate, b_gate, b_gate, b_gate, *w_projs, w_out)


def _xattn_kernel(h_ref, g_ref, wq_ref, k_ref, v_ref, wo_ref, o_ref):
    h = h_ref[...]
    hn = _rms(h, g_ref[...]).astype(BF16)
    q = (_dot(hn, wq_ref[...]) * (X_HEAD_DIM ** -0.5)).astype(BF16)
    heads = [slice(j * X_HEAD_DIM, (j + 1) * X_HEAD_DIM) for j in range(X_HEADS)]
    scores = [_dot_t(q[:, hs], k_ref[:, hs]) for hs in heads]
    outs = []
    for hs, s in zip(heads, scores):
        p = jnp.exp(s - jnp.max(s, axis=-1, keepdims=True))
        o = _dot(p.astype(BF16), v_ref[:, hs]) / jnp.sum(p, axis=-1, keepdims=True)
        outs.append(o.astype(BF16))
    o_ref[...] = h + _dot(jnp.concatenate(outs, axis=1), wo_ref[...])


def xattn(h, g, w_q, kv, w_o, layer, *, seq, tm):
    m, d = h.shape
    assert seq % tm == 0
    nb = seq // tm
    once = pl.Buffered(1)
    return pl.pallas_call(
        _xattn_kernel,
        grid=(m // tm,),
        in_specs=[
            pl.BlockSpec((tm, d), lambda i: (i, 0)),
            _vec(layer, d),
            pl.BlockSpec((None, d, d), lambda i: (layer, 0, 0), pipeline_mode=once),
            pl.BlockSpec((MEM_LEN, d), lambda i: (i // nb, 0)),
            pl.BlockSpec((MEM_LEN, d), lambda i: (i // nb, 1)),
            pl.BlockSpec((None, d, d), lambda i: (layer, 0, 0), pipeline_mode=once),
        ],
        out_specs=pl.BlockSpec((tm, d), lambda i: (i, 0)),
        out_shape=jax.ShapeDtypeStruct((m, d), F32),
        compiler_params=_params("arbitrary"),
        name="xattn",
    )(h, g, w_q, kv, kv, w_o)


PACK_BLOCK = 512
_IN_SPLIT_NAMES = ("c_q", "c_kv", "k_rope", "g_q", "g_k", "g_v", "g_r", "a_low", "conv", "gate")
_IN_SPLIT_WIDTHS = (MLA_RANK, MLA_RANK, MLA_ROPE, GLA_HEADS * GLA_DK, GLA_HEADS * GLA_DK, GLA_HEADS * GLA_DV,
                    GLA_HEADS * GLA_DV, GLA_RANK, 3 * CONV_DIM, 3 * D_MODEL)
_SRC = {name: sum(_IN_SPLIT_WIDTHS[:i]) for i, name in enumerate(_IN_SPLIT_NAMES)}
_PACK_SRC = (_SRC["conv"], _SRC["conv"] + PACK_BLOCK, _SRC["conv"] + 2 * PACK_BLOCK, _SRC["c_q"], _SRC["c_kv"],
             _SRC["g_v"], _SRC["g_r"], _SRC["g_q"], _SRC["k_rope"])
_PACK_SPECIAL = len(_PACK_SRC) - 1
_SRC_A_LOW = _SRC["a_low"]
_SRC_GATE = _SRC["gate"]
GATE_COL0 = len(_PACK_SRC) * PACK_BLOCK


def _pack_in_proj_kernel(src_ref, w_t_ref, a_low_ref, o_ref):
    del src_ref
    j = pl.program_id(1)

    @pl.when(j != _PACK_SPECIAL)
    def _():
        o_ref[...] = jnp.transpose(w_t_ref[0]).astype(BF16)

    @pl.when(j == _PACK_SPECIAL)
    def _():
        half = MLA_ROPE // 2
        k_rope = w_t_ref[0, 0:MLA_ROPE, :]
        row = lax.broadcasted_iota(jnp.int32, a_low_ref.shape[1:], 0)
        a_low = jnp.where(row < GLA_RANK, a_low_ref[0], 0.0)
        rest = jnp.zeros((PACK_BLOCK - 2 * LANES, k_rope.shape[1]), F32)
        blk = jnp.concatenate([k_rope, k_rope[half:], k_rope[:half], a_low, rest], axis=0)
        o_ref[...] = jnp.transpose(blk).astype(BF16)


def _pack_in_proj(w):
    depth, d, n = w.shape
    w_t = jnp.swapaxes(w, 1, 2)
    n_gate = n - _SRC_GATE
    assert n_gate % PACK_BLOCK == 0
    rows = _PACK_SRC + tuple(_SRC_GATE + PACK_BLOCK * k for k in range(n_gate // PACK_BLOCK))
    assert all(r % SUBLANES == 0 for r in rows) and _SRC_A_LOW % SUBLANES == 0
    src = jnp.asarray([r // SUBLANES for r in rows], jnp.int32)
    n_blocks = src.shape[0]
    grid_spec = pltpu.PrefetchScalarGridSpec(
        num_scalar_prefetch=1,
        grid=(depth, n_blocks),
        in_specs=[
            pl.BlockSpec((pl.Element(1), pl.Element(PACK_BLOCK), pl.Element(d)),
                         lambda l, j, src: (l, src[j] * SUBLANES, 0)),
            pl.BlockSpec((pl.Element(1), pl.Element(LANES), pl.Element(d)), lambda l, j, src: (l, _SRC_A_LOW, 0)),
        ],
        out_specs=pl.BlockSpec((None, d, PACK_BLOCK), lambda l, j, src: (l, 0, j)),
    )
    return pl.pallas_call(
        _pack_in_proj_kernel,
        grid_spec=grid_spec,
        out_shape=jax.ShapeDtypeStruct((depth, d, n_blocks * PACK_BLOCK), BF16),
        compiler_params=_params("arbitrary", "arbitrary"),
        name="pack_in_proj",
    )(src, w_t, w_t)


def _pack_uq(w):
    depth = w.shape[0]
    w = w.reshape(depth, MLA_RANK, MLA_HEADS, MLA_NOPE + MLA_ROPE)
    half = MLA_ROPE // 2
    nope, pe = w[..., :MLA_NOPE], w[..., MLA_NOPE:]
    swap = jnp.concatenate([pe[..., half:], pe[..., :half]], axis=-1)
    packed = jnp.concatenate([nope, pe, swap], axis=-1).reshape(depth, MLA_RANK, MLA_HEADS * MLA_QK_PAD)
    return jnp.swapaxes(packed, 1, 2).astype(BF16)


def _pack_ukv(w):
    depth = w.shape[0]
    w = w.reshape(depth, MLA_RANK, MLA_HEADS, MLA_NOPE + MLA_V)
    wk = w[..., :MLA_NOPE].reshape(depth, MLA_RANK, MLA_HEADS * MLA_NOPE)
    wv = w[..., MLA_NOPE:].reshape(depth, MLA_RANK, MLA_HEADS * MLA_V)
    return wk.astype(BF16), jnp.swapaxes(wv, 1, 2).astype(BF16)


def _rope_table(seq):
    inv_freq = 1.0 / (ROPE_THETA ** (jnp.arange(0, MLA_ROPE, 2, dtype=F32) / MLA_ROPE))
    ang = jnp.arange(seq, dtype=F32)[:, None] * inv_freq[None, :]
    cos, sin = jnp.cos(ang), jnp.sin(ang)
    return jnp.concatenate([cos, cos, -sin, sin], axis=1)


def kernel(x, mem, ffn1_norm, ffn1_w_in, ffn1_w_out, mix_norm, mix_w_in, mix_b_gate, mla_q_norm, mla_kv_norm,
           mla_w_uq, mla_w_ukv, mla_w_proj, gla_w_a2, gla_b_a, gla_norm, gla_w_proj, conv_w, conv_w_proj,
           mix_w_out, xattn_norm, mem_norm, xattn_w_q, xattn_w_kv, xattn_w_o, ffn2_norm, ffn2_w_in,
           ffn2_w_out, final_norm):
    batch, seq, d = x.shape
    depth = ffn1_norm.shape[0]
    tokens = batch * seq
    table = _rope_table(seq)
    h = x.reshape(tokens, d)
    mem2 = mem.reshape(batch * mem.shape[1], d)
    bf = lambda w: w.astype(BF16)
    vec = lambda p: p.reshape(depth, 1, -1)

    w_mix_in = _pack_in_proj(mix_w_in)
    wq_t = _pack_uq(mla_w_uq)
    wk, wv_t = _pack_ukv(mla_w_ukv)
    w_a2 = bf(jnp.concatenate(
        [gla_w_a2, jnp.zeros((depth, LANES - GLA_RANK, GLA_HEADS * GLA_DK), F32)], axis=1))
    w_projs = (bf(mla_w_proj), bf(gla_w_proj), bf(conv_w_proj))
    w_mix_out, w_xq, w_xo = bf(mix_w_out), bf(xattn_w_q), bf(xattn_w_o)
    conv_taps = conv_w.reshape(depth, 3, CONV_DIM)

    t = TILES
    for l in range(depth):
        h = ffn(h, vec(ffn1_norm), ffn1_w_in, ffn1_w_out, final_norm, l, tm=t.ffn_rows, tf=t.ffn_cols,
                final_norm=False)

        z = norm_matmul(h, vec(mix_norm), w_mix_in, l, n=Z_WIDTH, tm=t.rows, tn=Z_WIDTH, out_dtype=F32)
        q_t, k, v_t = mla_prep(z, table, vec(mla_q_norm), vec(mla_kv_norm), wq_t, wk, wv_t, l,
                               batch=batch, seq=seq, tm=t.rows)
        a_mla = mla_attn(q_t, k, v_t, tq=t.rows, heads=t.attn_heads).reshape(tokens, MLA_HEADS * MLA_V)
        a_gla, a_conv = gla_and_conv(z, w_a2, vec(gla_b_a), vec(gla_norm), conv_taps, l,
                                     batch=batch, seq=seq, tc=t.rows)
        h = merge(h, vec(mix_norm), (a_mla, a_gla, a_conv), w_mix_in, GATE_COL0, vec(mix_b_gate), w_projs,
                  w_mix_out, l, tm=t.rows, tn=t.merge_cols)

        kv = norm_matmul(mem2, vec(mem_norm), xattn_w_kv, l, n=2 * d, tm=mem2.shape[0], tn=t.kv_cols,
                         out_dtype=BF16)
        h = xattn(h, vec(xattn_norm), w_xq, kv, w_xo, l, seq=seq, tm=t.rows)

        h = ffn(h, vec(ffn2_norm), ffn2_w_in, ffn2_w_out, final_norm, l, tm=t.ffn_rows, tf=t.ffn_cols,
                final_norm=(l == depth - 1))
    return h.reshape(batch, seq, d)
```

```python
import functools
from typing import NamedTuple

import jax
import jax.numpy as jnp
from jax import lax
from jax.experimental import pallas as pl
from jax.experimental.pallas import tpu as pltpu

F32 = jnp.float32
BF16 = jnp.bfloat16

EPS = 1e-6
D_MODEL = 2048
D_FF = 5632
MLA_HEADS = 8
MLA_RANK = 512
MLA_NOPE = 128
MLA_ROPE = 64
MLA_V = 128
MLA_QK_PAD = 256
ROPE_THETA = 10000.0
LOG2_E = 1.4426950408889634
GLA_HEADS = 4
GLA_DK = 64
GLA_DV = 128
GLA_RANK = 16
GLA_TAU = 16.0
GLA_CHUNK = 64
GLA_SUBCHUNK = 16
CONV_DIM = 512
X_HEADS = 4
X_HEAD_DIM = D_MODEL // X_HEADS
MEM_LEN = 256

LANES = 128
SUBLANES = 8
VMEM_LIMIT_BYTES = 56 * 1024 * 1024


class Tiles(NamedTuple):
    rows: int = 512
    ffn_rows: int = 1024
    ffn_cols: int = 256
    merge_cols: int = 512
    kv_cols: int = 1024
    attn_heads: int = 8


TILES = Tiles()

Z_CONV = 0
Z_CQ = 1536
Z_CKV = 2048
Z_GV = 2560
Z_GR = 3072
Z_GQ = 3584
Z_GK = 3840
Z_KR = 4096
Z_AL = 4224
Z_WIDTH = 4352


def _params(*sem):
    return pltpu.CompilerParams(dimension_semantics=sem, vmem_limit_bytes=VMEM_LIMIT_BYTES)


def _rms(x, g):
    return x * lax.rsqrt(jnp.mean(x * x, axis=-1, keepdims=True) + EPS) * g


def _dot(a, b):
    return jnp.dot(a, b, preferred_element_type=F32)


def _dot_t(a, b):
    return lax.dot_general(a, b, (((1,), (1,)), ((), ())), preferred_element_type=F32)


def _norm_matmul_kernel(x_ref, g_ref, w_ref, o_ref):
    xn = _rms(x_ref[...], g_ref[...]).astype(BF16)
    o_ref[...] = _dot(xn, w_ref[...].astype(BF16)).astype(o_ref.dtype)


def _layer(layer, block, index):
    return pl.BlockSpec((None,) + tuple(block), lambda *ids: (layer,) + tuple(index(*ids)))


def _vec(layer, width):
    return _layer(layer, (1, width), lambda *ids: (0, 0))


def norm_matmul(x, g, w, layer, *, n, tm, tn, out_dtype):
    m, k = x.shape
    assert m % tm == 0 and n % tn == 0 and n <= w.shape[2]
    w_mode = pl.Buffered(1) if n == tn else None
    return pl.pallas_call(
        _norm_matmul_kernel,
        grid=(n // tn, m // tm),
        in_specs=[
            pl.BlockSpec((tm, k), lambda j, i: (i, 0)),
            _vec(layer, k),
            pl.BlockSpec((None, k, tn), lambda j, i: (layer, 0, j), pipeline_mode=w_mode),
        ],
        out_specs=pl.BlockSpec((tm, tn), lambda j, i: (i, j)),
        out_shape=jax.ShapeDtypeStruct((m, n), out_dtype),
        compiler_params=_params("arbitrary", "arbitrary"),
        name="norm_matmul",
    )(x, g, w)


def _ffn_kernel(x_ref, g_ref, w_in_hbm, w_out_hbm, gf_ref, o_ref, xn_ref, wg_buf, wu_buf, wo_buf, sem,
                *, layer, tf, nf, final_norm):
    i = pl.program_id(0)
    f = nf * tf

    def chunk_copies(j, slot):
        c0 = pl.multiple_of(j * tf, tf)
        return (
            pltpu.make_async_copy(w_in_hbm.at[layer, :, pl.ds(c0, tf)], wg_buf.at[slot], sem.at[0, slot]),
            pltpu.make_async_copy(w_in_hbm.at[layer, :, pl.ds(f + c0, tf)], wu_buf.at[slot], sem.at[1, slot]),
            pltpu.make_async_copy(w_out_hbm.at[layer, pl.ds(c0, tf), :], wo_buf.at[slot], sem.at[2, slot]),
        )

    def start(j, slot):
        for copy in chunk_copies(j, slot):
            copy.start()

    def wait(j, slot):
        for copy in chunk_copies(j, slot):
            copy.wait()

    @pl.when(i == 0)
    def _():
        start(0, 0)

    x = x_ref[...]
    xn_ref[...] = _rms(x, g_ref[...]).astype(BF16)
    o_ref[...] = x

    def chunk(j, slot, start_next):
        wait(j, slot)
        start_next()
        xn = xn_ref[...]
        gate = _dot(xn, wg_buf[slot].astype(BF16))
        up = _dot(xn, wu_buf[slot].astype(BF16))
        act = (0.5 * gate) * jax.nn.sigmoid(gate) * up
        o_ref[...] += _dot(act.astype(BF16), wo_buf[slot].astype(BF16))

    def pair(jj, carry):
        j0 = 2 * jj
        chunk(j0, 0, lambda: start(j0 + 1, 1))

        def start_after_odd():
            @pl.when(j0 + 2 < nf)
            def _():
                start(j0 + 2, 0)

            @pl.when(jnp.logical_and(j0 + 2 == nf, i + 1 < pl.num_programs(0)))
            def _():
                start(0, 0)

        chunk(j0 + 1, 1, start_after_odd)
        return carry

    lax.fori_loop(0, nf // 2, pair, 0)

    if final_norm:
        o_ref[...] = _rms(o_ref[...], gf_ref[...])


def ffn(x, g, w_in, w_out, g_final, layer, *, tm, tf, final_norm):
    m, d = x.shape
    f = w_out.shape[1]
    assert m % tm == 0 and f % (2 * tf) == 0
    nf = f // tf
    return pl.pallas_call(
        functools.partial(_ffn_kernel, layer=layer, tf=tf, nf=nf, final_norm=final_norm),
        grid=(m // tm,),
        in_specs=[
            pl.BlockSpec((tm, d), lambda i: (i, 0)),
            _vec(layer, d),
            pl.BlockSpec(memory_space=pl.ANY),
            pl.BlockSpec(memory_space=pl.ANY),
            pl.BlockSpec((1, d), lambda i: (0, 0)),
        ],
        out_specs=pl.BlockSpec((tm, d), lambda i: (i, 0)),
        out_shape=jax.ShapeDtypeStruct((m, d), F32),
        scratch_shapes=[
            pltpu.VMEM((tm, d), BF16),
            pltpu.VMEM((2, d, tf), F32), pltpu.VMEM((2, d, tf), F32), pltpu.VMEM((2, tf, d), F32),
            pltpu.SemaphoreType.DMA((3, 2)),
        ],
        compiler_params=_params("arbitrary"),
        name="ffn",
    )(x, g, w_in, w_out, g_final.reshape(1, d))


def _rope_pairs(x, table):
    y = x * table
    return y + pltpu.roll(y, MLA_ROPE, 1)


def _mla_prep_kernel(cq_ref, ckv_ref, kr_ref, tab_ref, tab_t_ref, gq_ref, gkv_ref, wq_t_ref, wk_ref, wv_t_ref,
                     q_t_ref, k_ref, v_t_ref):
    scale = (MLA_NOPE + MLA_ROPE) ** -0.5 * LOG2_E
    nq_t = jnp.transpose(_rms(cq_ref[...], gq_ref[...])).astype(BF16)
    q_t = _dot(wq_t_ref[...], nq_t) * scale
    nkv = _rms(ckv_ref[...], gkv_ref[...])
    kn = _dot(nkv.astype(BF16), wk_ref[...])
    v_t = _dot(wv_t_ref[...], jnp.transpose(nkv).astype(BF16))
    tab = tab_ref[...]
    tab_t = tab_t_ref[...]
    lane = lax.broadcasted_iota(jnp.int32, tab.shape, 1)
    k_pe = jnp.where(lane < MLA_ROPE, _rope_pairs(kr_ref[...], tab), 0.0).astype(BF16)
    zeros = jnp.zeros((MLA_QK_PAD - MLA_NOPE - MLA_ROPE, q_t.shape[1]), BF16)
    for h in range(MLA_HEADS):
        q0 = h * MLA_QK_PAD
        q_t_ref[0, h, 0:MLA_NOPE, :] = q_t[q0:q0 + MLA_NOPE].astype(BF16)
        pe = q_t[q0 + MLA_NOPE:q0 + MLA_QK_PAD] * tab_t
        q_t_ref[0, h, MLA_NOPE:MLA_NOPE + MLA_ROPE, :] = (pe[0:MLA_ROPE] + pe[MLA_ROPE:2 * MLA_ROPE]).astype(BF16)
        q_t_ref[0, h, MLA_NOPE + MLA_ROPE:MLA_QK_PAD, :] = zeros
        k_ref[0, h, :, 0:LANES] = kn[:, h * LANES:(h + 1) * LANES].astype(BF16)
        k_ref[0, h, :, LANES:2 * LANES] = k_pe
        v_t_ref[0, h] = v_t[h * MLA_V:(h + 1) * MLA_V].astype(BF16)


def mla_prep(z, table, g_q, g_kv, wq_t, wk, wv_t, layer, *, batch, seq, tm):
    assert seq % tm == 0
    nb = seq // tm
    r = MLA_RANK
    hw = MLA_HEADS
    tok = lambda width, col: pl.BlockSpec((tm, width), lambda i: (i, col // width))
    full = lambda a: _layer(layer, a.shape[1:], lambda i: (0, 0))
    feat_major = lambda width: pl.BlockSpec((1, hw, width, tm), lambda i: (i // nb, 0, 0, i % nb))
    return pl.pallas_call(
        _mla_prep_kernel,
        grid=(batch * nb,),
        in_specs=[
            tok(r, Z_CQ), tok(r, Z_CKV), tok(LANES, Z_KR),
            pl.BlockSpec((tm, LANES), lambda i: (i % nb, 0)),
            pl.BlockSpec((LANES, tm), lambda i: (0, i % nb)),
            full(g_q), full(g_kv), full(wq_t), full(wk), full(wv_t),
        ],
        out_specs=[
            feat_major(MLA_QK_PAD),
            pl.BlockSpec((1, hw, tm, MLA_QK_PAD), lambda i: (i // nb, 0, i % nb, 0)),
            feat_major(MLA_V),
        ],
        out_shape=[
            jax.ShapeDtypeStruct((batch, hw, MLA_QK_PAD, seq), BF16),
            jax.ShapeDtypeStruct((batch, hw, seq, MLA_QK_PAD), BF16),
            jax.ShapeDtypeStruct((batch, hw, MLA_V, seq), BF16),
        ],
        compiler_params=_params("arbitrary"),
        name="mla_prep",
    )(z, z, z, table, jnp.transpose(table), g_q, g_kv, wq_t, wk, wv_t)


def _mla_attn_kernel(q_t_ref, k_ref, v_t_ref, o_ref, m_ref, l_ref, acc_ref, *, tq, heads):
    qi = pl.program_id(2)
    m_ref[...] = jnp.full(m_ref.shape, -jnp.inf, F32)
    l_ref[...] = jnp.zeros(l_ref.shape, F32)
    acc_ref[...] = jnp.zeros(acc_ref.shape, F32)

    def step(j, masked):
        start = pl.multiple_of(j * tq, tq)
        scores = [_dot(k_ref[0, g, pl.ds(start, tq), :], q_t_ref[0, g]) for g in range(heads)]
        for g in range(heads):
            s = scores[g]
            if masked:
                key = lax.broadcasted_iota(jnp.int32, s.shape, 0)
                qry = lax.broadcasted_iota(jnp.int32, s.shape, 1)
                s = jnp.where(key <= qry, s, -jnp.inf)
            m_old = m_ref[g]
            m_new = jnp.maximum(m_old, jnp.max(s, axis=0, keepdims=True))
            p = jnp.exp2(s - m_new)
            alpha = jnp.exp2(m_old - m_new)
            l_ref[g] = alpha * l_ref[g] + jnp.sum(p, axis=0, keepdims=True)
            acc_ref[g] = alpha * acc_ref[g] + _dot(v_t_ref[0, g, :, pl.ds(start, tq)], p.astype(BF16))
            m_ref[g] = m_new

    def body(j, carry):
        step(j, False)
        return carry

    lax.fori_loop(0, qi, body, 0)
    step(qi, True)
    for g in range(heads):
        o_t = acc_ref[g] / l_ref[g]
        o_ref[0, :, g * MLA_V:(g + 1) * MLA_V] = jnp.transpose(o_t).astype(o_ref.dtype)


def mla_attn(q_t, k, v_t, *, tq, heads):
    b, h, s, dq = k.shape
    dv = v_t.shape[2]
    assert s % tq == 0 and h % heads == 0
    return pl.pallas_call(
        functools.partial(_mla_attn_kernel, tq=tq, heads=heads),
        grid=(b, h // heads, s // tq),
        in_specs=[
            pl.BlockSpec((1, heads, dq, tq), lambda bi, hi, qi: (bi, hi, 0, qi)),
            pl.BlockSpec((1, heads, s, dq), lambda bi, hi, qi: (bi, hi, 0, 0), pipeline_mode=pl.Buffered(1)),
            pl.BlockSpec((1, heads, dv, s), lambda bi, hi, qi: (bi, hi, 0, 0), pipeline_mode=pl.Buffered(1)),
        ],
        out_specs=pl.BlockSpec((1, tq, heads * dv), lambda bi, hi, qi: (bi, qi, hi)),
        out_shape=jax.ShapeDtypeStruct((b, s, h * dv), BF16),
        scratch_shapes=[pltpu.VMEM((heads, 1, tq), F32), pltpu.VMEM((heads, 1, tq), F32),
                        pltpu.VMEM((heads, dv, tq), F32)],
        compiler_params=_params("arbitrary", "arbitrary", "arbitrary"),
        name="mla_attn",
    )(q_t, k, v_t)


def _gla_kernel(q_ref, k_ref, v_ref, r_ref, al_ref, cx_ref, halo_ref, wa_ref, ba_ref, ng_ref, taps_ref,
                o_ref, oc_ref, state_ref, b_scr, q_scr, k_scr, v_scr, oi_scr, *, tc):
    c = GLA_CHUNK
    pair_w = 2 * GLA_DK
    pair_v = 2 * GLA_DV
    n_pairs = GLA_HEADS // 2
    sequence_start = pl.program_id(1) == 0

    @pl.when(sequence_start)
    def _():
        state_ref[...] = jnp.zeros(state_ref.shape, F32)

    oc_ref[...] = _short_conv(cx_ref[...], halo_ref[...], taps_ref[...], sequence_start).astype(oc_ref.dtype)

    row_i = lax.broadcasted_iota(jnp.int32, (c, c), 0)
    col_i = lax.broadcasted_iota(jnp.int32, (c, c), 1)
    tri = jnp.where(row_i >= col_i, 1.0, 0.0).astype(BF16)
    lane_k = lax.broadcasted_iota(jnp.int32, (pair_w, LANES), 0)
    head_sum = [jnp.where((lane_k // GLA_DK) == hh, 1.0, 0.0).astype(BF16) for hh in range(2)]
    sub = GLA_SUBCHUNK
    key_row = lax.broadcasted_iota(jnp.int32, (c, pair_w), 0)
    lane_h = lax.broadcasted_iota(jnp.int32, (1, pair_w), 1) // GLA_DK
    head_lanes = [jnp.where(lane_h == hh, 1.0, 0.0) for hh in range(2)]
    srow = lax.broadcasted_iota(jnp.int32, (pair_w, pair_v), 0) // GLA_DK
    scol = lax.broadcasted_iota(jnp.int32, (pair_w, pair_v), 1) // GLA_DV
    own_head = srow == scol

    def chunk(ci, carry):
        r0 = pl.multiple_of(ci * c, c)
        rows = pl.ds(r0, c)
        x = _dot(al_ref[rows, :].astype(BF16), wa_ref[...]) + ba_ref[...]
        log_a = (jnp.minimum(x, 0.0) - jnp.log(1.0 + jnp.exp(-jnp.abs(x)))) * (1.0 / GLA_TAU)
        p1 = log_a.astype(BF16)
        r1 = log_a - p1.astype(F32)
        p2 = r1.astype(BF16)
        p3 = (r1 - p2.astype(F32)).astype(BF16)
        b_scr[...] = _dot(tri, p1) + _dot(tri, p2) + _dot(tri, p3)
        q_scr[...] = q_ref[rows, :] * (GLA_DK ** -0.5)
        k_scr[...] = k_ref[rows, :]
        v_scr[...] = v_ref[rows, :]

        staged = []
        for p in range(n_pairs):
            kl = slice(p * pair_w, (p + 1) * pair_w)
            vl = slice(p * pair_v, (p + 1) * pair_v)
            bp = b_scr[:, kl]
            qp = q_scr[:, kl]
            kp = k_scr[:, kl]
            vp = v_scr[:, vl]
            b_last = b_scr[c - 1:c, kl]
            st = state_ref[p]
            q_dec = (qp * jnp.exp(bp)).astype(BF16)
            k_dec = (kp * jnp.exp(b_last - bp)).astype(BF16)
            o_inter = _dot(q_dec, st.astype(BF16))
            vp_bf = vp.astype(BF16)
            kv = lax.dot_general(k_dec, vp_bf, (((0,), (0,)), ((), ())), preferred_element_type=F32)
            decay_col = jnp.transpose(jnp.broadcast_to(jnp.exp(b_last), (LANES, pair_w)))
            decay_col = jnp.concatenate([decay_col, decay_col], axis=1)
            state_ref[p] = decay_col * st + jnp.where(own_head, kv, 0.0)
            q_blocks, k_blocks = [], []
            for i in range(1, c // sub):
                r = i * sub
                b_r = b_scr[r:r + 1, kl]
                q_i = q_scr[r:r + sub, kl] * jnp.exp(b_scr[r:r + sub, kl] - b_r)
                pieces = [jnp.zeros((r, pair_w), F32), q_i]
                if c - r - sub:
                    pieces.append(jnp.zeros((c - r - sub, pair_w), F32))
                q_blocks.append(jnp.concatenate(pieces, axis=0))
                k_blocks.append(kp * jnp.exp(jnp.where(key_row < r, b_r - bp, -jnp.inf)))
            q_cat = jnp.concatenate(q_blocks, axis=1)
            k_cat = jnp.concatenate(k_blocks, axis=1).astype(BF16)
            a_far = [_dot_t((q_cat * jnp.concatenate([head_lanes[hh]] * len(q_blocks), axis=1)).astype(BF16), k_cat)
                     for hh in range(2)]
            ys = []
            for i in range(c // sub):
                r = i * sub
                for t0 in (r, r + 8):
                    n = r + sub - t0
                    bt = b_scr[t0:r + sub, kl]
                    qt = q_scr[t0:r + sub, kl]
                    t_idx = t0 + lax.broadcasted_iota(jnp.int32, (n, pair_w), 0)
                    for s in range(t0, t0 + 8):
                        e = jnp.exp(jnp.where(t_idx >= s, bt - b_scr[s:s + 1, kl], -jnp.inf))
                        ys.append(qt * e * k_scr[s:s + 1, kl])
            y = jnp.concatenate(ys, axis=0).astype(BF16)
            a_near = [_dot(y, head_sum[hh]) for hh in range(2)]
            staged.append((o_inter, vp_bf, a_far, a_near))

        for p in range(n_pairs):
            o_inter, vp_bf, a_far, a_near = staged[p]
            for hh in range(2):
                hv = slice(p * pair_v + hh * GLA_DV, p * pair_v + (hh + 1) * GLA_DV)
                o_h = o_inter[:, hh * GLA_DV:(hh + 1) * GLA_DV] + _dot(a_far[hh].astype(BF16),
                                                                       vp_bf[:, hh * GLA_DV:(hh + 1) * GLA_DV])
                a = a_near[hh]
                for i in range(c // sub):
                    r = i * sub
                    base = i * (8 * sub + 8 * 8)
                    upd = a[base:base + sub] * v_scr[r:r + 1, hv]
                    for idx in range(1, 8):
                        upd += a[base + idx * sub:base + (idx + 1) * sub] * v_scr[r + idx:r + idx + 1, hv]
                    base += 8 * sub
                    upd8 = a[base:base + 8] * v_scr[r + 8:r + 9, hv]
                    for idx in range(1, 8):
                        upd8 += a[base + idx * 8:base + (idx + 1) * 8] * v_scr[r + 8 + idx:r + 9 + idx, hv]
                    upd = jnp.concatenate([upd[0:8], upd[8:sub] + upd8], axis=0)
                    oi_scr[r:r + sub, hv] = upd + o_h[r:r + sub]

        o = oi_scr[...]
        gate = r_ref[rows, :]
        gate = gate * jax.nn.sigmoid(gate)
        for h in range(GLA_HEADS):
            hv = slice(h * GLA_DV, (h + 1) * GLA_DV)
            oh = o[:, hv]
            oh = oh * lax.rsqrt(jnp.mean(oh * oh, axis=-1, keepdims=True) + EPS)
            o_ref[rows, hv] = (oh * ng_ref[:, hv] * gate[:, hv]).astype(o_ref.dtype)
        return carry

    lax.fori_loop(0, tc // c, chunk, 0, unroll=4)


def gla_and_conv(z, w_a2, b_a, norm_g, conv_taps, layer, *, batch, seq, tc):
    assert seq % tc == 0 and tc % GLA_CHUNK == 0
    nb = seq // tc
    kw = GLA_HEADS * GLA_DK
    vw = GLA_HEADS * GLA_DV
    cw = 3 * CONV_DIM
    tok = lambda width, col: pl.BlockSpec((tc, width), lambda b, i: (b * nb + i, col // width))
    full = lambda a: _layer(layer, a.shape[1:], lambda b, i: (0, 0))
    c = GLA_CHUNK
    return pl.pallas_call(
        functools.partial(_gla_kernel, tc=tc),
        grid=(batch, nb),
        in_specs=[
            tok(kw, Z_GQ), tok(kw, Z_GK), tok(vw, Z_GV), tok(vw, Z_GR), tok(LANES, Z_AL), tok(cw, Z_CONV),
            pl.BlockSpec((SUBLANES, cw),
                         lambda b, i: (jnp.maximum((b * nb + i) * (tc // SUBLANES) - 1, 0), Z_CONV // cw)),
            full(w_a2), full(b_a), full(norm_g), full(conv_taps),
        ],
        out_specs=[pl.BlockSpec((tc, vw), lambda b, i: (b * nb + i, 0)),
                   pl.BlockSpec((tc, CONV_DIM), lambda b, i: (b * nb + i, 0))],
        out_shape=[jax.ShapeDtypeStruct((batch * seq, vw), BF16),
                   jax.ShapeDtypeStruct((batch * seq, CONV_DIM), BF16)],
        scratch_shapes=[
            pltpu.VMEM((GLA_HEADS // 2, 2 * GLA_DK, 2 * GLA_DV), F32),
            pltpu.VMEM((c, kw), F32), pltpu.VMEM((c, kw), F32), pltpu.VMEM((c, kw), F32),
            pltpu.VMEM((c, vw), F32), pltpu.VMEM((c, vw), F32),
        ],
        compiler_params=_params("arbitrary", "arbitrary"),
        name="gla",
    )(z, z, z, z, z, z, z, w_a2, b_a, norm_g, conv_taps)


def _short_conv(x, halo, w, sequence_start):
    d = CONV_DIM
    z = x[:, d:2 * d] * x[:, 2 * d:3 * d]
    hz = halo[:, d:2 * d] * halo[:, 2 * d:3 * d]
    hz = jnp.where(sequence_start, 0.0, hz)
    row = lax.broadcasted_iota(jnp.int32, z.shape, 0)
    z1 = jnp.where(row == 0, hz[7:8], pltpu.roll(z, 1, 0))
    z2 = jnp.where(row == 0, hz[6:7], jnp.where(row == 1, hz[7:8], pltpu.roll(z, 2, 0)))
    y = w[2:3, :] * z + w[1:2, :] * z1 + w[0:1, :] * z2
    return x[:, 0:d] * y


def _merge_kernel(h_ref, g_ref, a0_ref, a1_ref, a2_ref, wg0_ref, wg1_ref, wg2_ref, b0_ref, b1_ref, b2_ref,
                  wp0_ref, wp1_ref, wp2_ref, wo_ref, o_ref, u_ref):
    j = pl.program_id(1)

    @pl.when(j == 0)
    def _():
        h = h_ref[...]
        u_ref[...] = _rms(h, g_ref[...]).astype(BF16)
        o_ref[...] = h

    u = u_ref[...]
    merged = None
    for a_ref, wg_ref, b_ref, wp_ref in ((a0_ref, wg0_ref, b0_ref, wp0_ref),
                                         (a1_ref, wg1_ref, b1_ref, wp1_ref),
                                         (a2_ref, wg2_ref, b2_ref, wp2_ref)):
        gate = jax.nn.sigmoid(_dot(u, wg_ref[...]) + b_ref[...])
        term = gate * _dot(a_ref[...], wp_ref[...].astype(BF16))
        merged = term if merged is None else merged + term
    o_ref[...] += _dot(merged.astype(BF16), wo_ref[...])


def merge(h, g, branches, w_gate, gate_col0, b_gate, w_projs, w_out, layer, *, tm, tn):
    m, d = h.shape
    assert m % tm == 0 and d % tn == 0 and gate_col0 % tn == 0
    nj = d // tn
    g0 = gate_col0 // tn
    row = lambda a: pl.BlockSpec((tm, a.shape[1]), lambda i, j: (i, 0))
    gate_w = lambda b: _layer(layer, (d, tn), lambda i, j: (0, g0 + b * nj + j))
    gate_b = lambda b: _layer(layer, (1, tn), lambda i, j: (0, b * nj + j))
    proj_w = lambda w: _layer(layer, (w.shape[1], tn), lambda i, j: (0, j))
    return pl.pallas_call(
        _merge_kernel,
        grid=(m // tm, nj),
        in_specs=[
            row(h), _vec(layer, d),
            row(branches[0]), row(branches[1]), row(branches[2]),
            gate_w(0), gate_w(1), gate_w(2), gate_b(0), gate_b(1), gate_b(2),
            proj_w(w_projs[0]), proj_w(w_projs[1]), proj_w(w_projs[2]),
            _layer(layer, (tn, d), lambda i, j: (j, 0)),
        ],
        out_specs=pl.BlockSpec((tm, d), lambda i, j: (i, 0)),
        out_shape=jax.ShapeDtypeStruct((m, d), F32),
        scratch_shapes=[pltpu.VMEM((tm, d), BF16)],
        compiler_params=_params("arbitrary", "arbitrary"),
        name="merge",
    )(h, g, *branches, w_gate, w_gate, w_gate, b_gate, b_gate, b_gate, *w_projs, w_out)


def _xattn_kernel(h_ref, g_ref, wq_ref, k_ref, v_ref, wo_ref, o_ref):
    h = h_ref[...]
    hn = _rms(h, g_ref[...]).astype(BF16)
    q = (_dot(hn, wq_ref[...]) * (X_HEAD_DIM ** -0.5)).astype(BF16)
    heads = [slice(j * X_HEAD_DIM, (j + 1) * X_HEAD_DIM) for j in range(X_HEADS)]
    scores = [_dot_t(q[:, hs], k_ref[:, hs]) for hs in heads]
    outs = []
    for hs, s in zip(heads, scores):
        p = jnp.exp(s - jnp.max(s, axis=-1, keepdims=True))
        o = _dot(p.astype(BF16), v_ref[:, hs]) / jnp.sum(p, axis=-1, keepdims=True)
        outs.append(o.astype(BF16))
    o_ref[...] = h + _dot(jnp.concatenate(outs, axis=1), wo_ref[...])


def xattn(h, g, w_q, kv, w_o, layer, *, seq, tm):
    m, d = h.shape
    assert seq % tm == 0
    nb = seq // tm
    once = pl.Buffered(1)
    return pl.pallas_call(
        _xattn_kernel,
        grid=(m // tm,),
        in_specs=[
            pl.BlockSpec((tm, d), lambda i: (i, 0)),
            _vec(layer, d),
            pl.BlockSpec((None, d, d), lambda i: (layer, 0, 0), pipeline_mode=once),
            pl.BlockSpec((MEM_LEN, d), lambda i: (i // nb, 0)),
            pl.BlockSpec((MEM_LEN, d), lambda i: (i // nb, 1)),
            pl.BlockSpec((None, d, d), lambda i: (layer, 0, 0), pipeline_mode=once),
        ],
        out_specs=pl.BlockSpec((tm, d), lambda i: (i, 0)),
        out_shape=jax.ShapeDtypeStruct((m, d), F32),
        compiler_params=_params("arbitrary"),
        name="xattn",
    )(h, g, w_q, kv, kv, w_o)


PACK_BLOCK = 512
_IN_SPLIT_NAMES = ("c_q", "c_kv", "k_rope", "g_q", "g_k", "g_v", "g_r", "a_low", "conv", "gate")
_IN_SPLIT_WIDTHS = (MLA_RANK, MLA_RANK, MLA_ROPE, GLA_HEADS * GLA_DK, GLA_HEADS * GLA_DK, GLA_HEADS * GLA_DV,
                    GLA_HEADS * GLA_DV, GLA_RANK, 3 * CONV_DIM, 3 * D_MODEL)
_SRC = {name: sum(_IN_SPLIT_WIDTHS[:i]) for i, name in enumerate(_IN_SPLIT_NAMES)}
_PACK_SRC = (_SRC["conv"], _SRC["conv"] + PACK_BLOCK, _SRC["conv"] + 2 * PACK_BLOCK, _SRC["c_q"], _SRC["c_kv"],
             _SRC["g_v"], _SRC["g_r"], _SRC["g_q"], _SRC["k_rope"])
_PACK_SPECIAL = len(_PACK_SRC) - 1
_SRC_A_LOW = _SRC["a_low"]
_SRC_GATE = _SRC["gate"]
GATE_COL0 = len(_PACK_SRC) * PACK_BLOCK


def _pack_in_proj_kernel(src_ref, w_t_ref, a_low_ref, o_ref):
    del src_ref
    j = pl.program_id(1)

    @pl.when(j != _PACK_SPECIAL)
    def _():
        o_ref[...] = jnp.transpose(w_t_ref[0]).astype(BF16)

    @pl.when(j == _PACK_SPECIAL)
    def _():
        half = MLA_ROPE // 2
        k_rope = w_t_ref[0, 0:MLA_ROPE, :]
        row = lax.broadcasted_iota(jnp.int32, a_low_ref.shape[1:], 0)
        a_low = jnp.where(row < GLA_RANK, a_low_ref[0], 0.0)
        rest = jnp.zeros((PACK_BLOCK - 2 * LANES, k_rope.shape[1]), F32)
        blk = jnp.concatenate([k_rope, k_rope[half:], k_rope[:half], a_low, rest], axis=0)
        o_ref[...] = jnp.transpose(blk).astype(BF16)


def _pack_in_proj(w):
    depth, d, n = w.shape
    w_t = jnp.swapaxes(w, 1, 2)
    n_gate = n - _SRC_GATE
    assert n_gate % PACK_BLOCK == 0
    rows = _PACK_SRC + tuple(_SRC_GATE + PACK_BLOCK * k for k in range(n_gate // PACK_BLOCK))
    assert all(r % SUBLANES == 0 for r in rows) and _SRC_A_LOW % SUBLANES == 0
    src = jnp.asarray([r // SUBLANES for r in rows], jnp.int32)
    n_blocks = src.shape[0]
    grid_spec = pltpu.PrefetchScalarGridSpec(
        num_scalar_prefetch=1,
        grid=(depth, n_blocks),
        in_specs=[
            pl.BlockSpec((pl.Element(1), pl.Element(PACK_BLOCK), pl.Element(d)),
                         lambda l, j, src: (l, src[j] * SUBLANES, 0)),
            pl.BlockSpec((pl.Element(1), pl.Element(LANES), pl.Element(d)), lambda l, j, src: (l, _SRC_A_LOW, 0)),
        ],
        out_specs=pl.BlockSpec((None, d, PACK_BLOCK), lambda l, j, src: (l, 0, j)),
    )
    return pl.pallas_call(
        _pack_in_proj_kernel,
        grid_spec=grid_spec,
        out_shape=jax.ShapeDtypeStruct((depth, d, n_blocks * PACK_BLOCK), BF16),
        compiler_params=_params("arbitrary", "arbitrary"),
        name="pack_in_proj",
    )(src, w_t, w_t)


def _pack_uq(w):
    depth = w.shape[0]
    w = w.reshape(depth, MLA_RANK, MLA_HEADS, MLA_NOPE + MLA_ROPE)
    half = MLA_ROPE // 2
    nope, pe = w[..., :MLA_NOPE], w[..., MLA_NOPE:]
    swap = jnp.concatenate([pe[..., half:], pe[..., :half]], axis=-1)
    packed = jnp.concatenate([nope, pe, swap], axis=-1).reshape(depth, MLA_RANK, MLA_HEADS * MLA_QK_PAD)
    return jnp.swapaxes(packed, 1, 2).astype(BF16)


def _pack_ukv(w):
    depth = w.shape[0]
    w = w.reshape(depth, MLA_RANK, MLA_HEADS, MLA_NOPE + MLA_V)
    wk = w[..., :MLA_NOPE].reshape(depth, MLA_RANK, MLA_HEADS * MLA_NOPE)
    wv = w[..., MLA_NOPE:].reshape(depth, MLA_RANK, MLA_HEADS * MLA_V)
    return wk.astype(BF16), jnp.swapaxes(wv, 1, 2).astype(BF16)


def _rope_table(seq):
    inv_freq = 1.0 / (ROPE_THETA ** (jnp.arange(0, MLA_ROPE, 2, dtype=F32) / MLA_ROPE))
    ang = jnp.arange(seq, dtype=F32)[:, None] * inv_freq[None, :]
    cos, sin = jnp.cos(ang), jnp.sin(ang)
    return jnp.concatenate([cos, cos, -sin, sin], axis=1)


def kernel(x, mem, ffn1_norm, ffn1_w_in, ffn1_w_out, mix_norm, mix_w_in, mix_b_gate, mla_q_norm, mla_kv_norm,
           mla_w_uq, mla_w_ukv, mla_w_proj, gla_w_a2, gla_b_a, gla_norm, gla_w_proj, conv_w, conv_w_proj,
           mix_w_out, xattn_norm, mem_norm, xattn_w_q, xattn_w_kv, xattn_w_o, ffn2_norm, ffn2_w_in,
           ffn2_w_out, final_norm):
    batch, seq, d = x.shape
    depth = ffn1_norm.shape[0]
    tokens = batch * seq
    table = _rope_table(seq)
    h = x.reshape(tokens, d)
    mem2 = mem.reshape(batch * mem.shape[1], d)
    bf = lambda w: w.astype(BF16)
    vec = lambda p: p.reshape(depth, 1, -1)

    w_mix_in = _pack_in_proj(mix_w_in)
    wq_t = _pack_uq(mla_w_uq)
    wk, wv_t = _pack_ukv(mla_w_ukv)
    w_a2 = bf(jnp.concatenate(
        [gla_w_a2, jnp.zeros((depth, LANES - GLA_RANK, GLA_HEADS * GLA_DK), F32)], axis=1))
    w_projs = (mla_w_proj, gla_w_proj, conv_w_proj)
    w_mix_out, w_xq, w_xo = bf(mix_w_out), bf(xattn_w_q), bf(xattn_w_o)
    conv_taps = conv_w.reshape(depth, 3, CONV_DIM)

    t = TILES
    for l in range(depth):
        h = ffn(h, vec(ffn1_norm), ffn1_w_in, ffn1_w_out, final_norm, l, tm=t.ffn_rows, tf=t.ffn_cols,
                final_norm=False)

        z = norm_matmul(h, vec(mix_norm), w_mix_in, l, n=Z_WIDTH, tm=t.rows, tn=Z_WIDTH, out_dtype=F32)
        q_t, k, v_t = mla_prep(z, table, vec(mla_q_norm), vec(mla_kv_norm), wq_t, wk, wv_t, l,
                               batch=batch, seq=seq, tm=t.rows)
        a_mla = mla_attn(q_t, k, v_t, tq=t.rows, heads=t.attn_heads).reshape(tokens, MLA_HEADS * MLA_V)
        a_gla, a_conv = gla_and_conv(z, w_a2, vec(gla_b_a), vec(gla_norm), conv_taps, l,
                                     batch=batch, seq=seq, tc=t.rows)
        h = merge(h, vec(mix_norm), (a_mla, a_gla, a_conv), w_mix_in, GATE_COL0, vec(mix_b_gate), w_projs,
                  w_mix_out, l, tm=t.rows, tn=t.merge_cols)

        kv = norm_matmul(mem2, vec(mem_norm), xattn_w_kv, l, n=2 * d, tm=mem2.shape[0], tn=t.kv_cols,
                         out_dtype=BF16)
        h = xattn(h, vec(xattn_norm), w_xq, kv, w_xo, l, seq=seq, tm=t.rows)

        h = ffn(h, vec(ffn2_norm), ffn2_w_in, ffn2_w_out, final_norm, l, tm=t.ffn_rows, tf=t.ffn_cols,
                final_norm=(l == depth - 1))
    return h.reshape(batch, seq, d)
```

```python
import functools
from typing import NamedTuple

import jax
import jax.numpy as jnp
from jax import lax
from jax.experimental import pallas as pl
from jax.experimental.pallas import tpu as pltpu

F32 = jnp.float32
BF16 = jnp.bfloat16

EPS = 1e-6
D_MODEL = 2048
D_FF = 5632
MLA_HEADS = 8
MLA_RANK = 512
MLA_NOPE = 128
MLA_ROPE = 64
MLA_V = 128
MLA_QK_PAD = 256
ROPE_THETA = 10000.0
LOG2_E = 1.4426950408889634
GLA_HEADS = 4
GLA_DK = 64
GLA_DV = 128
GLA_RANK = 16
GLA_TAU = 16.0
GLA_CHUNK = 64
GLA_SUBCHUNK = 16
CONV_DIM = 512
X_HEADS = 4
X_HEAD_DIM = D_MODEL // X_HEADS
MEM_LEN = 256

LANES = 128
SUBLANES = 8
VMEM_LIMIT_BYTES = 56 * 1024 * 1024
FFN_VMEM_LIMIT_BYTES = 60 * 1024 * 1024


class Tiles(NamedTuple):
    rows: int = 512
    ffn_rows: int = 1024
    ffn_cols: int = 512
    merge_cols: int = 512
    kv_cols: int = 1024
    attn_heads: int = 8


TILES = Tiles()

Z_CONV = 0
Z_CQ = 1536
Z_CKV = 2048
Z_GV = 2560
Z_GR = 3072
Z_GQ = 3584
Z_GK = 3840
Z_KR = 4096
Z_AL = 4224
Z_WIDTH = 4352


def _params(*sem):
    return pltpu.CompilerParams(dimension_semantics=sem, vmem_limit_bytes=VMEM_LIMIT_BYTES)


def _rms(x, g):
    return x * lax.rsqrt(jnp.mean(x * x, axis=-1, keepdims=True) + EPS) * g


def _dot(a, b):
    return jnp.dot(a, b, preferred_element_type=F32)


def _dot_t(a, b):
    return lax.dot_general(a, b, (((1,), (1,)), ((), ())), preferred_element_type=F32)


def _norm_matmul_kernel(x_ref, g_ref, w_ref, o_ref):
    xn = _rms(x_ref[...], g_ref[...]).astype(BF16)
    o_ref[...] = _dot(xn, w_ref[...].astype(BF16)).astype(o_ref.dtype)


def _layer(layer, block, index):
    return pl.BlockSpec((None,) + tuple(block), lambda *ids: (layer,) + tuple(index(*ids)))


def _vec(layer, width):
    return _layer(layer, (1, width), lambda *ids: (0, 0))


def norm_matmul(x, g, w, layer, *, n, tm, tn, out_dtype):
    m, k = x.shape
    assert m % tm == 0 and n % tn == 0 and n <= w.shape[2]
    w_mode = pl.Buffered(1) if n == tn else None
    return pl.pallas_call(
        _norm_matmul_kernel,
        grid=(n // tn, m // tm),
        in_specs=[
            pl.BlockSpec((tm, k), lambda j, i: (i, 0)),
            _vec(layer, k),
            pl.BlockSpec((None, k, tn), lambda j, i: (layer, 0, j), pipeline_mode=w_mode),
        ],
        out_specs=pl.BlockSpec((tm, tn), lambda j, i: (i, j)),
        out_shape=jax.ShapeDtypeStruct((m, n), out_dtype),
        compiler_params=_params("arbitrary", "arbitrary"),
        name="norm_matmul",
    )(x, g, w)


def _ffn_kernel(x_hbm, g_ref, w_in_hbm, w_out_hbm, gf_ref, o_ref, x_buf, xn_ref, wg_buf, wu_buf, wo_buf,
                x_sem, w_sem, *, layer, tm, tf, nf, final_norm):
    i = pl.program_id(0)
    n_blocks = pl.num_programs(0)
    f = nf * tf

    def x_copy(block):
        return pltpu.make_async_copy(x_hbm.at[pl.ds(pl.multiple_of(block * tm, tm), tm), :], x_buf, x_sem)

    def chunk_copies(j, slot):
        c0 = pl.multiple_of(j * tf, tf)
        return (
            pltpu.make_async_copy(w_in_hbm.at[layer, :, pl.ds(c0, tf)], wg_buf.at[slot], w_sem.at[0, slot]),
            pltpu.make_async_copy(w_in_hbm.at[layer, :, pl.ds(f + c0, tf)], wu_buf.at[slot], w_sem.at[1, slot]),
            pltpu.make_async_copy(w_out_hbm.at[layer, pl.ds(c0, tf), :], wo_buf.at[slot], w_sem.at[2, slot]),
        )

    def start(j, slot):
        for copy in chunk_copies(j, slot):
            copy.start()

    def wait(j, slot):
        for copy in chunk_copies(j, slot):
            copy.wait()

    @pl.when(i == 0)
    def _():
        x_copy(0).start()
        start(0, 0)

    x_copy(i).wait()
    x = x_buf[...]
    xn_ref[...] = _rms(x, g_ref[...]).astype(BF16)
    o_ref[...] = x

    @pl.when(i + 1 < n_blocks)
    def _():
        x_copy(i + 1).start()

    def chunk(j, carry):
        slot = lax.rem(i * nf + j, 2)
        wait(j, slot)

        @pl.when(j + 1 < nf)
        def _():
            start(j + 1, 1 - slot)

        @pl.when(jnp.logical_and(j + 1 == nf, i + 1 < n_blocks))
        def _():
            start(0, 1 - slot)

        xn = xn_ref[...]
        gate = _dot(xn, wg_buf[slot].astype(BF16))
        up = _dot(xn, wu_buf[slot].astype(BF16))
        act = (0.5 * gate) * jax.nn.sigmoid(gate) * up
        o_ref[...] += _dot(act.astype(BF16), wo_buf[slot].astype(BF16))
        return carry

    lax.fori_loop(0, nf, chunk, 0)

    if final_norm:
        o_ref[...] = _rms(o_ref[...], gf_ref[...])


def ffn(x, g, w_in, w_out, g_final, layer, *, tm, tf, final_norm):
    m, d = x.shape
    f = w_out.shape[1]
    assert m % tm == 0 and f % tf == 0
    nf = f // tf
    return pl.pallas_call(
        functools.partial(_ffn_kernel, layer=layer, tm=tm, tf=tf, nf=nf, final_norm=final_norm),
        grid=(m // tm,),
        in_specs=[
            pl.BlockSpec(memory_space=pl.ANY),
            _vec(layer, d),
            pl.BlockSpec(memory_space=pl.ANY),
            pl.BlockSpec(memory_space=pl.ANY),
            pl.BlockSpec((1, d), lambda i: (0, 0)),
        ],
        out_specs=pl.BlockSpec((tm, d), lambda i: (i, 0)),
        out_shape=jax.ShapeDtypeStruct((m, d), F32),
        scratch_shapes=[
            pltpu.VMEM((tm, d), F32), pltpu.VMEM((tm, d), BF16),
            pltpu.VMEM((2, d, tf), F32), pltpu.VMEM((2, d, tf), F32), pltpu.VMEM((2, tf, d), F32),
            pltpu.SemaphoreType.DMA(()), pltpu.SemaphoreType.DMA((3, 2)),
        ],
        compiler_params=pltpu.CompilerParams(dimension_semantics=("arbitrary",),
                                             vmem_limit_bytes=FFN_VMEM_LIMIT_BYTES),
        name="ffn",
    )(x, g, w_in, w_out, g_final.reshape(1, d))


def _rope_pairs(x, table):
    y = x * table
    return y + pltpu.roll(y, MLA_ROPE, 1)


def _mla_prep_kernel(cq_ref, ckv_ref, kr_ref, tab_ref, tab_t_ref, gq_ref, gkv_ref, wq_t_ref, wk_ref, wv_t_ref,
                     q_t_ref, k_ref, v_t_ref):
    scale = (MLA_NOPE + MLA_ROPE) ** -0.5 * LOG2_E
    nq_t = jnp.transpose(_rms(cq_ref[...], gq_ref[...])).astype(BF16)
    q_t = _dot(wq_t_ref[...], nq_t) * scale
    nkv = _rms(ckv_ref[...], gkv_ref[...])
    kn = _dot(nkv.astype(BF16), wk_ref[...])
    v_t = _dot(wv_t_ref[...], jnp.transpose(nkv).astype(BF16))
    tab = tab_ref[...]
    tab_t = tab_t_ref[...]
    lane = lax.broadcasted_iota(jnp.int32, tab.shape, 1)
    k_pe = jnp.where(lane < MLA_ROPE, _rope_pairs(kr_ref[...], tab), 0.0).astype(BF16)
    zeros = jnp.zeros((MLA_QK_PAD - MLA_NOPE - MLA_ROPE, q_t.shape[1]), BF16)
    for h in range(MLA_HEADS):
        q0 = h * MLA_QK_PAD
        q_t_ref[0, h, 0:MLA_NOPE, :] = q_t[q0:q0 + MLA_NOPE].astype(BF16)
        pe = q_t[q0 + MLA_NOPE:q0 + MLA_QK_PAD] * tab_t
        q_t_ref[0, h, MLA_NOPE:MLA_NOPE + MLA_ROPE, :] = (pe[0:MLA_ROPE] + pe[MLA_ROPE:2 * MLA_ROPE]).astype(BF16)
        q_t_ref[0, h, MLA_NOPE + MLA_ROPE:MLA_QK_PAD, :] = zeros
        k_ref[0, h, :, 0:LANES] = kn[:, h * LANES:(h + 1) * LANES].astype(BF16)
        k_ref[0, h, :, LANES:2 * LANES] = k_pe
        v_t_ref[0, h] = v_t[h * MLA_V:(h + 1) * MLA_V].astype(BF16)


def mla_prep(z, table, g_q, g_kv, wq_t, wk, wv_t, layer, *, batch, seq, tm):
    assert seq % tm == 0
    nb = seq // tm
    r = MLA_RANK
    hw = MLA_HEADS
    tok = lambda width, col: pl.BlockSpec((tm, width), lambda i: (i, col // width))
    full = lambda a: _layer(layer, a.shape[1:], lambda i: (0, 0))
    feat_major = lambda width: pl.BlockSpec((1, hw, width, tm), lambda i: (i // nb, 0, 0, i % nb))
    return pl.pallas_call(
        _mla_prep_kernel,
        grid=(batch * nb,),
        in_specs=[
            tok(r, Z_CQ), tok(r, Z_CKV), tok(LANES, Z_KR),
            pl.BlockSpec((tm, LANES), lambda i: (i % nb, 0)),
            pl.BlockSpec((LANES, tm), lambda i: (0, i % nb)),
            full(g_q), full(g_kv), full(wq_t), full(wk), full(wv_t),
        ],
        out_specs=[
            feat_major(MLA_QK_PAD),
            pl.BlockSpec((1, hw, tm, MLA_QK_PAD), lambda i: (i // nb, 0, i % nb, 0)),
            feat_major(MLA_V),
        ],
        out_shape=[
            jax.ShapeDtypeStruct((batch, hw, MLA_QK_PAD, seq), BF16),
            jax.ShapeDtypeStruct((batch, hw, seq, MLA_QK_PAD), BF16),
            jax.ShapeDtypeStruct((batch, hw, MLA_V, seq), BF16),
        ],
        compiler_params=_params("arbitrary"),
        name="mla_prep",
    )(z, z, z, table, jnp.transpose(table), g_q, g_kv, wq_t, wk, wv_t)


def _mla_attn_kernel(q_t_ref, k_ref, v_t_ref, o_ref, m_ref, l_ref, acc_ref, *, tq, heads):
    qi = pl.program_id(2)
    m_ref[...] = jnp.full(m_ref.shape, -jnp.inf, F32)
    l_ref[...] = jnp.zeros(l_ref.shape, F32)
    acc_ref[...] = jnp.zeros(acc_ref.shape, F32)

    def step(j, masked):
        start = pl.multiple_of(j * tq, tq)
        scores = [_dot(k_ref[0, g, pl.ds(start, tq), :], q_t_ref[0, g]) for g in range(heads)]
        for g in range(heads):
            s = scores[g]
            if masked:
                key = lax.broadcasted_iota(jnp.int32, s.shape, 0)
                qry = lax.broadcasted_iota(jnp.int32, s.shape, 1)
                s = jnp.where(key <= qry, s, -jnp.inf)
            m_old = m_ref[g]
            m_new = jnp.maximum(m_old, jnp.max(s, axis=0, keepdims=True))
            p = jnp.exp2(s - m_new)
            alpha = jnp.exp2(m_old - m_new)
            l_ref[g] = alpha * l_ref[g] + jnp.sum(p, axis=0, keepdims=True)
            acc_ref[g] = alpha * acc_ref[g] + _dot(v_t_ref[0, g, :, pl.ds(start, tq)], p.astype(BF16))
            m_ref[g] = m_new

    def body(j, carry):
        step(j, False)
        return carry

    lax.fori_loop(0, qi, body, 0)
    step(qi, True)
    for g in range(heads):
        o_t = acc_ref[g] / l_ref[g]
        o_ref[0, :, g * MLA_V:(g + 1) * MLA_V] = jnp.transpose(o_t).astype(o_ref.dtype)


def mla_attn(q_t, k, v_t, *, tq, heads):
    b, h, s, dq = k.shape
    dv = v_t.shape[2]
    assert s % tq == 0 and h % heads == 0
    return pl.pallas_call(
        functools.partial(_mla_attn_kernel, tq=tq, heads=heads),
        grid=(b, h // heads, s // tq),
        in_specs=[
            pl.BlockSpec((1, heads, dq, tq), lambda bi, hi, qi: (bi, hi, 0, qi)),
            pl.BlockSpec((1, heads, s, dq), lambda bi, hi, qi: (bi, hi, 0, 0), pipeline_mode=pl.Buffered(1)),
            pl.BlockSpec((1, heads, dv, s), lambda bi, hi, qi: (bi, hi, 0, 0), pipeline_mode=pl.Buffered(1)),
        ],
        out_specs=pl.BlockSpec((1, tq, heads * dv), lambda bi, hi, qi: (bi, qi, hi)),
        out_shape=jax.ShapeDtypeStruct((b, s, h * dv), BF16),
        scratch_shapes=[pltpu.VMEM((heads, 1, tq), F32), pltpu.VMEM((heads, 1, tq), F32),
                        pltpu.VMEM((heads, dv, tq), F32)],
        compiler_params=_params("arbitrary", "arbitrary", "arbitrary"),
        name="mla_attn",
    )(q_t, k, v_t)


def _gla_kernel(q_ref, k_ref, v_ref, r_ref, al_ref, cx_ref, halo_ref, wa_ref, ba_ref, ng_ref, taps_ref,
                o_ref, oc_ref, state_ref, b_scr, q_scr, k_scr, v_scr, oi_scr, *, tc):
    c = GLA_CHUNK
    pair_w = 2 * GLA_DK
    pair_v = 2 * GLA_DV
    n_pairs = GLA_HEADS // 2
    sequence_start = pl.program_id(1) == 0

    @pl.when(sequence_start)
    def _():
        state_ref[...] = jnp.zeros(state_ref.shape, F32)

    oc_ref[...] = _short_conv(cx_ref[...], halo_ref[...], taps_ref[...], sequence_start).astype(oc_ref.dtype)

    row_i = lax.broadcasted_iota(jnp.int32, (c, c), 0)
    col_i = lax.broadcasted_iota(jnp.int32, (c, c), 1)
    tri = jnp.where(row_i >= col_i, 1.0, 0.0).astype(BF16)
    lane_k = lax.broadcasted_iota(jnp.int32, (pair_w, LANES), 0)
    head_sum = [jnp.where((lane_k // GLA_DK) == hh, 1.0, 0.0).astype(BF16) for hh in range(2)]
    sub = GLA_SUBCHUNK
    key_row = lax.broadcasted_iota(jnp.int32, (c, pair_w), 0)
    lane_h = lax.broadcasted_iota(jnp.int32, (1, pair_w), 1) // GLA_DK
    head_lanes = [jnp.where(lane_h == hh, 1.0, 0.0) for hh in range(2)]
    srow = lax.broadcasted_iota(jnp.int32, (pair_w, pair_v), 0) // GLA_DK
    scol = lax.broadcasted_iota(jnp.int32, (pair_w, pair_v), 1) // GLA_DV
    own_head = srow == scol

    def chunk(ci, carry):
        r0 = pl.multiple_of(ci * c, c)
        rows = pl.ds(r0, c)
        x = _dot(al_ref[rows, :].astype(BF16), wa_ref[...]) + ba_ref[...]
        log_a = (jnp.minimum(x, 0.0) - jnp.log(1.0 + jnp.exp(-jnp.abs(x)))) * (1.0 / GLA_TAU)
        p1 = log_a.astype(BF16)
        r1 = log_a - p1.astype(F32)
        p2 = r1.astype(BF16)
        p3 = (r1 - p2.astype(F32)).astype(BF16)
        b_scr[...] = _dot(tri, p1) + _dot(tri, p2) + _dot(tri, p3)
        q_scr[...] = q_ref[rows, :] * (GLA_DK ** -0.5)
        k_scr[...] = k_ref[rows, :]
        v_scr[...] = v_ref[rows, :]

        staged = []
        for p in range(n_pairs):
            kl = slice(p * pair_w, (p + 1) * pair_w)
            vl = slice(p * pair_v, (p + 1) * pair_v)
            bp = b_scr[:, kl]
            qp = q_scr[:, kl]
            kp = k_scr[:, kl]
            vp = v_scr[:, vl]
            b_last = b_scr[c - 1:c, kl]
            st = state_ref[p]
            q_dec = (qp * jnp.exp(bp)).astype(BF16)
            k_dec = (kp * jnp.exp(b_last - bp)).astype(BF16)
            o_inter = _dot(q_dec, st.astype(BF16))
            vp_bf = vp.astype(BF16)
            kv = lax.dot_general(k_dec, vp_bf, (((0,), (0,)), ((), ())), preferred_element_type=F32)
            decay_col = jnp.transpose(jnp.broadcast_to(jnp.exp(b_last), (LANES, pair_w)))
            decay_col = jnp.concatenate([decay_col, decay_col], axis=1)
            state_ref[p] = decay_col * st + jnp.where(own_head, kv, 0.0)
            q_blocks, k_blocks = [], []
            for i in range(1, c // sub):
                r = i * sub
                b_r = b_scr[r:r + 1, kl]
                q_i = q_scr[r:r + sub, kl] * jnp.exp(b_scr[r:r + sub, kl] - b_r)
                pieces = [jnp.zeros((r, pair_w), F32), q_i]
                if c - r - sub:
                    pieces.append(jnp.zeros((c - r - sub, pair_w), F32))
                q_blocks.append(jnp.concatenate(pieces, axis=0))
                k_blocks.append(kp * jnp.exp(jnp.where(key_row < r, b_r - bp, -jnp.inf)))
            q_cat = jnp.concatenate(q_blocks, axis=1)
            k_cat = jnp.concatenate(k_blocks, axis=1).astype(BF16)
            a_far = [_dot_t((q_cat * jnp.concatenate([head_lanes[hh]] * len(q_blocks), axis=1)).astype(BF16), k_cat)
                     for hh in range(2)]
            ys = []
            for i in range(c // sub):
                r = i * sub
                for t0 in (r, r + 8):
                    n = r + sub - t0
                    bt = b_scr[t0:r + sub, kl]
                    qt = q_scr[t0:r + sub, kl]
                    t_idx = t0 + lax.broadcasted_iota(jnp.int32, (n, pair_w), 0)
                    for s in range(t0, t0 + 8):
                        e = jnp.exp(jnp.where(t_idx >= s, bt - b_scr[s:s + 1, kl], -jnp.inf))
                        ys.append(qt * e * k_scr[s:s + 1, kl])
            y = jnp.concatenate(ys, axis=0).astype(BF16)
            a_near = [_dot(y, head_sum[hh]) for hh in range(2)]
            staged.append((o_inter, vp_bf, a_far, a_near))

        for p in range(n_pairs):
            o_inter, vp_bf, a_far, a_near = staged[p]
            for hh in range(2):
                hv = slice(p * pair_v + hh * GLA_DV, p * pair_v + (hh + 1) * GLA_DV)
                o_h = o_inter[:, hh * GLA_DV:(hh + 1) * GLA_DV] + _dot(a_far[hh].astype(BF16),
                                                                       vp_bf[:, hh * GLA_DV:(hh + 1) * GLA_DV])
                a = a_near[hh]
                for i in range(c // sub):
                    r = i * sub
                    base = i * (8 * sub + 8 * 8)
                    upd = a[base:base + sub] * v_scr[r:r + 1, hv]
                    for idx in range(1, 8):
                        upd += a[base + idx * sub:base + (idx + 1) * sub] * v_scr[r + idx:r + idx + 1, hv]
                    base += 8 * sub
                    upd8 = a[base:base + 8] * v_scr[r + 8:r + 9, hv]
                    for idx in range(1, 8):
                        upd8 += a[base + idx * 8:base + (idx + 1) * 8] * v_scr[r + 8 + idx:r + 9 + idx, hv]
                    upd = jnp.concatenate([upd[0:8], upd[8:sub] + upd8], axis=0)
                    oi_scr[r:r + sub, hv] = upd + o_h[r:r + sub]

        o = oi_scr[...]
        gate = r_ref[rows, :]
        gate = gate * jax.nn.sigmoid(gate)
        for h in range(GLA_HEADS):
            hv = slice(h * GLA_DV, (h + 1) * GLA_DV)
            oh = o[:, hv]
            oh = oh * lax.rsqrt(jnp.mean(oh * oh, axis=-1, keepdims=True) + EPS)
            o_ref[rows, hv] = (oh * ng_ref[:, hv] * gate[:, hv]).astype(o_ref.dtype)
        return carry

    lax.fori_loop(0, tc // c, chunk, 0, unroll=4)


def gla_and_conv(z, w_a2, b_a, norm_g, conv_taps, layer, *, batch, seq, tc):
    assert seq % tc == 0 and tc % GLA_CHUNK == 0
    nb = seq // tc
    kw = GLA_HEADS * GLA_DK
    vw = GLA_HEADS * GLA_DV
    cw = 3 * CONV_DIM
    tok = lambda width, col: pl.BlockSpec((tc, width), lambda b, i: (b * nb + i, col // width))
    full = lambda a: _layer(layer, a.shape[1:], lambda b, i: (0, 0))
    c = GLA_CHUNK
    return pl.pallas_call(
        functools.partial(_gla_kernel, tc=tc),
        grid=(batch, nb),
        in_specs=[
            tok(kw, Z_GQ), tok(kw, Z_GK), tok(vw, Z_GV), tok(vw, Z_GR), tok(LANES, Z_AL), tok(cw, Z_CONV),
            pl.BlockSpec((SUBLANES, cw),
                         lambda b, i: (jnp.maximum((b * nb + i) * (tc // SUBLANES) - 1, 0), Z_CONV // cw)),
            full(w_a2), full(b_a), full(norm_g), full(conv_taps),
        ],
        out_specs=[pl.BlockSpec((tc, vw), lambda b, i: (b * nb + i, 0)),
                   pl.BlockSpec((tc, CONV_DIM), lambda b, i: (b * nb + i, 0))],
        out_shape=[jax.ShapeDtypeStruct((batch * seq, vw), BF16),
                   jax.ShapeDtypeStruct((batch * seq, CONV_DIM), BF16)],
        scratch_shapes=[
            pltpu.VMEM((GLA_HEADS // 2, 2 * GLA_DK, 2 * GLA_DV), F32),
            pltpu.VMEM((c, kw), F32), pltpu.VMEM((c, kw), F32), pltpu.VMEM((c, kw), F32),
            pltpu.VMEM((c, vw), F32), pltpu.VMEM((c, vw), F32),
        ],
        compiler_params=_params("arbitrary", "arbitrary"),
        name="gla",
    )(z, z, z, z, z, z, z, w_a2, b_a, norm_g, conv_taps)


def _short_conv(x, halo, w, sequence_start):
    d = CONV_DIM
    z = x[:, d:2 * d] * x[:, 2 * d:3 * d]
    hz = halo[:, d:2 * d] * halo[:, 2 * d:3 * d]
    hz = jnp.where(sequence_start, 0.0, hz)
    row = lax.broadcasted_iota(jnp.int32, z.shape, 0)
    z1 = jnp.where(row == 0, hz[7:8], pltpu.roll(z, 1, 0))
    z2 = jnp.where(row == 0, hz[6:7], jnp.where(row == 1, hz[7:8], pltpu.roll(z, 2, 0)))
    y = w[2:3, :] * z + w[1:2, :] * z1 + w[0:1, :] * z2
    return x[:, 0:d] * y


def _merge_kernel(h_ref, g_ref, a0_ref, a1_ref, a2_ref, wg0_ref, wg1_ref, wg2_ref, b0_ref, b1_ref, b2_ref,
                  wp0_ref, wp1_ref, wp2_ref, wo_ref, o_ref, u_ref):
    j = pl.program_id(1)

    @pl.when(j == 0)
    def _():
        h = h_ref[...]
        u_ref[...] = _rms(h, g_ref[...]).astype(BF16)
        o_ref[...] = h

    u = u_ref[...]
    merged = None
    for a_ref, wg_ref, b_ref, wp_ref in ((a0_ref, wg0_ref, b0_ref, wp0_ref),
                                         (a1_ref, wg1_ref, b1_ref, wp1_ref),
                                         (a2_ref, wg2_ref, b2_ref, wp2_ref)):
        gate = jax.nn.sigmoid(_dot(u, wg_ref[...]) + b_ref[...])
        term = gate * _dot(a_ref[...], wp_ref[...])
        merged = term if merged is None else merged + term
    o_ref[...] += _dot(merged.astype(BF16), wo_ref[...])


def merge(h, g, branches, w_gate, gate_col0, b_gate, w_projs, w_out, layer, *, tm, tn):
    m, d = h.shape
    assert m % tm == 0 and d % tn == 0 and gate_col0 % tn == 0
    nj = d // tn
    g0 = gate_col0 // tn
    row = lambda a: pl.BlockSpec((tm, a.shape[1]), lambda i, j: (i, 0))
    gate_w = lambda b: _layer(layer, (d, tn), lambda i, j: (0, g0 + b * nj + j))
    gate_b = lambda b: _layer(layer, (1, tn), lambda i, j: (0, b * nj + j))
    proj_w = lambda w: _layer(layer, (w.shape[1], tn), lambda i, j: (0, j))
    return pl.pallas_call(
        _merge_kernel,
        grid=(m // tm, nj),
        in_specs=[
            row(h), _vec(layer, d),
            row(branches[0]), row(branches[1]), row(branches[2]),
            gate_w(0), gate_w(1), gate_w(2), gate_b(0), gate_b(1), gate_b(2),
            proj_w(w_projs[0]), proj_w(w_projs[1]), proj_w(w_projs[2]),
            _layer(layer, (tn, d), lambda i, j: (j, 0)),
        ],
        out_specs=pl.BlockSpec((tm, d), lambda i, j: (i, 0)),
        out_shape=jax.ShapeDtypeStruct((m, d), F32),
        scratch_shapes=[pltpu.VMEM((tm, d), BF16)],
        compiler_params=_params("arbitrary", "arbitrary"),
        name="merge",
    )(h, g, *branches, w_gate, w_gate, w_gate, b_gate, b_gate, b_gate, *w_projs, w_out)


def _xattn_kernel(h_ref, g_ref, wq_ref, k_ref, v_ref, wo_ref, o_ref):
    h = h_ref[...]
    hn = _rms(h, g_ref[...]).astype(BF16)
    q = (_dot(hn, wq_ref[...]) * (X_HEAD_DIM ** -0.5)).astype(BF16)
    heads = [slice(j * X_HEAD_DIM, (j + 1) * X_HEAD_DIM) for j in range(X_HEADS)]
    scores = [_dot_t(q[:, hs], k_ref[:, hs]) for hs in heads]
    outs = []
    for hs, s in zip(heads, scores):
        p = jnp.exp(s - jnp.max(s, axis=-1, keepdims=True))
        o = _dot(p.astype(BF16), v_ref[:, hs]) / jnp.sum(p, axis=-1, keepdims=True)
        outs.append(o.astype(BF16))
    o_ref[...] = h + _dot(jnp.concatenate(outs, axis=1), wo_ref[...])


def xattn(h, g, w_q, kv, w_o, layer, *, seq, tm):
    m, d = h.shape
    assert seq % tm == 0
    nb = seq // tm
    once = pl.Buffered(1)
    return pl.pallas_call(
        _xattn_kernel,
        grid=(m // tm,),
        in_specs=[
            pl.BlockSpec((tm, d), lambda i: (i, 0)),
            _vec(layer, d),
            pl.BlockSpec((None, d, d), lambda i: (layer, 0, 0), pipeline_mode=once),
            pl.BlockSpec((MEM_LEN, d), lambda i: (i // nb, 0)),
            pl.BlockSpec((MEM_LEN, d), lambda i: (i // nb, 1)),
            pl.BlockSpec((None, d, d), lambda i: (layer, 0, 0), pipeline_mode=once),
        ],
        out_specs=pl.BlockSpec((tm, d), lambda i: (i, 0)),
        out_shape=jax.ShapeDtypeStruct((m, d), F32),
        compiler_params=_params("arbitrary"),
        name="xattn",
    )(h, g, w_q, kv, kv, w_o)


PACK_BLOCK = 512
_IN_SPLIT_NAMES = ("c_q", "c_kv", "k_rope", "g_q", "g_k", "g_v", "g_r", "a_low", "conv", "gate")
_IN_SPLIT_WIDTHS = (MLA_RANK, MLA_RANK, MLA_ROPE, GLA_HEADS * GLA_DK, GLA_HEADS * GLA_DK, GLA_HEADS * GLA_DV,
                    GLA_HEADS * GLA_DV, GLA_RANK, 3 * CONV_DIM, 3 * D_MODEL)
_SRC = {name: sum(_IN_SPLIT_WIDTHS[:i]) for i, name in enumerate(_IN_SPLIT_NAMES)}
_PACK_SRC = (_SRC["conv"], _SRC["conv"] + PACK_BLOCK, _SRC["conv"] + 2 * PACK_BLOCK, _SRC["c_q"], _SRC["c_kv"],
             _SRC["g_v"], _SRC["g_r"], _SRC["g_q"], _SRC["k_rope"])
_PACK_SPECIAL = len(_PACK_SRC) - 1
_SRC_A_LOW = _SRC["a_low"]
_SRC_GATE = _SRC["gate"]
GATE_COL0 = len(_PACK_SRC) * PACK_BLOCK


def _pack_in_proj_kernel(src_ref, w_t_ref, a_low_ref, o_ref):
    del src_ref
    j = pl.program_id(1)

    @pl.when(j != _PACK_SPECIAL)
    def _():
        o_ref[...] = jnp.transpose(w_t_ref[0]).astype(BF16)

    @pl.when(j == _PACK_SPECIAL)
    def _():
        half = MLA_ROPE // 2
        k_rope = w_t_ref[0, 0:MLA_ROPE, :]
        row = lax.broadcasted_iota(jnp.int32, a_low_ref.shape[1:], 0)
        a_low = jnp.where(row < GLA_RANK, a_low_ref[0], 0.0)
        rest = jnp.zeros((PACK_BLOCK - 2 * LANES, k_rope.shape[1]), F32)
        blk = jnp.concatenate([k_rope, k_rope[half:], k_rope[:half], a_low, rest], axis=0)
        o_ref[...] = jnp.transpose(blk).astype(BF16)


def _pack_in_proj(w):
    depth, d, n = w.shape
    w_t = jnp.swapaxes(w, 1, 2)
    n_gate = n - _SRC_GATE
    assert n_gate % PACK_BLOCK == 0
    rows = _PACK_SRC + tuple(_SRC_GATE + PACK_BLOCK * k for k in range(n_gate // PACK_BLOCK))
    assert all(r % SUBLANES == 0 for r in rows) and _SRC_A_LOW % SUBLANES == 0
    src = jnp.asarray([r // SUBLANES for r in rows], jnp.int32)
    n_blocks = src.shape[0]
    grid_spec = pltpu.PrefetchScalarGridSpec(
        num_scalar_prefetch=1,
        grid=(depth, n_blocks),
        in_specs=[
            pl.BlockSpec((pl.Element(1), pl.Element(PACK_BLOCK), pl.Element(d)),
                         lambda l, j, src: (l, src[j] * SUBLANES, 0)),
            pl.BlockSpec((pl.Element(1), pl.Element(LANES), pl.Element(d)), lambda l, j, src: (l, _SRC_A_LOW, 0)),
        ],
        out_specs=pl.BlockSpec((None, d, PACK_BLOCK), lambda l, j, src: (l, 0, j)),
    )
    return pl.pallas_call(
        _pack_in_proj_kernel,
        grid_spec=grid_spec,
        out_shape=jax.ShapeDtypeStruct((depth, d, n_blocks * PACK_BLOCK), BF16),
        compiler_params=_params("arbitrary", "arbitrary"),
        name="pack_in_proj",
    )(src, w_t, w_t)


def _pack_uq(w):
    depth = w.shape[0]
    w = w.reshape(depth, MLA_RANK, MLA_HEADS, MLA_NOPE + MLA_ROPE)
    half = MLA_ROPE // 2
    nope, pe = w[..., :MLA_NOPE], w[..., MLA_NOPE:]
    swap = jnp.concatenate([pe[..., half:], pe[..., :half]], axis=-1)
    packed = jnp.concatenate([nope, pe, swap], axis=-1).reshape(depth, MLA_RANK, MLA_HEADS * MLA_QK_PAD)
    return jnp.swapaxes(packed, 1, 2).astype(BF16)


def _pack_ukv(w):
    depth = w.shape[0]
    w = w.reshape(depth, MLA_RANK, MLA_HEADS, MLA_NOPE + MLA_V)
    wk = w[..., :MLA_NOPE].reshape(depth, MLA_RANK, MLA_HEADS * MLA_NOPE)
    wv = w[..., MLA_NOPE:].reshape(depth, MLA_RANK, MLA_HEADS * MLA_V)
    return wk.astype(BF16), jnp.swapaxes(wv, 1, 2).astype(BF16)


def _rope_table(seq):
    inv_freq = 1.0 / (ROPE_THETA ** (jnp.arange(0, MLA_ROPE, 2, dtype=F32) / MLA_ROPE))
    ang = jnp.arange(seq, dtype=F32)[:, None] * inv_freq[None, :]
    cos, sin = jnp.cos(ang), jnp.sin(ang)
    return jnp.concatenate([cos, cos, -sin, sin], axis=1)


def kernel(x, mem, ffn1_norm, ffn1_w_in, ffn1_w_out, mix_norm, mix_w_in, mix_b_gate, mla_q_norm, mla_kv_norm,
           mla_w_uq, mla_w_ukv, mla_w_proj, gla_w_a2, gla_b_a, gla_norm, gla_w_proj, conv_w, conv_w_proj,
           mix_w_out, xattn_norm, mem_norm, xattn_w_q, xattn_w_kv, xattn_w_o, ffn2_norm, ffn2_w_in,
           ffn2_w_out, final_norm):
    batch, seq, d = x.shape
    depth = ffn1_norm.shape[0]
    tokens = batch * seq
    table = _rope_table(seq)
    h = x.reshape(tokens, d)
    mem2 = mem.reshape(batch * mem.shape[1], d)
    bf = lambda w: w.astype(BF16)
    vec = lambda p: p.reshape(depth, 1, -1)

    w_mix_in = _pack_in_proj(mix_w_in)
    wq_t = _pack_uq(mla_w_uq)
    wk, wv_t = _pack_ukv(mla_w_ukv)
    w_a2 = bf(jnp.concatenate(
        [gla_w_a2, jnp.zeros((depth, LANES - GLA_RANK, GLA_HEADS * GLA_DK), F32)], axis=1))
    w_projs = (bf(mla_w_proj), bf(gla_w_proj), bf(conv_w_proj))
    w_mix_out, w_xq, w_xo = bf(mix_w_out), bf(xattn_w_q), bf(xattn_w_o)
    conv_taps = conv_w.reshape(depth, 3, CONV_DIM)

    t = TILES
    for l in range(depth):
        h = ffn(h, vec(ffn1_norm), ffn1_w_in, ffn1_w_out, final_norm, l, tm=t.ffn_rows, tf=t.ffn_cols,
                final_norm=False)

        z = norm_matmul(h, vec(mix_norm), w_mix_in, l, n=Z_WIDTH, tm=t.rows, tn=Z_WIDTH, out_dtype=F32)
        q_t, k, v_t = mla_prep(z, table, vec(mla_q_norm), vec(mla_kv_norm), wq_t, wk, wv_t, l,
                               batch=batch, seq=seq, tm=t.rows)
        a_mla = mla_attn(q_t, k, v_t, tq=t.rows, heads=t.attn_heads).reshape(tokens, MLA_HEADS * MLA_V)
        a_gla, a_conv = gla_and_conv(z, w_a2, vec(gla_b_a), vec(gla_norm), conv_taps, l,
                                     batch=batch, seq=seq, tc=t.rows)
        h = merge(h, vec(mix_norm), (a_mla, a_gla, a_conv), w_mix_in, GATE_COL0, vec(mix_b_gate), w_projs,
                  w_mix_out, l, tm=t.rows, tn=t.merge_cols)

        kv = norm_matmul(mem2, vec(mem_norm), xattn_w_kv, l, n=2 * d, tm=mem2.shape[0], tn=t.kv_cols,
                         out_dtype=BF16)
        h = xattn(h, vec(xattn_norm), w_xq, kv, w_xo, l, seq=seq, tm=t.rows)

        h = ffn(h, vec(ffn2_norm), ffn2_w_in, ffn2_w_out, final_norm, l, tm=t.ffn_rows, tf=t.ffn_cols,
                final_norm=(l == depth - 1))
    return h.reshape(batch, seq, d)
```

```python
import functools
from typing import NamedTuple

import jax
import jax.numpy as jnp
from jax import lax
from jax.experimental import pallas as pl
from jax.experimental.pallas import tpu as pltpu

F32 = jnp.float32
BF16 = jnp.bfloat16

EPS = 1e-6
D_MODEL = 2048
D_FF = 5632
MLA_HEADS = 8
MLA_RANK = 512
MLA_NOPE = 128
MLA_ROPE = 64
MLA_V = 128
MLA_QK_PAD = 256
ROPE_THETA = 10000.0
LOG2_E = 1.4426950408889634
MLA_V_ROWS = MLA_V + 16
GLA_HEADS = 4
GLA_DK = 64
GLA_DV = 128
GLA_RANK = 16
GLA_TAU = 16.0
GLA_CHUNK = 64
GLA_SUBCHUNK = 16
CONV_DIM = 512
X_HEADS = 4
X_HEAD_DIM = D_MODEL // X_HEADS
MEM_LEN = 256

LANES = 128
SUBLANES = 8
VMEM_LIMIT_BYTES = 56 * 1024 * 1024
FFN_VMEM_LIMIT_BYTES = 60 * 1024 * 1024


class Tiles(NamedTuple):
    rows: int = 512
    ffn_rows: int = 1024
    ffn_cols: int = 512
    merge_cols: int = 512
    kv_cols: int = 1024
    attn_heads: int = 8


TILES = Tiles()

Z_CONV = 0
Z_CQ = 1536
Z_CKV = 2048
Z_GV = 2560
Z_GR = 3072
Z_GQ = 3584
Z_GK = 3840
Z_KR = 4096
Z_AL = 4224
Z_WIDTH = 4352


def _params(*sem):
    return pltpu.CompilerParams(dimension_semantics=sem, vmem_limit_bytes=VMEM_LIMIT_BYTES)


def _rms(x, g):
    return x * lax.rsqrt(jnp.mean(x * x, axis=-1, keepdims=True) + EPS) * g


def _dot(a, b):
    return jnp.dot(a, b, preferred_element_type=F32)


def _dot_t(a, b):
    return lax.dot_general(a, b, (((1,), (1,)), ((), ())), preferred_element_type=F32)


def _norm_matmul_kernel(x_ref, g_ref, w_ref, o_ref):
    xn = _rms(x_ref[...], g_ref[...]).astype(BF16)
    o_ref[...] = _dot(xn, w_ref[...].astype(BF16)).astype(o_ref.dtype)


def _layer(layer, block, index):
    return pl.BlockSpec((None,) + tuple(block), lambda *ids: (layer,) + tuple(index(*ids)))


def _vec(layer, width):
    return _layer(layer, (1, width), lambda *ids: (0, 0))


def norm_matmul(x, g, w, layer, *, n, tm, tn, out_dtype):
    m, k = x.shape
    assert m % tm == 0 and n % tn == 0 and n <= w.shape[2]
    w_mode = pl.Buffered(1) if n == tn else None
    return pl.pallas_call(
        _norm_matmul_kernel,
        grid=(n // tn, m // tm),
        in_specs=[
            pl.BlockSpec((tm, k), lambda j, i: (i, 0)),
            _vec(layer, k),
            pl.BlockSpec((None, k, tn), lambda j, i: (layer, 0, j), pipeline_mode=w_mode),
        ],
        out_specs=pl.BlockSpec((tm, tn), lambda j, i: (i, j)),
        out_shape=jax.ShapeDtypeStruct((m, n), out_dtype),
        compiler_params=_params("arbitrary", "arbitrary"),
        name="norm_matmul",
    )(x, g, w)


def _ffn_kernel(x_hbm, g_ref, w_in_hbm, w_out_hbm, gf_ref, o_ref, x_buf, xn_ref, wg_buf, wu_buf, wo_buf,
                x_sem, w_sem, *, layer, tm, tf, nf, final_norm):
    i = pl.program_id(0)
    n_blocks = pl.num_programs(0)
    f = nf * tf

    def x_copy(block):
        return pltpu.make_async_copy(x_hbm.at[pl.ds(pl.multiple_of(block * tm, tm), tm), :], x_buf, x_sem)

    def chunk_copies(j, slot):
        c0 = pl.multiple_of(j * tf, tf)
        return (
            pltpu.make_async_copy(w_in_hbm.at[layer, :, pl.ds(c0, tf)], wg_buf.at[slot], w_sem.at[0, slot]),
            pltpu.make_async_copy(w_in_hbm.at[layer, :, pl.ds(f + c0, tf)], wu_buf.at[slot], w_sem.at[1, slot]),
            pltpu.make_async_copy(w_out_hbm.at[layer, pl.ds(c0, tf), :], wo_buf.at[slot], w_sem.at[2, slot]),
        )

    def start(j, slot):
        for copy in chunk_copies(j, slot):
            copy.start()

    def wait(j, slot):
        for copy in chunk_copies(j, slot):
            copy.wait()

    @pl.when(i == 0)
    def _():
        x_copy(0).start()
        start(0, 0)

    x_copy(i).wait()
    x = x_buf[...]
    xn_ref[...] = _rms(x, g_ref[...]).astype(BF16)
    o_ref[...] = x

    @pl.when(i + 1 < n_blocks)
    def _():
        x_copy(i + 1).start()

    def chunk(j, carry):
        slot = lax.rem(i * nf + j, 2)
        wait(j, slot)

        @pl.when(j + 1 < nf)
        def _():
            start(j + 1, 1 - slot)

        @pl.when(jnp.logical_and(j + 1 == nf, i + 1 < n_blocks))
        def _():
            start(0, 1 - slot)

        xn = xn_ref[...]
        gate = _dot(xn, wg_buf[slot].astype(BF16))
        up = _dot(xn, wu_buf[slot].astype(BF16))
        act = (0.5 * gate) * jax.nn.sigmoid(gate) * up
        o_ref[...] += _dot(act.astype(BF16), wo_buf[slot].astype(BF16))
        return carry

    lax.fori_loop(0, nf, chunk, 0)

    if final_norm:
        o_ref[...] = _rms(o_ref[...], gf_ref[...])


def ffn(x, g, w_in, w_out, g_final, layer, *, tm, tf, final_norm):
    m, d = x.shape
    f = w_out.shape[1]
    assert m % tm == 0 and f % tf == 0
    nf = f // tf
    return pl.pallas_call(
        functools.partial(_ffn_kernel, layer=layer, tm=tm, tf=tf, nf=nf, final_norm=final_norm),
        grid=(m // tm,),
        in_specs=[
            pl.BlockSpec(memory_space=pl.ANY),
            _vec(layer, d),
            pl.BlockSpec(memory_space=pl.ANY),
            pl.BlockSpec(memory_space=pl.ANY),
            pl.BlockSpec((1, d), lambda i: (0, 0)),
        ],
        out_specs=pl.BlockSpec((tm, d), lambda i: (i, 0)),
        out_shape=jax.ShapeDtypeStruct((m, d), F32),
        scratch_shapes=[
            pltpu.VMEM((tm, d), F32), pltpu.VMEM((tm, d), BF16),
            pltpu.VMEM((2, d, tf), F32), pltpu.VMEM((2, d, tf), F32), pltpu.VMEM((2, tf, d), F32),
            pltpu.SemaphoreType.DMA(()), pltpu.SemaphoreType.DMA((3, 2)),
        ],
        compiler_params=pltpu.CompilerParams(dimension_semantics=("arbitrary",),
                                             vmem_limit_bytes=FFN_VMEM_LIMIT_BYTES),
        name="ffn",
    )(x, g, w_in, w_out, g_final.reshape(1, d))


def _rope_pairs(x, table):
    y = x * table
    return y + pltpu.roll(y, MLA_ROPE, 1)


def _mla_prep_kernel(cq_ref, ckv_ref, kr_ref, tab_ref, tab_t_ref, gq_ref, gkv_ref, wq_t_ref, wk_ref, wv_t_ref,
                     q_t_ref, k_ref, v_t_ref):
    scale = (MLA_NOPE + MLA_ROPE) ** -0.5 * LOG2_E
    nq_t = jnp.transpose(_rms(cq_ref[...], gq_ref[...])).astype(BF16)
    q_t = _dot(wq_t_ref[...], nq_t) * scale
    nkv = _rms(ckv_ref[...], gkv_ref[...])
    kn = _dot(nkv.astype(BF16), wk_ref[...])
    v_t = _dot(wv_t_ref[...], jnp.transpose(nkv).astype(BF16))
    tab = tab_ref[...]
    tab_t = tab_t_ref[...]
    lane = lax.broadcasted_iota(jnp.int32, tab.shape, 1)
    k_pe = jnp.where(lane < MLA_ROPE, _rope_pairs(kr_ref[...], tab), 0.0).astype(BF16)
    zeros = jnp.zeros((MLA_QK_PAD - MLA_NOPE - MLA_ROPE, q_t.shape[1]), BF16)
    ones_row = lax.broadcasted_iota(jnp.int32, (MLA_V_ROWS - MLA_V, q_t.shape[1]), 0) == 0
    ones_tile = jnp.where(ones_row, 1.0, 0.0).astype(BF16)
    for h in range(MLA_HEADS):
        q0 = h * MLA_QK_PAD
        q_t_ref[0, h, 0:MLA_NOPE, :] = q_t[q0:q0 + MLA_NOPE].astype(BF16)
        pe = q_t[q0 + MLA_NOPE:q0 + MLA_QK_PAD] * tab_t
        q_t_ref[0, h, MLA_NOPE:MLA_NOPE + MLA_ROPE, :] = (pe[0:MLA_ROPE] + pe[MLA_ROPE:2 * MLA_ROPE]).astype(BF16)
        q_t_ref[0, h, MLA_NOPE + MLA_ROPE:MLA_QK_PAD, :] = zeros
        k_ref[0, h, :, 0:LANES] = kn[:, h * LANES:(h + 1) * LANES].astype(BF16)
        k_ref[0, h, :, LANES:2 * LANES] = k_pe
        v_t_ref[0, h, 0:MLA_V, :] = v_t[h * MLA_V:(h + 1) * MLA_V].astype(BF16)
        v_t_ref[0, h, MLA_V:MLA_V_ROWS, :] = ones_tile


def mla_prep(z, table, g_q, g_kv, wq_t, wk, wv_t, layer, *, batch, seq, tm):
    assert seq % tm == 0
    nb = seq // tm
    r = MLA_RANK
    hw = MLA_HEADS
    tok = lambda width, col: pl.BlockSpec((tm, width), lambda i: (i, col // width))
    full = lambda a: _layer(layer, a.shape[1:], lambda i: (0, 0))
    feat_major = lambda width: pl.BlockSpec((1, hw, width, tm), lambda i: (i // nb, 0, 0, i % nb))
    return pl.pallas_call(
        _mla_prep_kernel,
        grid=(batch * nb,),
        in_specs=[
            tok(r, Z_CQ), tok(r, Z_CKV), tok(LANES, Z_KR),
            pl.BlockSpec((tm, LANES), lambda i: (i % nb, 0)),
            pl.BlockSpec((LANES, tm), lambda i: (0, i % nb)),
            full(g_q), full(g_kv), full(wq_t), full(wk), full(wv_t),
        ],
        out_specs=[
            feat_major(MLA_QK_PAD),
            pl.BlockSpec((1, hw, tm, MLA_QK_PAD), lambda i: (i // nb, 0, i % nb, 0)),
            feat_major(MLA_V_ROWS),
        ],
        out_shape=[
            jax.ShapeDtypeStruct((batch, hw, MLA_QK_PAD, seq), BF16),
            jax.ShapeDtypeStruct((batch, hw, seq, MLA_QK_PAD), BF16),
            jax.ShapeDtypeStruct((batch, hw, MLA_V_ROWS, seq), BF16),
        ],
        compiler_params=_params("arbitrary"),
        name="mla_prep",
    )(z, z, z, table, jnp.transpose(table), g_q, g_kv, wq_t, wk, wv_t)


def _mla_attn_kernel(q_t_ref, k_ref, v_t_ref, o_ref, m_ref, acc_ref, *, tq, heads):
    qi = pl.program_id(2)
    m_ref[...] = jnp.full(m_ref.shape, -jnp.inf, F32)
    acc_ref[...] = jnp.zeros(acc_ref.shape, F32)

    def step(j, masked):
        start = pl.multiple_of(j * tq, tq)
        scores = [_dot(k_ref[0, g, pl.ds(start, tq), :], q_t_ref[0, g]) for g in range(heads)]
        for g in range(heads):
            s = scores[g]
            if masked:
                key = lax.broadcasted_iota(jnp.int32, s.shape, 0)
                qry = lax.broadcasted_iota(jnp.int32, s.shape, 1)
                s = jnp.where(key <= qry, s, -jnp.inf)
            m_old = m_ref[g]
            m_new = jnp.maximum(m_old, jnp.max(s, axis=0, keepdims=True))
            p = jnp.exp2(s - m_new).astype(BF16)
            alpha = jnp.exp2(m_old - m_new)
            acc_ref[g] = alpha * acc_ref[g] + _dot(v_t_ref[0, g, :, pl.ds(start, tq)], p)
            m_ref[g] = m_new

    def body(j, carry):
        step(j, False)
        return carry

    lax.fori_loop(0, qi, body, 0)
    step(qi, True)
    for g in range(heads):
        o_t = acc_ref[g, 0:MLA_V] / acc_ref[g, MLA_V:MLA_V + 1]
        o_ref[0, :, g * MLA_V:(g + 1) * MLA_V] = jnp.transpose(o_t).astype(o_ref.dtype)


def mla_attn(q_t, k, v_t, *, tq, heads):
    b, h, s, dq = k.shape
    dv = v_t.shape[2]
    assert s % tq == 0 and h % heads == 0 and dv == MLA_V_ROWS
    return pl.pallas_call(
        functools.partial(_mla_attn_kernel, tq=tq, heads=heads),
        grid=(b, h // heads, s // tq),
        in_specs=[
            pl.BlockSpec((1, heads, dq, tq), lambda bi, hi, qi: (bi, hi, 0, qi)),
            pl.BlockSpec((1, heads, s, dq), lambda bi, hi, qi: (bi, hi, 0, 0), pipeline_mode=pl.Buffered(1)),
            pl.BlockSpec((1, heads, dv, s), lambda bi, hi, qi: (bi, hi, 0, 0), pipeline_mode=pl.Buffered(1)),
        ],
        out_specs=pl.BlockSpec((1, tq, heads * MLA_V), lambda bi, hi, qi: (bi, qi, hi)),
        out_shape=jax.ShapeDtypeStruct((b, s, h * MLA_V), BF16),
        scratch_shapes=[pltpu.VMEM((heads, 1, tq), F32), pltpu.VMEM((heads, dv, tq), F32)],
        compiler_params=_params("arbitrary", "arbitrary", "arbitrary"),
        name="mla_attn",
    )(q_t, k, v_t)


def _gla_kernel(q_ref, k_ref, v_ref, r_ref, al_ref, cx_ref, halo_ref, wa_ref, ba_ref, ng_ref, taps_ref,
                o_ref, oc_ref, state_ref, b_scr, q_scr, k_scr, v_scr, oi_scr, *, tc):
    c = GLA_CHUNK
    pair_w = 2 * GLA_DK
    pair_v = 2 * GLA_DV
    n_pairs = GLA_HEADS // 2
    sequence_start = pl.program_id(1) == 0

    @pl.when(sequence_start)
    def _():
        state_ref[...] = jnp.zeros(state_ref.shape, F32)

    oc_ref[...] = _short_conv(cx_ref[...], halo_ref[...], taps_ref[...], sequence_start).astype(oc_ref.dtype)

    row_i = lax.broadcasted_iota(jnp.int32, (c, c), 0)
    col_i = lax.broadcasted_iota(jnp.int32, (c, c), 1)
    tri = jnp.where(row_i >= col_i, 1.0, 0.0).astype(BF16)
    lane_k = lax.broadcasted_iota(jnp.int32, (pair_w, LANES), 0)
    head_sum = [jnp.where((lane_k // GLA_DK) == hh, 1.0, 0.0).astype(BF16) for hh in range(2)]
    sub = GLA_SUBCHUNK
    key_row = lax.broadcasted_iota(jnp.int32, (c, pair_w), 0)
    lane_h = lax.broadcasted_iota(jnp.int32, (1, pair_w), 1) // GLA_DK
    head_lanes = [jnp.where(lane_h == hh, 1.0, 0.0) for hh in range(2)]
    srow = lax.broadcasted_iota(jnp.int32, (pair_w, pair_v), 0) // GLA_DK
    scol = lax.broadcasted_iota(jnp.int32, (pair_w, pair_v), 1) // GLA_DV
    own_head = srow == scol

    def chunk(ci, carry):
        r0 = pl.multiple_of(ci * c, c)
        rows = pl.ds(r0, c)
        x = _dot(al_ref[rows, :].astype(BF16), wa_ref[...]) + ba_ref[...]
        log_a = (jnp.minimum(x, 0.0) - jnp.log(1.0 + jnp.exp(-jnp.abs(x)))) * (1.0 / GLA_TAU)
        p1 = log_a.astype(BF16)
        r1 = log_a - p1.astype(F32)
        p2 = r1.astype(BF16)
        p3 = (r1 - p2.astype(F32)).astype(BF16)
        b_scr[...] = _dot(tri, p1) + _dot(tri, p2) + _dot(tri, p3)
        q_scr[...] = q_ref[rows, :] * (GLA_DK ** -0.5)
        k_scr[...] = k_ref[rows, :]
        v_scr[...] = v_ref[rows, :]

        staged = []
        for p in range(n_pairs):
            kl = slice(p * pair_w, (p + 1) * pair_w)
            vl = slice(p * pair_v, (p + 1) * pair_v)
            bp = b_scr[:, kl]
            qp = q_scr[:, kl]
            kp = k_scr[:, kl]
            vp = v_scr[:, vl]
            b_last = b_scr[c - 1:c, kl]
            st = state_ref[p]
            q_dec = (qp * jnp.exp(bp)).astype(BF16)
            k_dec = (kp * jnp.exp(b_last - bp)).astype(BF16)
            o_inter = _dot(q_dec, st.astype(BF16))
            vp_bf = vp.astype(BF16)
            kv = lax.dot_general(k_dec, vp_bf, (((0,), (0,)), ((), ())), preferred_element_type=F32)
            decay_col = jnp.transpose(jnp.broadcast_to(jnp.exp(b_last), (LANES, pair_w)))
            decay_col = jnp.concatenate([decay_col, decay_col], axis=1)
            state_ref[p] = decay_col * st + jnp.where(own_head, kv, 0.0)
            q_blocks, k_blocks = [], []
            for i in range(1, c // sub):
                r = i * sub
                b_r = b_scr[r:r + 1, kl]
                q_i = q_scr[r:r + sub, kl] * jnp.exp(b_scr[r:r + sub, kl] - b_r)
                pieces = [jnp.zeros((r, pair_w), F32), q_i]
                if c - r - sub:
                    pieces.append(jnp.zeros((c - r - sub, pair_w), F32))
                q_blocks.append(jnp.concatenate(pieces, axis=0))
                k_blocks.append(kp * jnp.exp(jnp.where(key_row < r, b_r - bp, -jnp.inf)))
            q_cat = jnp.concatenate(q_blocks, axis=1)
            k_cat = jnp.concatenate(k_blocks, axis=1).astype(BF16)
            a_far = [_dot_t((q_cat * jnp.concatenate([head_lanes[hh]] * len(q_blocks), axis=1)).astype(BF16), k_cat)
                     for hh in range(2)]
            ys = []
            for i in range(c // sub):
                r = i * sub
                for t0 in (r, r + 8):
                    n = r + sub - t0
                    bt = b_scr[t0:r + sub, kl]
                    qt = q_scr[t0:r + sub, kl]
                    t_idx = t0 + lax.broadcasted_iota(jnp.int32, (n, pair_w), 0)
                    for s in range(t0, t0 + 8):
                        e = jnp.exp(jnp.where(t_idx >= s, bt - b_scr[s:s + 1, kl], -jnp.inf))
                        ys.append(qt * e * k_scr[s:s + 1, kl])
            y = jnp.concatenate(ys, axis=0).astype(BF16)
            a_near = [_dot(y, head_sum[hh]) for hh in range(2)]
            staged.append((o_inter, vp_bf, a_far, a_near))

        for p in range(n_pairs):
            o_inter, vp_bf, a_far, a_near = staged[p]
            for hh in range(2):
                hv = slice(p * pair_v + hh * GLA_DV, p * pair_v + (hh + 1) * GLA_DV)
                o_h = o_inter[:, hh * GLA_DV:(hh + 1) * GLA_DV] + _dot(a_far[hh].astype(BF16),
                                                                       vp_bf[:, hh * GLA_DV:(hh + 1) * GLA_DV])
                a = a_near[hh]
                for i in range(c // sub):
                    r = i * sub
                    base = i * (8 * sub + 8 * 8)
                    upd = a[base:base + sub] * v_scr[r:r + 1, hv]
                    for idx in range(1, 8):
                        upd += a[base + idx * sub:base + (idx + 1) * sub] * v_scr[r + idx:r + idx + 1, hv]
                    base += 8 * sub
                    upd8 = a[base:base + 8] * v_scr[r + 8:r + 9, hv]
                    for idx in range(1, 8):
                        upd8 += a[base + idx * 8:base + (idx + 1) * 8] * v_scr[r + 8 + idx:r + 9 + idx, hv]
                    upd = jnp.concatenate([upd[0:8], upd[8:sub] + upd8], axis=0)
                    oi_scr[r:r + sub, hv] = upd + o_h[r:r + sub]

        o = oi_scr[...]
        gate = r_ref[rows, :]
        gate = gate * jax.nn.sigmoid(gate)
        for h in range(GLA_HEADS):
            hv = slice(h * GLA_DV, (h + 1) * GLA_DV)
            oh = o[:, hv]
            oh = oh * lax.rsqrt(jnp.mean(oh * oh, axis=-1, keepdims=True) + EPS)
            o_ref[rows, hv] = (oh * ng_ref[:, hv] * gate[:, hv]).astype(o_ref.dtype)
        return carry

    lax.fori_loop(0, tc // c, chunk, 0, unroll=4)


def gla_and_conv(z, w_a2, b_a, norm_g, conv_taps, layer, *, batch, seq, tc):
    assert seq % tc == 0 and tc % GLA_CHUNK == 0
    nb = seq // tc
    kw = GLA_HEADS * GLA_DK
    vw = GLA_HEADS * GLA_DV
    cw = 3 * CONV_DIM
    tok = lambda width, col: pl.BlockSpec((tc, width), lambda b, i: (b * nb + i, col // width))
    full = lambda a: _layer(layer, a.shape[1:], lambda b, i: (0, 0))
    c = GLA_CHUNK
    return pl.pallas_call(
        functools.partial(_gla_kernel, tc=tc),
        grid=(batch, nb),
        in_specs=[
            tok(kw, Z_GQ), tok(kw, Z_GK), tok(vw, Z_GV), tok(vw, Z_GR), tok(LANES, Z_AL), tok(cw, Z_CONV),
            pl.BlockSpec((SUBLANES, cw),
                         lambda b, i: (jnp.maximum((b * nb + i) * (tc // SUBLANES) - 1, 0), Z_CONV // cw)),
            full(w_a2), full(b_a), full(norm_g), full(conv_taps),
        ],
        out_specs=[pl.BlockSpec((tc, vw), lambda b, i: (b * nb + i, 0)),
                   pl.BlockSpec((tc, CONV_DIM), lambda b, i: (b * nb + i, 0))],
        out_shape=[jax.ShapeDtypeStruct((batch * seq, vw), BF16),
                   jax.ShapeDtypeStruct((batch * seq, CONV_DIM), BF16)],
        scratch_shapes=[
            pltpu.VMEM((GLA_HEADS // 2, 2 * GLA_DK, 2 * GLA_DV), F32),
            pltpu.VMEM((c, kw), F32), pltpu.VMEM((c, kw), F32), pltpu.VMEM((c, kw), F32),
            pltpu.VMEM((c, vw), F32), pltpu.VMEM((c, vw), F32),
        ],
        compiler_params=_params("arbitrary", "arbitrary"),
        name="gla",
    )(z, z, z, z, z, z, z, w_a2, b_a, norm_g, conv_taps)


def _short_conv(x, halo, w, sequence_start):
    d = CONV_DIM
    z = x[:, d:2 * d] * x[:, 2 * d:3 * d]
    hz = halo[:, d:2 * d] * halo[:, 2 * d:3 * d]
    hz = jnp.where(sequence_start, 0.0, hz)
    row = lax.broadcasted_iota(jnp.int32, z.shape, 0)
    z1 = jnp.where(row == 0, hz[7:8], pltpu.roll(z, 1, 0))
    z2 = jnp.where(row == 0, hz[6:7], jnp.where(row == 1, hz[7:8], pltpu.roll(z, 2, 0)))
    y = w[2:3, :] * z + w[1:2, :] * z1 + w[0:1, :] * z2
    return x[:, 0:d] * y


def _merge_kernel(h_ref, g_ref, a0_ref, a1_ref, a2_ref, wg0_ref, wg1_ref, wg2_ref, b0_ref, b1_ref, b2_ref,
                  wp0_ref, wp1_ref, wp2_ref, wo_ref, o_ref, u_ref):
    j = pl.program_id(1)

    @pl.when(j == 0)
    def _():
        h = h_ref[...]
        u_ref[...] = _rms(h, g_ref[...]).astype(BF16)
        o_ref[...] = h

    u = u_ref[...]
    merged = None
    for a_ref, wg_ref, b_ref, wp_ref in ((a0_ref, wg0_ref, b0_ref, wp0_ref),
                                         (a1_ref, wg1_ref, b1_ref, wp1_ref),
                                         (a2_ref, wg2_ref, b2_ref, wp2_ref)):
        gate = jax.nn.sigmoid(_dot(u, wg_ref[...]) + b_ref[...])
        term = gate * _dot(a_ref[...], wp_ref[...])
        merged = term if merged is None else merged + term
    o_ref[...] += _dot(merged.astype(BF16), wo_ref[...])


def merge(h, g, branches, w_gate, gate_col0, b_gate, w_projs, w_out, layer, *, tm, tn):
    m, d = h.shape
    assert m % tm == 0 and d % tn == 0 and gate_col0 % tn == 0
    nj = d // tn
    g0 = gate_col0 // tn
    row = lambda a: pl.BlockSpec((tm, a.shape[1]), lambda i, j: (i, 0))
    gate_w = lambda b: _layer(layer, (d, tn), lambda i, j: (0, g0 + b * nj + j))
    gate_b = lambda b: _layer(layer, (1, tn), lambda i, j: (0, b * nj + j))
    proj_w = lambda w: _layer(layer, (w.shape[1], tn), lambda i, j: (0, j))
    return pl.pallas_call(
        _merge_kernel,
        grid=(m // tm, nj),
        in_specs=[
            row(h), _vec(layer, d),
            row(branches[0]), row(branches[1]), row(branches[2]),
            gate_w(0), gate_w(1), gate_w(2), gate_b(0), gate_b(1), gate_b(2),
            proj_w(w_projs[0]), proj_w(w_projs[1]), proj_w(w_projs[2]),
            _layer(layer, (tn, d), lambda i, j: (j, 0)),
        ],
        out_specs=pl.BlockSpec((tm, d), lambda i, j: (i, 0)),
        out_shape=jax.ShapeDtypeStruct((m, d), F32),
        scratch_shapes=[pltpu.VMEM((tm, d), BF16)],
        compiler_params=_params("arbitrary", "arbitrary"),
        name="merge",
    )(h, g, *branches, w_gate, w_gate, w_gate, b_gate, b_gate, b_gate, *w_projs, w_out)


def _xattn_kernel(h_ref, g_ref, wq_ref, k_ref, v_ref, wo_ref, o_ref):
    h = h_ref[...]
    hn = _rms(h, g_ref[...]).astype(BF16)
    q = (_dot(hn, wq_ref[...]) * (X_HEAD_DIM ** -0.5)).astype(BF16)
    heads = [slice(j * X_HEAD_DIM, (j + 1) * X_HEAD_DIM) for j in range(X_HEADS)]
    scores = [_dot_t(q[:, hs], k_ref[:, hs]) for hs in heads]
    outs = []
    for hs, s in zip(heads, scores):
        p = jnp.exp(s - jnp.max(s, axis=-1, keepdims=True))
        o = _dot(p.astype(BF16), v_ref[:, hs]) / jnp.sum(p, axis=-1, keepdims=True)
        outs.append(o.astype(BF16))
    o_ref[...] = h + _dot(jnp.concatenate(outs, axis=1), wo_ref[...])


def xattn(h, g, w_q, kv, w_o, layer, *, seq, tm):
    m, d = h.shape
    assert seq % tm == 0
    nb = seq // tm
    once = pl.Buffered(1)
    return pl.pallas_call(
        _xattn_kernel,
        grid=(m // tm,),
        in_specs=[
            pl.BlockSpec((tm, d), lambda i: (i, 0)),
            _vec(layer, d),
            pl.BlockSpec((None, d, d), lambda i: (layer, 0, 0), pipeline_mode=once),
            pl.BlockSpec((MEM_LEN, d), lambda i: (i // nb, 0)),
            pl.BlockSpec((MEM_LEN, d), lambda i: (i // nb, 1)),
            pl.BlockSpec((None, d, d), lambda i: (layer, 0, 0), pipeline_mode=once),
        ],
        out_specs=pl.BlockSpec((tm, d), lambda i: (i, 0)),
        out_shape=jax.ShapeDtypeStruct((m, d), F32),
        compiler_params=_params("arbitrary"),
        name="xattn",
    )(h, g, w_q, kv, kv, w_o)


PACK_BLOCK = 512
_IN_SPLIT_NAMES = ("c_q", "c_kv", "k_rope", "g_q", "g_k", "g_v", "g_r", "a_low", "conv", "gate")
_IN_SPLIT_WIDTHS = (MLA_RANK, MLA_RANK, MLA_ROPE, GLA_HEADS * GLA_DK, GLA_HEADS * GLA_DK, GLA_HEADS * GLA_DV,
                    GLA_HEADS * GLA_DV, GLA_RANK, 3 * CONV_DIM, 3 * D_MODEL)
_SRC = {name: sum(_IN_SPLIT_WIDTHS[:i]) for i, name in enumerate(_IN_SPLIT_NAMES)}
_PACK_SRC = (_SRC["conv"], _SRC["conv"] + PACK_BLOCK, _SRC["conv"] + 2 * PACK_BLOCK, _SRC["c_q"], _SRC["c_kv"],
             _SRC["g_v"], _SRC["g_r"], _SRC["g_q"], _SRC["k_rope"])
_PACK_SPECIAL = len(_PACK_SRC) - 1
_SRC_A_LOW = _SRC["a_low"]
_SRC_GATE = _SRC["gate"]
GATE_COL0 = len(_PACK_SRC) * PACK_BLOCK


def _pack_in_proj_kernel(src_ref, w_t_ref, a_low_ref, o_ref):
    del src_ref
    j = pl.program_id(1)

    @pl.when(j != _PACK_SPECIAL)
    def _():
        o_ref[...] = jnp.transpose(w_t_ref[0]).astype(BF16)

    @pl.when(j == _PACK_SPECIAL)
    def _():
        half = MLA_ROPE // 2
        k_rope = w_t_ref[0, 0:MLA_ROPE, :]
        row = lax.broadcasted_iota(jnp.int32, a_low_ref.shape[1:], 0)
        a_low = jnp.where(row < GLA_RANK, a_low_ref[0], 0.0)
        rest = jnp.zeros((PACK_BLOCK - 2 * LANES, k_rope.shape[1]), F32)
        blk = jnp.concatenate([k_rope, k_rope[half:], k_rope[:half], a_low, rest], axis=0)
        o_ref[...] = jnp.transpose(blk).astype(BF16)


def _pack_in_proj(w):
    depth, d, n = w.shape
    w_t = jnp.swapaxes(w, 1, 2)
    n_gate = n - _SRC_GATE
    assert n_gate % PACK_BLOCK == 0
    rows = _PACK_SRC + tuple(_SRC_GATE + PACK_BLOCK * k for k in range(n_gate // PACK_BLOCK))
    assert all(r % SUBLANES == 0 for r in rows) and _SRC_A_LOW % SUBLANES == 0
    src = jnp.asarray([r // SUBLANES for r in rows], jnp.int32)
    n_blocks = src.shape[0]
    grid_spec = pltpu.PrefetchScalarGridSpec(
        num_scalar_prefetch=1,
        grid=(depth, n_blocks),
        in_specs=[
            pl.BlockSpec((pl.Element(1), pl.Element(PACK_BLOCK), pl.Element(d)),
                         lambda l, j, src: (l, src[j] * SUBLANES, 0)),
            pl.BlockSpec((pl.Element(1), pl.Element(LANES), pl.Element(d)), lambda l, j, src: (l, _SRC_A_LOW, 0)),
        ],
        out_specs=pl.BlockSpec((None, d, PACK_BLOCK), lambda l, j, src: (l, 0, j)),
    )
    return pl.pallas_call(
        _pack_in_proj_kernel,
        grid_spec=grid_spec,
        out_shape=jax.ShapeDtypeStruct((depth, d, n_blocks * PACK_BLOCK), BF16),
        compiler_params=_params("arbitrary", "arbitrary"),
        name="pack_in_proj",
    )(src, w_t, w_t)


def _pack_uq(w):
    depth = w.shape[0]
    w = w.reshape(depth, MLA_RANK, MLA_HEADS, MLA_NOPE + MLA_ROPE)
    half = MLA_ROPE // 2
    nope, pe = w[..., :MLA_NOPE], w[..., MLA_NOPE:]
    swap = jnp.concatenate([pe[..., half:], pe[..., :half]], axis=-1)
    packed = jnp.concatenate([nope, pe, swap], axis=-1).reshape(depth, MLA_RANK, MLA_HEADS * MLA_QK_PAD)
    return jnp.swapaxes(packed, 1, 2).astype(BF16)


def _pack_ukv(w):
    depth = w.shape[0]
    w = w.reshape(depth, MLA_RANK, MLA_HEADS, MLA_NOPE + MLA_V)
    wk = w[..., :MLA_NOPE].reshape(depth, MLA_RANK, MLA_HEADS * MLA_NOPE)
    wv = w[..., MLA_NOPE:].reshape(depth, MLA_RANK, MLA_HEADS * MLA_V)
    return wk.astype(BF16), jnp.swapaxes(wv, 1, 2).astype(BF16)


def _rope_table(seq):
    inv_freq = 1.0 / (ROPE_THETA ** (jnp.arange(0, MLA_ROPE, 2, dtype=F32) / MLA_ROPE))
    ang = jnp.arange(seq, dtype=F32)[:, None] * inv_freq[None, :]
    cos, sin = jnp.cos(ang), jnp.sin(ang)
    return jnp.concatenate([cos, cos, -sin, sin], axis=1)


def kernel(x, mem, ffn1_norm, ffn1_w_in, ffn1_w_out, mix_norm, mix_w_in, mix_b_gate, mla_q_norm, mla_kv_norm,
           mla_w_uq, mla_w_ukv, mla_w_proj, gla_w_a2, gla_b_a, gla_norm, gla_w_proj, conv_w, conv_w_proj,
           mix_w_out, xattn_norm, mem_norm, xattn_w_q, xattn_w_kv, xattn_w_o, ffn2_norm, ffn2_w_in,
           ffn2_w_out, final_norm):
    batch, seq, d = x.shape
    depth = ffn1_norm.shape[0]
    tokens = batch * seq
    table = _rope_table(seq)
    h = x.reshape(tokens, d)
    mem2 = mem.reshape(batch * mem.shape[1], d)
    bf = lambda w: w.astype(BF16)
    vec = lambda p: p.reshape(depth, 1, -1)

    w_mix_in = _pack_in_proj(mix_w_in)
    wq_t = _pack_uq(mla_w_uq)
    wk, wv_t = _pack_ukv(mla_w_ukv)
    w_a2 = bf(jnp.concatenate(
        [gla_w_a2, jnp.zeros((depth, LANES - GLA_RANK, GLA_HEADS * GLA_DK), F32)], axis=1))
    w_projs = (bf(mla_w_proj), bf(gla_w_proj), bf(conv_w_proj))
    w_mix_out, w_xq, w_xo = bf(mix_w_out), bf(xattn_w_q), bf(xattn_w_o)
    conv_taps = conv_w.reshape(depth, 3, CONV_DIM)

    t = TILES
    for l in range(depth):
        h = ffn(h, vec(ffn1_norm), ffn1_w_in, ffn1_w_out, final_norm, l, tm=t.ffn_rows, tf=t.ffn_cols,
                final_norm=False)

        z = norm_matmul(h, vec(mix_norm), w_mix_in, l, n=Z_WIDTH, tm=t.rows, tn=Z_WIDTH, out_dtype=F32)
        q_t, k, v_t = mla_prep(z, table, vec(mla_q_norm), vec(mla_kv_norm), wq_t, wk, wv_t, l,
                               batch=batch, seq=seq, tm=t.rows)
        a_mla = mla_attn(q_t, k, v_t, tq=t.rows, heads=t.attn_heads).reshape(tokens, MLA_HEADS * MLA_V)
        a_gla, a_conv = gla_and_conv(z, w_a2, vec(gla_b_a), vec(gla_norm), conv_taps, l,
                                     batch=batch, seq=seq, tc=t.rows)
        h = merge(h, vec(mix_norm), (a_mla, a_gla, a_conv), w_mix_in, GATE_COL0, vec(mix_b_gate), w_projs,
                  w_mix_out, l, tm=t.rows, tn=t.merge_cols)

        kv = norm_matmul(mem2, vec(mem_norm), xattn_w_kv, l, n=2 * d, tm=mem2.shape[0], tn=t.kv_cols,
                         out_dtype=BF16)
        h = xattn(h, vec(xattn_norm), w_xq, kv, w_xo, l, seq=seq, tm=t.rows)

        h = ffn(h, vec(ffn2_norm), ffn2_w_in, ffn2_w_out, final_norm, l, tm=t.ffn_rows, tf=t.ffn_cols,
                final_norm=(l == depth - 1))
    return h.reshape(batch, seq, d)
```

```python
import functools
from typing import NamedTuple

import jax
import jax.numpy as jnp
from jax import lax
from jax.experimental import pallas as pl
from jax.experimental.pallas import tpu as pltpu

F32 = jnp.float32
BF16 = jnp.bfloat16

EPS = 1e-6
D_MODEL = 2048
D_FF = 5632
MLA_HEADS = 8
MLA_RANK = 512
MLA_NOPE = 128
MLA_ROPE = 64
MLA_V = 128
MLA_QK_PAD = 256
ROPE_THETA = 10000.0
LOG2_E = 1.4426950408889634
MLA_V_ROWS = MLA_V + 16
GLA_HEADS = 4
GLA_DK = 64
GLA_DV = 128
GLA_RANK = 16
GLA_TAU = 16.0
GLA_CHUNK = 64
GLA_SUBCHUNK = 16
CONV_DIM = 512
X_HEADS = 4
X_HEAD_DIM = D_MODEL // X_HEADS
MEM_LEN = 256

LANES = 128
SUBLANES = 8
VMEM_LIMIT_BYTES = 56 * 1024 * 1024
FFN_VMEM_LIMIT_BYTES = 60 * 1024 * 1024


class Tiles(NamedTuple):
    rows: int = 512
    ffn_rows: int = 1024
    ffn_cols: int = 512
    merge_cols: int = 512
    kv_cols: int = 1024
    attn_heads: int = 8
    prep_rows: int = 1024


TILES = Tiles()

Z_CONV = 0
Z_CQ = 1536
Z_CKV = 2048
Z_GV = 2560
Z_GR = 3072
Z_GQ = 3584
Z_GK = 3840
Z_KR = 4096
Z_AL = 4224
Z_WIDTH = 4352


def _params(*sem):
    return pltpu.CompilerParams(dimension_semantics=sem, vmem_limit_bytes=VMEM_LIMIT_BYTES)


def _rms(x, g):
    return x * lax.rsqrt(jnp.mean(x * x, axis=-1, keepdims=True) + EPS) * g


def _dot(a, b):
    return jnp.dot(a, b, preferred_element_type=F32)


def _dot_t(a, b):
    return lax.dot_general(a, b, (((1,), (1,)), ((), ())), preferred_element_type=F32)


def _norm_matmul_kernel(x_ref, g_ref, w_ref, o_ref):
    xn = _rms(x_ref[...], g_ref[...]).astype(BF16)
    o_ref[...] = _dot(xn, w_ref[...].astype(BF16)).astype(o_ref.dtype)


def _layer(layer, block, index):
    return pl.BlockSpec((None,) + tuple(block), lambda *ids: (layer,) + tuple(index(*ids)))


def _vec(layer, width):
    return _layer(layer, (1, width), lambda *ids: (0, 0))


def norm_matmul(x, g, w, layer, *, n, tm, tn, out_dtype):
    m, k = x.shape
    assert m % tm == 0 and n % tn == 0 and n <= w.shape[2]
    w_mode = pl.Buffered(1) if n == tn else None
    return pl.pallas_call(
        _norm_matmul_kernel,
        grid=(n // tn, m // tm),
        in_specs=[
            pl.BlockSpec((tm, k), lambda j, i: (i, 0)),
            _vec(layer, k),
            pl.BlockSpec((None, k, tn), lambda j, i: (layer, 0, j), pipeline_mode=w_mode),
        ],
        out_specs=pl.BlockSpec((tm, tn), lambda j, i: (i, j)),
        out_shape=jax.ShapeDtypeStruct((m, n), out_dtype),
        compiler_params=_params("arbitrary", "arbitrary"),
        name="norm_matmul",
    )(x, g, w)


def _ffn_kernel(x_hbm, g_ref, w_in_hbm, w_out_hbm, gf_ref, o_ref, x_buf, xn_ref, wg_buf, wu_buf, wo_buf,
                x_sem, w_sem, *, layer, tm, tf, nf, final_norm):
    i = pl.program_id(0)
    n_blocks = pl.num_programs(0)
    f = nf * tf

    def x_copy(block):
        return pltpu.make_async_copy(x_hbm.at[pl.ds(pl.multiple_of(block * tm, tm), tm), :], x_buf, x_sem)

    def chunk_copies(j, slot):
        c0 = pl.multiple_of(j * tf, tf)
        return (
            pltpu.make_async_copy(w_in_hbm.at[layer, :, pl.ds(c0, tf)], wg_buf.at[slot], w_sem.at[0, slot]),
            pltpu.make_async_copy(w_in_hbm.at[layer, :, pl.ds(f + c0, tf)], wu_buf.at[slot], w_sem.at[1, slot]),
            pltpu.make_async_copy(w_out_hbm.at[layer, pl.ds(c0, tf), :], wo_buf.at[slot], w_sem.at[2, slot]),
        )

    def start(j, slot):
        for copy in chunk_copies(j, slot):
            copy.start()

    def wait(j, slot):
        for copy in chunk_copies(j, slot):
            copy.wait()

    @pl.when(i == 0)
    def _():
        x_copy(0).start()
        start(0, 0)

    x_copy(i).wait()
    x = x_buf[...]
    xn_ref[...] = _rms(x, g_ref[...]).astype(BF16)
    o_ref[...] = x

    @pl.when(i + 1 < n_blocks)
    def _():
        x_copy(i + 1).start()

    def chunk(j, carry):
        slot = lax.rem(i * nf + j, 2)
        wait(j, slot)

        @pl.when(j + 1 < nf)
        def _():
            start(j + 1, 1 - slot)

        @pl.when(jnp.logical_and(j + 1 == nf, i + 1 < n_blocks))
        def _():
            start(0, 1 - slot)

        xn = xn_ref[...]
        gate = _dot(xn, wg_buf[slot].astype(BF16))
        up = _dot(xn, wu_buf[slot].astype(BF16))
        act = (0.5 * gate) * jax.nn.sigmoid(gate) * up
        o_ref[...] += _dot(act.astype(BF16), wo_buf[slot].astype(BF16))
        return carry

    lax.fori_loop(0, nf, chunk, 0)

    if final_norm:
        o_ref[...] = _rms(o_ref[...], gf_ref[...])


def ffn(x, g, w_in, w_out, g_final, layer, *, tm, tf, final_norm):
    m, d = x.shape
    f = w_out.shape[1]
    assert m % tm == 0 and f % tf == 0
    nf = f // tf
    return pl.pallas_call(
        functools.partial(_ffn_kernel, layer=layer, tm=tm, tf=tf, nf=nf, final_norm=final_norm),
        grid=(m // tm,),
        in_specs=[
            pl.BlockSpec(memory_space=pl.ANY),
            _vec(layer, d),
            pl.BlockSpec(memory_space=pl.ANY),
            pl.BlockSpec(memory_space=pl.ANY),
            pl.BlockSpec((1, d), lambda i: (0, 0)),
        ],
        out_specs=pl.BlockSpec((tm, d), lambda i: (i, 0)),
        out_shape=jax.ShapeDtypeStruct((m, d), F32),
        scratch_shapes=[
            pltpu.VMEM((tm, d), F32), pltpu.VMEM((tm, d), BF16),
            pltpu.VMEM((2, d, tf), F32), pltpu.VMEM((2, d, tf), F32), pltpu.VMEM((2, tf, d), F32),
            pltpu.SemaphoreType.DMA(()), pltpu.SemaphoreType.DMA((3, 2)),
        ],
        compiler_params=pltpu.CompilerParams(dimension_semantics=("arbitrary",),
                                             vmem_limit_bytes=FFN_VMEM_LIMIT_BYTES),
        name="ffn",
    )(x, g, w_in, w_out, g_final.reshape(1, d))


def _rope_pairs(x, table):
    y = x * table
    return y + pltpu.roll(y, MLA_ROPE, 1)


def _mla_prep_kernel(cq_ref, ckv_ref, kr_ref, tab_ref, tab_t_ref, gq_ref, gkv_ref, wq_t_ref, wk_ref, wv_t_ref,
                     q_t_ref, k_ref, v_t_ref):
    scale = (MLA_NOPE + MLA_ROPE) ** -0.5 * LOG2_E
    nq_t = jnp.transpose(_rms(cq_ref[...], gq_ref[...])).astype(BF16)
    q_t = _dot(wq_t_ref[...], nq_t) * scale
    nkv = _rms(ckv_ref[...], gkv_ref[...])
    kn = _dot(nkv.astype(BF16), wk_ref[...])
    v_t = _dot(wv_t_ref[...], jnp.transpose(nkv).astype(BF16))
    tab = tab_ref[...]
    tab_t = tab_t_ref[...]
    lane = lax.broadcasted_iota(jnp.int32, tab.shape, 1)
    k_pe = jnp.where(lane < MLA_ROPE, _rope_pairs(kr_ref[...], tab), 0.0).astype(BF16)
    zeros = jnp.zeros((MLA_QK_PAD - MLA_NOPE - MLA_ROPE, q_t.shape[1]), BF16)
    ones_row = lax.broadcasted_iota(jnp.int32, (MLA_V_ROWS - MLA_V, q_t.shape[1]), 0) == 0
    ones_tile = jnp.where(ones_row, 1.0, 0.0).astype(BF16)
    for h in range(MLA_HEADS):
        q0 = h * MLA_QK_PAD
        q_t_ref[0, h, 0:MLA_NOPE, :] = q_t[q0:q0 + MLA_NOPE].astype(BF16)
        pe = q_t[q0 + MLA_NOPE:q0 + MLA_QK_PAD] * tab_t
        q_t_ref[0, h, MLA_NOPE:MLA_NOPE + MLA_ROPE, :] = (pe[0:MLA_ROPE] + pe[MLA_ROPE:2 * MLA_ROPE]).astype(BF16)
        q_t_ref[0, h, MLA_NOPE + MLA_ROPE:MLA_QK_PAD, :] = zeros
        k_ref[0, h, :, 0:LANES] = kn[:, h * LANES:(h + 1) * LANES].astype(BF16)
        k_ref[0, h, :, LANES:2 * LANES] = k_pe
        v_t_ref[0, h, 0:MLA_V, :] = v_t[h * MLA_V:(h + 1) * MLA_V].astype(BF16)
        v_t_ref[0, h, MLA_V:MLA_V_ROWS, :] = ones_tile


def mla_prep(z, table, g_q, g_kv, wq_t, wk, wv_t, layer, *, batch, seq, tm):
    assert seq % tm == 0
    nb = seq // tm
    r = MLA_RANK
    hw = MLA_HEADS
    tok = lambda width, col: pl.BlockSpec((tm, width), lambda i: (i, col // width))
    full = lambda a: _layer(layer, a.shape[1:], lambda i: (0, 0))
    feat_major = lambda width: pl.BlockSpec((1, hw, width, tm), lambda i: (i // nb, 0, 0, i % nb))
    return pl.pallas_call(
        _mla_prep_kernel,
        grid=(batch * nb,),
        in_specs=[
            tok(r, Z_CQ), tok(r, Z_CKV), tok(LANES, Z_KR),
            pl.BlockSpec((tm, LANES), lambda i: (i % nb, 0)),
            pl.BlockSpec((LANES, tm), lambda i: (0, i % nb)),
            full(g_q), full(g_kv), full(wq_t), full(wk), full(wv_t),
        ],
        out_specs=[
            feat_major(MLA_QK_PAD),
            pl.BlockSpec((1, hw, tm, MLA_QK_PAD), lambda i: (i // nb, 0, i % nb, 0)),
            feat_major(MLA_V_ROWS),
        ],
        out_shape=[
            jax.ShapeDtypeStruct((batch, hw, MLA_QK_PAD, seq), BF16),
            jax.ShapeDtypeStruct((batch, hw, seq, MLA_QK_PAD), BF16),
            jax.ShapeDtypeStruct((batch, hw, MLA_V_ROWS, seq), BF16),
        ],
        compiler_params=_params("arbitrary"),
        name="mla_prep",
    )(z, z, z, table, jnp.transpose(table), g_q, g_kv, wq_t, wk, wv_t)


def _mla_attn_kernel(q_t_ref, k_ref, v_t_ref, o_ref, m_ref, acc_ref, *, tq, heads):
    qi = pl.program_id(2)
    m_ref[...] = jnp.full(m_ref.shape, -jnp.inf, F32)
    acc_ref[...] = jnp.zeros(acc_ref.shape, F32)

    def step(j, masked):
        start = pl.multiple_of(j * tq, tq)
        scores = [_dot(k_ref[0, g, pl.ds(start, tq), :], q_t_ref[0, g]) for g in range(heads)]
        for g in range(heads):
            s = scores[g]
            if masked:
                key = lax.broadcasted_iota(jnp.int32, s.shape, 0)
                qry = lax.broadcasted_iota(jnp.int32, s.shape, 1)
                s = jnp.where(key <= qry, s, -jnp.inf)
            m_old = m_ref[g]
            m_new = jnp.maximum(m_old, jnp.max(s, axis=0, keepdims=True))
            p = jnp.exp2(s - m_new).astype(BF16)
            alpha = jnp.exp2(m_old - m_new)
            acc_ref[g] = alpha * acc_ref[g] + _dot(v_t_ref[0, g, :, pl.ds(start, tq)], p)
            m_ref[g] = m_new

    def body(j, carry):
        step(j, False)
        return carry

    lax.fori_loop(0, qi, body, 0)
    step(qi, True)
    for g in range(heads):
        o_t = acc_ref[g, 0:MLA_V] / acc_ref[g, MLA_V:MLA_V + 1]
        o_ref[0, :, g * MLA_V:(g + 1) * MLA_V] = jnp.transpose(o_t).astype(o_ref.dtype)


def mla_attn(q_t, k, v_t, *, tq, heads):
    b, h, s, dq = k.shape
    dv = v_t.shape[2]
    assert s % tq == 0 and h % heads == 0 and dv == MLA_V_ROWS
    return pl.pallas_call(
        functools.partial(_mla_attn_kernel, tq=tq, heads=heads),
        grid=(b, h // heads, s // tq),
        in_specs=[
            pl.BlockSpec((1, heads, dq, tq), lambda bi, hi, qi: (bi, hi, 0, qi)),
            pl.BlockSpec((1, heads, s, dq), lambda bi, hi, qi: (bi, hi, 0, 0), pipeline_mode=pl.Buffered(1)),
            pl.BlockSpec((1, heads, dv, s), lambda bi, hi, qi: (bi, hi, 0, 0), pipeline_mode=pl.Buffered(1)),
        ],
        out_specs=pl.BlockSpec((1, tq, heads * MLA_V), lambda bi, hi, qi: (bi, qi, hi)),
        out_shape=jax.ShapeDtypeStruct((b, s, h * MLA_V), BF16),
        scratch_shapes=[pltpu.VMEM((heads, 1, tq), F32), pltpu.VMEM((heads, dv, tq), F32)],
        compiler_params=_params("arbitrary", "arbitrary", "arbitrary"),
        name="mla_attn",
    )(q_t, k, v_t)


def _gla_kernel(q_ref, k_ref, v_ref, r_ref, al_ref, cx_ref, halo_ref, wa_ref, ba_ref, ng_ref, taps_ref,
                o_ref, oc_ref, state_ref, b_scr, q_scr, k_scr, v_scr, oi_scr, *, tc):
    c = GLA_CHUNK
    pair_w = 2 * GLA_DK
    pair_v = 2 * GLA_DV
    n_pairs = GLA_HEADS // 2
    sequence_start = pl.program_id(1) == 0

    @pl.when(sequence_start)
    def _():
        state_ref[...] = jnp.zeros(state_ref.shape, F32)

    oc_ref[...] = _short_conv(cx_ref[...], halo_ref[...], taps_ref[...], sequence_start).astype(oc_ref.dtype)

    row_i = lax.broadcasted_iota(jnp.int32, (c, c), 0)
    col_i = lax.broadcasted_iota(jnp.int32, (c, c), 1)
    tri = jnp.where(row_i >= col_i, 1.0, 0.0).astype(BF16)
    lane_k = lax.broadcasted_iota(jnp.int32, (pair_w, LANES), 0)
    head_sum = [jnp.where((lane_k // GLA_DK) == hh, 1.0, 0.0).astype(BF16) for hh in range(2)]
    sub = GLA_SUBCHUNK
    key_row = lax.broadcasted_iota(jnp.int32, (c, pair_w), 0)
    lane_h = lax.broadcasted_iota(jnp.int32, (1, pair_w), 1) // GLA_DK
    head_lanes = [jnp.where(lane_h == hh, 1.0, 0.0) for hh in range(2)]
    srow = lax.broadcasted_iota(jnp.int32, (pair_w, pair_v), 0) // GLA_DK
    scol = lax.broadcasted_iota(jnp.int32, (pair_w, pair_v), 1) // GLA_DV
    own_head = srow == scol

    def chunk(ci, carry):
        r0 = pl.multiple_of(ci * c, c)
        rows = pl.ds(r0, c)
        x = _dot(al_ref[rows, :].astype(BF16), wa_ref[...]) + ba_ref[...]
        log_a = (jnp.minimum(x, 0.0) - jnp.log(1.0 + jnp.exp(-jnp.abs(x)))) * (1.0 / GLA_TAU)
        p1 = log_a.astype(BF16)
        r1 = log_a - p1.astype(F32)
        p2 = r1.astype(BF16)
        p3 = (r1 - p2.astype(F32)).astype(BF16)
        b_scr[...] = _dot(tri, p1) + _dot(tri, p2) + _dot(tri, p3)
        q_scr[...] = q_ref[rows, :] * (GLA_DK ** -0.5)
        k_scr[...] = k_ref[rows, :]
        v_scr[...] = v_ref[rows, :]

        staged = []
        for p in range(n_pairs):
            kl = slice(p * pair_w, (p + 1) * pair_w)
            vl = slice(p * pair_v, (p + 1) * pair_v)
            bp = b_scr[:, kl]
            qp = q_scr[:, kl]
            kp = k_scr[:, kl]
            vp = v_scr[:, vl]
            b_last = b_scr[c - 1:c, kl]
            st = state_ref[p]
            q_dec = (qp * jnp.exp(bp)).astype(BF16)
            k_dec = (kp * jnp.exp(b_last - bp)).astype(BF16)
            o_inter = _dot(q_dec, st.astype(BF16))
            vp_bf = vp.astype(BF16)
            kv = lax.dot_general(k_dec, vp_bf, (((0,), (0,)), ((), ())), preferred_element_type=F32)
            decay_col = jnp.transpose(jnp.broadcast_to(jnp.exp(b_last), (LANES, pair_w)))
            decay_col = jnp.concatenate([decay_col, decay_col], axis=1)
            state_ref[p] = decay_col * st + jnp.where(own_head, kv, 0.0)
            q_blocks, k_blocks = [], []
            for i in range(1, c // sub):
                r = i * sub
                b_r = b_scr[r:r + 1, kl]
                q_i = q_scr[r:r + sub, kl] * jnp.exp(b_scr[r:r + sub, kl] - b_r)
                pieces = [jnp.zeros((r, pair_w), F32), q_i]
                if c - r - sub:
                    pieces.append(jnp.zeros((c - r - sub, pair_w), F32))
                q_blocks.append(jnp.concatenate(pieces, axis=0))
                k_blocks.append(kp * jnp.exp(jnp.where(key_row < r, b_r - bp, -jnp.inf)))
            q_cat = jnp.concatenate(q_blocks, axis=1)
            k_cat = jnp.concatenate(k_blocks, axis=1).astype(BF16)
            a_far = [_dot_t((q_cat * jnp.concatenate([head_lanes[hh]] * len(q_blocks), axis=1)).astype(BF16), k_cat)
                     for hh in range(2)]
            ys = []
            for i in range(c // sub):
                r = i * sub
                for t0 in (r, r + 8):
                    n = r + sub - t0
                    bt = b_scr[t0:r + sub, kl]
                    qt = q_scr[t0:r + sub, kl]
                    t_idx = t0 + lax.broadcasted_iota(jnp.int32, (n, pair_w), 0)
                    for s in range(t0, t0 + 8):
                        e = jnp.exp(jnp.where(t_idx >= s, bt - b_scr[s:s + 1, kl], -jnp.inf))
                        ys.append(qt * e * k_scr[s:s + 1, kl])
            y = jnp.concatenate(ys, axis=0).astype(BF16)
            a_near = [_dot(y, head_sum[hh]) for hh in range(2)]
            staged.append((o_inter, vp_bf, a_far, a_near))

        for p in range(n_pairs):
            o_inter, vp_bf, a_far, a_near = staged[p]
            for hh in range(2):
                hv = slice(p * pair_v + hh * GLA_DV, p * pair_v + (hh + 1) * GLA_DV)
                o_h = o_inter[:, hh * GLA_DV:(hh + 1) * GLA_DV] + _dot(a_far[hh].astype(BF16),
                                                                       vp_bf[:, hh * GLA_DV:(hh + 1) * GLA_DV])
                a = a_near[hh]
                for i in range(c // sub):
                    r = i * sub
                    base = i * (8 * sub + 8 * 8)
                    upd = a[base:base + sub] * v_scr[r:r + 1, hv]
                    for idx in range(1, 8):
                        upd += a[base + idx * sub:base + (idx + 1) * sub] * v_scr[r + idx:r + idx + 1, hv]
                    base += 8 * sub
                    upd8 = a[base:base + 8] * v_scr[r + 8:r + 9, hv]
                    for idx in range(1, 8):
                        upd8 += a[base + idx * 8:base + (idx + 1) * 8] * v_scr[r + 8 + idx:r + 9 + idx, hv]
                    upd = jnp.concatenate([upd[0:8], upd[8:sub] + upd8], axis=0)
                    oi_scr[r:r + sub, hv] = upd + o_h[r:r + sub]

        o = oi_scr[...]
        gate = r_ref[rows, :]
        gate = gate * jax.nn.sigmoid(gate)
        for h in range(GLA_HEADS):
            hv = slice(h * GLA_DV, (h + 1) * GLA_DV)
            oh = o[:, hv]
            oh = oh * lax.rsqrt(jnp.mean(oh * oh, axis=-1, keepdims=True) + EPS)
            o_ref[rows, hv] = (oh * ng_ref[:, hv] * gate[:, hv]).astype(o_ref.dtype)
        return carry

    lax.fori_loop(0, tc // c, chunk, 0, unroll=4)


def gla_and_conv(z, w_a2, b_a, norm_g, conv_taps, layer, *, batch, seq, tc):
    assert seq % tc == 0 and tc % GLA_CHUNK == 0
    nb = seq // tc
    kw = GLA_HEADS * GLA_DK
    vw = GLA_HEADS * GLA_DV
    cw = 3 * CONV_DIM
    tok = lambda width, col: pl.BlockSpec((tc, width), lambda b, i: (b * nb + i, col // width))
    full = lambda a: _layer(layer, a.shape[1:], lambda b, i: (0, 0))
    c = GLA_CHUNK
    return pl.pallas_call(
        functools.partial(_gla_kernel, tc=tc),
        grid=(batch, nb),
        in_specs=[
            tok(kw, Z_GQ), tok(kw, Z_GK), tok(vw, Z_GV), tok(vw, Z_GR), tok(LANES, Z_AL), tok(cw, Z_CONV),
            pl.BlockSpec((SUBLANES, cw),
                         lambda b, i: (jnp.maximum((b * nb + i) * (tc // SUBLANES) - 1, 0), Z_CONV // cw)),
            full(w_a2), full(b_a), full(norm_g), full(conv_taps),
        ],
        out_specs=[pl.BlockSpec((tc, vw), lambda b, i: (b * nb + i, 0)),
                   pl.BlockSpec((tc, CONV_DIM), lambda b, i: (b * nb + i, 0))],
        out_shape=[jax.ShapeDtypeStruct((batch * seq, vw), BF16),
                   jax.ShapeDtypeStruct((batch * seq, CONV_DIM), BF16)],
        scratch_shapes=[
            pltpu.VMEM((GLA_HEADS // 2, 2 * GLA_DK, 2 * GLA_DV), F32),
            pltpu.VMEM((c, kw), F32), pltpu.VMEM((c, kw), F32), pltpu.VMEM((c, kw), F32),
            pltpu.VMEM((c, vw), F32), pltpu.VMEM((c, vw), F32),
        ],
        compiler_params=_params("arbitrary", "arbitrary"),
        name="gla",
    )(z, z, z, z, z, z, z, w_a2, b_a, norm_g, conv_taps)


def _short_conv(x, halo, w, sequence_start):
    d = CONV_DIM
    z = x[:, d:2 * d] * x[:, 2 * d:3 * d]
    hz = halo[:, d:2 * d] * halo[:, 2 * d:3 * d]
    hz = jnp.where(sequence_start, 0.0, hz)
    row = lax.broadcasted_iota(jnp.int32, z.shape, 0)
    z1 = jnp.where(row == 0, hz[7:8], pltpu.roll(z, 1, 0))
    z2 = jnp.where(row == 0, hz[6:7], jnp.where(row == 1, hz[7:8], pltpu.roll(z, 2, 0)))
    y = w[2:3, :] * z + w[1:2, :] * z1 + w[0:1, :] * z2
    return x[:, 0:d] * y


def _merge_kernel(h_ref, g_ref, a0_ref, a1_ref, a2_ref, wg0_ref, wg1_ref, wg2_ref, b0_ref, b1_ref, b2_ref,
                  wp0_ref, wp1_ref, wp2_ref, wo_ref, o_ref, u_ref):
    j = pl.program_id(1)

    @pl.when(j == 0)
    def _():
        h = h_ref[...]
        u_ref[...] = _rms(h, g_ref[...]).astype(BF16)
        o_ref[...] = h

    u = u_ref[...]
    merged = None
    for a_ref, wg_ref, b_ref, wp_ref in ((a0_ref, wg0_ref, b0_ref, wp0_ref),
                                         (a1_ref, wg1_ref, b1_ref, wp1_ref),
                                         (a2_ref, wg2_ref, b2_ref, wp2_ref)):
        gate = jax.nn.sigmoid(_dot(u, wg_ref[...]) + b_ref[...])
        term = gate * _dot(a_ref[...], wp_ref[...])
        merged = term if merged is None else merged + term
    o_ref[...] += _dot(merged.astype(BF16), wo_ref[...])


def merge(h, g, branches, w_gate, gate_col0, b_gate, w_projs, w_out, layer, *, tm, tn):
    m, d = h.shape
    assert m % tm == 0 and d % tn == 0 and gate_col0 % tn == 0
    nj = d // tn
    g0 = gate_col0 // tn
    row = lambda a: pl.BlockSpec((tm, a.shape[1]), lambda i, j: (i, 0))
    gate_w = lambda b: _layer(layer, (d, tn), lambda i, j: (0, g0 + b * nj + j))
    gate_b = lambda b: _layer(layer, (1, tn), lambda i, j: (0, b * nj + j))
    proj_w = lambda w: _layer(layer, (w.shape[1], tn), lambda i, j: (0, j))
    return pl.pallas_call(
        _merge_kernel,
        grid=(m // tm, nj),
        in_specs=[
            row(h), _vec(layer, d),
            row(branches[0]), row(branches[1]), row(branches[2]),
            gate_w(0), gate_w(1), gate_w(2), gate_b(0), gate_b(1), gate_b(2),
            proj_w(w_projs[0]), proj_w(w_projs[1]), proj_w(w_projs[2]),
            _layer(layer, (tn, d), lambda i, j: (j, 0)),
        ],
        out_specs=pl.BlockSpec((tm, d), lambda i, j: (i, 0)),
        out_shape=jax.ShapeDtypeStruct((m, d), F32),
        scratch_shapes=[pltpu.VMEM((tm, d), BF16)],
        compiler_params=_params("arbitrary", "arbitrary"),
        name="merge",
    )(h, g, *branches, w_gate, w_gate, w_gate, b_gate, b_gate, b_gate, *w_projs, w_out)


def _xattn_kernel(h_ref, g_ref, wq_ref, k_ref, v_ref, wo_ref, o_ref):
    h = h_ref[...]
    hn = _rms(h, g_ref[...]).astype(BF16)
    q = (_dot(hn, wq_ref[...]) * (X_HEAD_DIM ** -0.5)).astype(BF16)
    heads = [slice(j * X_HEAD_DIM, (j + 1) * X_HEAD_DIM) for j in range(X_HEADS)]
    scores = [_dot_t(q[:, hs], k_ref[:, hs]) for hs in heads]
    outs = []
    for hs, s in zip(heads, scores):
        p = jnp.exp(s - jnp.max(s, axis=-1, keepdims=True))
        o = _dot(p.astype(BF16), v_ref[:, hs]) / jnp.sum(p, axis=-1, keepdims=True)
        outs.append(o.astype(BF16))
    o_ref[...] = h + _dot(jnp.concatenate(outs, axis=1), wo_ref[...])


def xattn(h, g, w_q, kv, w_o, layer, *, seq, tm):
    m, d = h.shape
    assert seq % tm == 0
    nb = seq // tm
    once = pl.Buffered(1)
    return pl.pallas_call(
        _xattn_kernel,
        grid=(m // tm,),
        in_specs=[
            pl.BlockSpec((tm, d), lambda i: (i, 0)),
            _vec(layer, d),
            pl.BlockSpec((None, d, d), lambda i: (layer, 0, 0), pipeline_mode=once),
            pl.BlockSpec((MEM_LEN, d), lambda i: (i // nb, 0)),
            pl.BlockSpec((MEM_LEN, d), lambda i: (i // nb, 1)),
            pl.BlockSpec((None, d, d), lambda i: (layer, 0, 0), pipeline_mode=once),
        ],
        out_specs=pl.BlockSpec((tm, d), lambda i: (i, 0)),
        out_shape=jax.ShapeDtypeStruct((m, d), F32),
        compiler_params=_params("arbitrary"),
        name="xattn",
    )(h, g, w_q, kv, kv, w_o)


PACK_BLOCK = 512
_IN_SPLIT_NAMES = ("c_q", "c_kv", "k_rope", "g_q", "g_k", "g_v", "g_r", "a_low", "conv", "gate")
_IN_SPLIT_WIDTHS = (MLA_RANK, MLA_RANK, MLA_ROPE, GLA_HEADS * GLA_DK, GLA_HEADS * GLA_DK, GLA_HEADS * GLA_DV,
                    GLA_HEADS * GLA_DV, GLA_RANK, 3 * CONV_DIM, 3 * D_MODEL)
_SRC = {name: sum(_IN_SPLIT_WIDTHS[:i]) for i, name in enumerate(_IN_SPLIT_NAMES)}
_PACK_SRC = (_SRC["conv"], _SRC["conv"] + PACK_BLOCK, _SRC["conv"] + 2 * PACK_BLOCK, _SRC["c_q"], _SRC["c_kv"],
             _SRC["g_v"], _SRC["g_r"], _SRC["g_q"], _SRC["k_rope"])
_PACK_SPECIAL = len(_PACK_SRC) - 1
_SRC_A_LOW = _SRC["a_low"]
_SRC_GATE = _SRC["gate"]
GATE_COL0 = len(_PACK_SRC) * PACK_BLOCK


def _pack_in_proj_kernel(src_ref, w_t_ref, a_low_ref, o_ref):
    del src_ref
    j = pl.program_id(1)

    @pl.when(j != _PACK_SPECIAL)
    def _():
        o_ref[...] = jnp.transpose(w_t_ref[0]).astype(BF16)

    @pl.when(j == _PACK_SPECIAL)
    def _():
        half = MLA_ROPE // 2
        k_rope = w_t_ref[0, 0:MLA_ROPE, :]
        row = lax.broadcasted_iota(jnp.int32, a_low_ref.shape[1:], 0)
        a_low = jnp.where(row < GLA_RANK, a_low_ref[0], 0.0)
        rest = jnp.zeros((PACK_BLOCK - 2 * LANES, k_rope.shape[1]), F32)
        blk = jnp.concatenate([k_rope, k_rope[half:], k_rope[:half], a_low, rest], axis=0)
        o_ref[...] = jnp.transpose(blk).astype(BF16)


def _pack_in_proj(w):
    depth, d, n = w.shape
    w_t = jnp.swapaxes(w, 1, 2)
    n_gate = n - _SRC_GATE
    assert n_gate % PACK_BLOCK == 0
    rows = _PACK_SRC + tuple(_SRC_GATE + PACK_BLOCK * k for k in range(n_gate // PACK_BLOCK))
    assert all(r % SUBLANES == 0 for r in rows) and _SRC_A_LOW % SUBLANES == 0
    src = jnp.asarray([r // SUBLANES for r in rows], jnp.int32)
    n_blocks = src.shape[0]
    grid_spec = pltpu.PrefetchScalarGridSpec(
        num_scalar_prefetch=1,
        grid=(depth, n_blocks),
        in_specs=[
            pl.BlockSpec((pl.Element(1), pl.Element(PACK_BLOCK), pl.Element(d)),
                         lambda l, j, src: (l, src[j] * SUBLANES, 0)),
            pl.BlockSpec((pl.Element(1), pl.Element(LANES), pl.Element(d)), lambda l, j, src: (l, _SRC_A_LOW, 0)),
        ],
        out_specs=pl.BlockSpec((None, d, PACK_BLOCK), lambda l, j, src: (l, 0, j)),
    )
    return pl.pallas_call(
        _pack_in_proj_kernel,
        grid_spec=grid_spec,
        out_shape=jax.ShapeDtypeStruct((depth, d, n_blocks * PACK_BLOCK), BF16),
        compiler_params=_params("arbitrary", "arbitrary"),
        name="pack_in_proj",
    )(src, w_t, w_t)


def _pack_uq(w):
    depth = w.shape[0]
    w = w.reshape(depth, MLA_RANK, MLA_HEADS, MLA_NOPE + MLA_ROPE)
    half = MLA_ROPE // 2
    nope, pe = w[..., :MLA_NOPE], w[..., MLA_NOPE:]
    swap = jnp.concatenate([pe[..., half:], pe[..., :half]], axis=-1)
    packed = jnp.concatenate([nope, pe, swap], axis=-1).reshape(depth, MLA_RANK, MLA_HEADS * MLA_QK_PAD)
    return jnp.swapaxes(packed, 1, 2).astype(BF16)


def _pack_ukv(w):
    depth = w.shape[0]
    w = w.reshape(depth, MLA_RANK, MLA_HEADS, MLA_NOPE + MLA_V)
    wk = w[..., :MLA_NOPE].reshape(depth, MLA_RANK, MLA_HEADS * MLA_NOPE)
    wv = w[..., MLA_NOPE:].reshape(depth, MLA_RANK, MLA_HEADS * MLA_V)
    return wk.astype(BF16), jnp.swapaxes(wv, 1, 2).astype(BF16)


def _rope_table(seq):
    inv_freq = 1.0 / (ROPE_THETA ** (jnp.arange(0, MLA_ROPE, 2, dtype=F32) / MLA_ROPE))
    ang = jnp.arange(seq, dtype=F32)[:, None] * inv_freq[None, :]
    cos, sin = jnp.cos(ang), jnp.sin(ang)
    return jnp.concatenate([cos, cos, -sin, sin], axis=1)


def kernel(x, mem, ffn1_norm, ffn1_w_in, ffn1_w_out, mix_norm, mix_w_in, mix_b_gate, mla_q_norm, mla_kv_norm,
           mla_w_uq, mla_w_ukv, mla_w_proj, gla_w_a2, gla_b_a, gla_norm, gla_w_proj, conv_w, conv_w_proj,
           mix_w_out, xattn_norm, mem_norm, xattn_w_q, xattn_w_kv, xattn_w_o, ffn2_norm, ffn2_w_in,
           ffn2_w_out, final_norm):
    batch, seq, d = x.shape
    depth = ffn1_norm.shape[0]
    tokens = batch * seq
    table = _rope_table(seq)
    h = x.reshape(tokens, d)
    mem2 = mem.reshape(batch * mem.shape[1], d)
    bf = lambda w: w.astype(BF16)
    vec = lambda p: p.reshape(depth, 1, -1)

    w_mix_in = _pack_in_proj(mix_w_in)
    wq_t = _pack_uq(mla_w_uq)
    wk, wv_t = _pack_ukv(mla_w_ukv)
    w_a2 = bf(jnp.concatenate(
        [gla_w_a2, jnp.zeros((depth, LANES - GLA_RANK, GLA_HEADS * GLA_DK), F32)], axis=1))
    w_projs = (bf(mla_w_proj), bf(gla_w_proj), bf(conv_w_proj))
    w_mix_out, w_xq, w_xo = bf(mix_w_out), bf(xattn_w_q), bf(xattn_w_o)
    conv_taps = conv_w.reshape(depth, 3, CONV_DIM)

    t = TILES
    for l in range(depth):
        h = ffn(h, vec(ffn1_norm), ffn1_w_in, ffn1_w_out, final_norm, l, tm=t.ffn_rows, tf=t.ffn_cols,
                final_norm=False)

        z = norm_matmul(h, vec(mix_norm), w_mix_in, l, n=Z_WIDTH, tm=t.rows, tn=Z_WIDTH, out_dtype=F32)
        q_t, k, v_t = mla_prep(z, table, vec(mla_q_norm), vec(mla_kv_norm), wq_t, wk, wv_t, l,
                               batch=batch, seq=seq, tm=t.prep_rows)
        a_mla = mla_attn(q_t, k, v_t, tq=t.rows, heads=t.attn_heads).reshape(tokens, MLA_HEADS * MLA_V)
        a_gla, a_conv = gla_and_conv(z, w_a2, vec(gla_b_a), vec(gla_norm), conv_taps, l,
                                     batch=batch, seq=seq, tc=t.rows)
        h = merge(h, vec(mix_norm), (a_mla, a_gla, a_conv), w_mix_in, GATE_COL0, vec(mix_b_gate), w_projs,
                  w_mix_out, l, tm=t.rows, tn=t.merge_cols)

        kv = norm_matmul(mem2, vec(mem_norm), xattn_w_kv, l, n=2 * d, tm=mem2.shape[0], tn=t.kv_cols,
                         out_dtype=BF16)
        h = xattn(h, vec(xattn_norm), w_xq, kv, w_xo, l, seq=seq, tm=t.rows)

        h = ffn(h, vec(ffn2_norm), ffn2_w_in, ffn2_w_out, final_norm, l, tm=t.ffn_rows, tf=t.ffn_cols,
                final_norm=(l == depth - 1))
    return h.reshape(batch, seq, d)
```

```python
import functools
from typing import NamedTuple

import jax
import jax.numpy as jnp
from jax import lax
from jax.experimental import pallas as pl
from jax.experimental.pallas import tpu as pltpu

F32 = jnp.float32
BF16 = jnp.bfloat16

EPS = 1e-6
D_MODEL = 2048
D_FF = 5632
MLA_HEADS = 8
MLA_RANK = 512
MLA_NOPE = 128
MLA_ROPE = 64
MLA_V = 128
MLA_QK_PAD = 256
ROPE_THETA = 10000.0
LOG2_E = 1.4426950408889634
MLA_V_ROWS = MLA_V + 16
GLA_HEADS = 4
GLA_DK = 64
GLA_DV = 128
GLA_RANK = 16
GLA_TAU = 16.0
GLA_CHUNK = 64
GLA_SUBCHUNK = 16
CONV_DIM = 512
X_HEADS = 4
X_HEAD_DIM = D_MODEL // X_HEADS
MEM_LEN = 256

LANES = 128
SUBLANES = 8
VMEM_LIMIT_BYTES = 56 * 1024 * 1024
FFN_VMEM_LIMIT_BYTES = 60 * 1024 * 1024


class Tiles(NamedTuple):
    rows: int = 512
    ffn_rows: int = 1024
    ffn_cols: int = 512
    merge_cols: int = 512
    kv_cols: int = 1024
    attn_heads: int = 8
    prep_rows: int = 1024
    gla_rows: int = 1024


TILES = Tiles()

Z_CONV = 0
Z_CQ = 1536
Z_CKV = 2048
Z_GV = 2560
Z_GR = 3072
Z_GQ = 3584
Z_GK = 3840
Z_KR = 4096
Z_AL = 4224
Z_WIDTH = 4352


def _params(*sem):
    return pltpu.CompilerParams(dimension_semantics=sem, vmem_limit_bytes=VMEM_LIMIT_BYTES)


def _rms(x, g):
    return x * lax.rsqrt(jnp.mean(x * x, axis=-1, keepdims=True) + EPS) * g


def _dot(a, b):
    return jnp.dot(a, b, preferred_element_type=F32)


def _dot_t(a, b):
    return lax.dot_general(a, b, (((1,), (1,)), ((), ())), preferred_element_type=F32)


def _norm_matmul_kernel(x_ref, g_ref, w_ref, o_ref):
    xn = _rms(x_ref[...], g_ref[...]).astype(BF16)
    o_ref[...] = _dot(xn, w_ref[...].astype(BF16)).astype(o_ref.dtype)


def _layer(layer, block, index):
    return pl.BlockSpec((None,) + tuple(block), lambda *ids: (layer,) + tuple(index(*ids)))


def _vec(layer, width):
    return _layer(layer, (1, width), lambda *ids: (0, 0))


def norm_matmul(x, g, w, layer, *, n, tm, tn, out_dtype):
    m, k = x.shape
    assert m % tm == 0 and n % tn == 0 and n <= w.shape[2]
    w_mode = pl.Buffered(1) if n == tn else None
    return pl.pallas_call(
        _norm_matmul_kernel,
        grid=(n // tn, m // tm),
        in_specs=[
            pl.BlockSpec((tm, k), lambda j, i: (i, 0)),
            _vec(layer, k),
            pl.BlockSpec((None, k, tn), lambda j, i: (layer, 0, j), pipeline_mode=w_mode),
        ],
        out_specs=pl.BlockSpec((tm, tn), lambda j, i: (i, j)),
        out_shape=jax.ShapeDtypeStruct((m, n), out_dtype),
        compiler_params=_params("arbitrary", "arbitrary"),
        name="norm_matmul",
    )(x, g, w)


def _ffn_kernel(x_hbm, g_ref, w_in_hbm, w_out_hbm, gf_ref, o_ref, x_buf, xn_ref, wg_buf, wu_buf, wo_buf,
                x_sem, w_sem, *, layer, tm, tf, nf, final_norm):
    i = pl.program_id(0)
    n_blocks = pl.num_programs(0)
    f = nf * tf

    def x_copy(block):
        return pltpu.make_async_copy(x_hbm.at[pl.ds(pl.multiple_of(block * tm, tm), tm), :], x_buf, x_sem)

    def chunk_copies(j, slot):
        c0 = pl.multiple_of(j * tf, tf)
        return (
            pltpu.make_async_copy(w_in_hbm.at[layer, :, pl.ds(c0, tf)], wg_buf.at[slot], w_sem.at[0, slot]),
            pltpu.make_async_copy(w_in_hbm.at[layer, :, pl.ds(f + c0, tf)], wu_buf.at[slot], w_sem.at[1, slot]),
            pltpu.make_async_copy(w_out_hbm.at[layer, pl.ds(c0, tf), :], wo_buf.at[slot], w_sem.at[2, slot]),
        )

    def start(j, slot):
        for copy in chunk_copies(j, slot):
            copy.start()

    def wait(j, slot):
        for copy in chunk_copies(j, slot):
            copy.wait()

    @pl.when(i == 0)
    def _():
        x_copy(0).start()
        start(0, 0)

    x_copy(i).wait()
    x = x_buf[...]
    xn_ref[...] = _rms(x, g_ref[...]).astype(BF16)
    o_ref[...] = x

    @pl.when(i + 1 < n_blocks)
    def _():
        x_copy(i + 1).start()

    def chunk(j, carry):
        slot = lax.rem(i * nf + j, 2)
        wait(j, slot)

        @pl.when(j + 1 < nf)
        def _():
            start(j + 1, 1 - slot)

        @pl.when(jnp.logical_and(j + 1 == nf, i + 1 < n_blocks))
        def _():
            start(0, 1 - slot)

        xn = xn_ref[...]
        gate = _dot(xn, wg_buf[slot].astype(BF16))
        up = _dot(xn, wu_buf[slot].astype(BF16))
        act = (0.5 * gate) * jax.nn.sigmoid(gate) * up
        o_ref[...] += _dot(act.astype(BF16), wo_buf[slot].astype(BF16))
        return carry

    lax.fori_loop(0, nf, chunk, 0)

    if final_norm:
        o_ref[...] = _rms(o_ref[...], gf_ref[...])


def ffn(x, g, w_in, w_out, g_final, layer, *, tm, tf, final_norm):
    m, d = x.shape
    f = w_out.shape[1]
    assert m % tm == 0 and f % tf == 0
    nf = f // tf
    return pl.pallas_call(
        functools.partial(_ffn_kernel, layer=layer, tm=tm, tf=tf, nf=nf, final_norm=final_norm),
        grid=(m // tm,),
        in_specs=[
            pl.BlockSpec(memory_space=pl.ANY),
            _vec(layer, d),
            pl.BlockSpec(memory_space=pl.ANY),
            pl.BlockSpec(memory_space=pl.ANY),
            pl.BlockSpec((1, d), lambda i: (0, 0)),
        ],
        out_specs=pl.BlockSpec((tm, d), lambda i: (i, 0)),
        out_shape=jax.ShapeDtypeStruct((m, d), F32),
        scratch_shapes=[
            pltpu.VMEM((tm, d), F32), pltpu.VMEM((tm, d), BF16),
            pltpu.VMEM((2, d, tf), F32), pltpu.VMEM((2, d, tf), F32), pltpu.VMEM((2, tf, d), F32),
            pltpu.SemaphoreType.DMA(()), pltpu.SemaphoreType.DMA((3, 2)),
        ],
        compiler_params=pltpu.CompilerParams(dimension_semantics=("arbitrary",),
                                             vmem_limit_bytes=FFN_VMEM_LIMIT_BYTES),
        name="ffn",
    )(x, g, w_in, w_out, g_final.reshape(1, d))


def _rope_pairs(x, table):
    y = x * table
    return y + pltpu.roll(y, MLA_ROPE, 1)


def _mla_prep_kernel(cq_ref, ckv_ref, kr_ref, tab_ref, tab_t_ref, gq_ref, gkv_ref, wq_t_ref, wk_ref, wv_t_ref,
                     q_t_ref, k_ref, v_t_ref):
    scale = (MLA_NOPE + MLA_ROPE) ** -0.5 * LOG2_E
    nq_t = jnp.transpose(_rms(cq_ref[...], gq_ref[...])).astype(BF16)
    q_t = _dot(wq_t_ref[...], nq_t) * scale
    nkv = _rms(ckv_ref[...], gkv_ref[...])
    kn = _dot(nkv.astype(BF16), wk_ref[...])
    v_t = _dot(wv_t_ref[...], jnp.transpose(nkv).astype(BF16))
    tab = tab_ref[...]
    tab_t = tab_t_ref[...]
    lane = lax.broadcasted_iota(jnp.int32, tab.shape, 1)
    k_pe = jnp.where(lane < MLA_ROPE, _rope_pairs(kr_ref[...], tab), 0.0).astype(BF16)
    zeros = jnp.zeros((MLA_QK_PAD - MLA_NOPE - MLA_ROPE, q_t.shape[1]), BF16)
    ones_row = lax.broadcasted_iota(jnp.int32, (MLA_V_ROWS - MLA_V, q_t.shape[1]), 0) == 0
    ones_tile = jnp.where(ones_row, 1.0, 0.0).astype(BF16)
    for h in range(MLA_HEADS):
        q0 = h * MLA_QK_PAD
        q_t_ref[0, h, 0:MLA_NOPE, :] = q_t[q0:q0 + MLA_NOPE].astype(BF16)
        pe = q_t[q0 + MLA_NOPE:q0 + MLA_QK_PAD] * tab_t
        q_t_ref[0, h, MLA_NOPE:MLA_NOPE + MLA_ROPE, :] = (pe[0:MLA_ROPE] + pe[MLA_ROPE:2 * MLA_ROPE]).astype(BF16)
        q_t_ref[0, h, MLA_NOPE + MLA_ROPE:MLA_QK_PAD, :] = zeros
        k_ref[0, h, :, 0:LANES] = kn[:, h * LANES:(h + 1) * LANES].astype(BF16)
        k_ref[0, h, :, LANES:2 * LANES] = k_pe
        v_t_ref[0, h, 0:MLA_V, :] = v_t[h * MLA_V:(h + 1) * MLA_V].astype(BF16)
        v_t_ref[0, h, MLA_V:MLA_V_ROWS, :] = ones_tile


def mla_prep(z, table, g_q, g_kv, wq_t, wk, wv_t, layer, *, batch, seq, tm):
    assert seq % tm == 0
    nb = seq // tm
    r = MLA_RANK
    hw = MLA_HEADS
    tok = lambda width, col: pl.BlockSpec((tm, width), lambda i: (i, col // width))
    full = lambda a: _layer(layer, a.shape[1:], lambda i: (0, 0))
    feat_major = lambda width: pl.BlockSpec((1, hw, width, tm), lambda i: (i // nb, 0, 0, i % nb))
    return pl.pallas_call(
        _mla_prep_kernel,
        grid=(batch * nb,),
        in_specs=[
            tok(r, Z_CQ), tok(r, Z_CKV), tok(LANES, Z_KR),
            pl.BlockSpec((tm, LANES), lambda i: (i % nb, 0)),
            pl.BlockSpec((LANES, tm), lambda i: (0, i % nb)),
            full(g_q), full(g_kv), full(wq_t), full(wk), full(wv_t),
        ],
        out_specs=[
            feat_major(MLA_QK_PAD),
            pl.BlockSpec((1, hw, tm, MLA_QK_PAD), lambda i: (i // nb, 0, i % nb, 0)),
            feat_major(MLA_V_ROWS),
        ],
        out_shape=[
            jax.ShapeDtypeStruct((batch, hw, MLA_QK_PAD, seq), BF16),
            jax.ShapeDtypeStruct((batch, hw, seq, MLA_QK_PAD), BF16),
            jax.ShapeDtypeStruct((batch, hw, MLA_V_ROWS, seq), BF16),
        ],
        compiler_params=_params("arbitrary"),
        name="mla_prep",
    )(z, z, z, table, jnp.transpose(table), g_q, g_kv, wq_t, wk, wv_t)


def _mla_attn_kernel(q_t_ref, k_ref, v_t_ref, o_ref, m_ref, acc_ref, *, tq, heads):
    qi = pl.program_id(2)
    m_ref[...] = jnp.full(m_ref.shape, -jnp.inf, F32)
    acc_ref[...] = jnp.zeros(acc_ref.shape, F32)

    def step(j, masked):
        start = pl.multiple_of(j * tq, tq)
        scores = [_dot(k_ref[0, g, pl.ds(start, tq), :], q_t_ref[0, g]) for g in range(heads)]
        for g in range(heads):
            s = scores[g]
            if masked:
                key = lax.broadcasted_iota(jnp.int32, s.shape, 0)
                qry = lax.broadcasted_iota(jnp.int32, s.shape, 1)
                s = jnp.where(key <= qry, s, -jnp.inf)
            m_old = m_ref[g]
            m_new = jnp.maximum(m_old, jnp.max(s, axis=0, keepdims=True))
            p = jnp.exp2(s - m_new).astype(BF16)
            alpha = jnp.exp2(m_old - m_new)
            acc_ref[g] = alpha * acc_ref[g] + _dot(v_t_ref[0, g, :, pl.ds(start, tq)], p)
            m_ref[g] = m_new

    def body(j, carry):
        step(j, False)
        return carry

    lax.fori_loop(0, qi, body, 0)
    step(qi, True)
    for g in range(heads):
        o_t = acc_ref[g, 0:MLA_V] / acc_ref[g, MLA_V:MLA_V + 1]
        o_ref[0, :, g * MLA_V:(g + 1) * MLA_V] = jnp.transpose(o_t).astype(o_ref.dtype)


def mla_attn(q_t, k, v_t, *, tq, heads):
    b, h, s, dq = k.shape
    dv = v_t.shape[2]
    assert s % tq == 0 and h % heads == 0 and dv == MLA_V_ROWS
    return pl.pallas_call(
        functools.partial(_mla_attn_kernel, tq=tq, heads=heads),
        grid=(b, h // heads, s // tq),
        in_specs=[
            pl.BlockSpec((1, heads, dq, tq), lambda bi, hi, qi: (bi, hi, 0, qi)),
            pl.BlockSpec((1, heads, s, dq), lambda bi, hi, qi: (bi, hi, 0, 0), pipeline_mode=pl.Buffered(1)),
            pl.BlockSpec((1, heads, dv, s), lambda bi, hi, qi: (bi, hi, 0, 0), pipeline_mode=pl.Buffered(1)),
        ],
        out_specs=pl.BlockSpec((1, tq, heads * MLA_V), lambda bi, hi, qi: (bi, qi, hi)),
        out_shape=jax.ShapeDtypeStruct((b, s, h * MLA_V), BF16),
        scratch_shapes=[pltpu.VMEM((heads, 1, tq), F32), pltpu.VMEM((heads, dv, tq), F32)],
        compiler_params=_params("arbitrary", "arbitrary", "arbitrary"),
        name="mla_attn",
    )(q_t, k, v_t)


def _gla_kernel(q_ref, k_ref, v_ref, r_ref, al_ref, cx_ref, halo_ref, wa_ref, ba_ref, ng_ref, taps_ref,
                o_ref, oc_ref, state_ref, b_scr, q_scr, k_scr, v_scr, oi_scr, *, tc):
    c = GLA_CHUNK
    pair_w = 2 * GLA_DK
    pair_v = 2 * GLA_DV
    n_pairs = GLA_HEADS // 2
    sequence_start = pl.program_id(1) == 0

    @pl.when(sequence_start)
    def _():
        state_ref[...] = jnp.zeros(state_ref.shape, F32)

    oc_ref[...] = _short_conv(cx_ref[...], halo_ref[...], taps_ref[...], sequence_start).astype(oc_ref.dtype)

    row_i = lax.broadcasted_iota(jnp.int32, (c, c), 0)
    col_i = lax.broadcasted_iota(jnp.int32, (c, c), 1)
    tri = jnp.where(row_i >= col_i, 1.0, 0.0).astype(BF16)
    lane_k = lax.broadcasted_iota(jnp.int32, (pair_w, LANES), 0)
    head_sum = [jnp.where((lane_k // GLA_DK) == hh, 1.0, 0.0).astype(BF16) for hh in range(2)]
    sub = GLA_SUBCHUNK
    key_row = lax.broadcasted_iota(jnp.int32, (c, pair_w), 0)
    lane_h = lax.broadcasted_iota(jnp.int32, (1, pair_w), 1) // GLA_DK
    head_lanes = [jnp.where(lane_h == hh, 1.0, 0.0) for hh in range(2)]
    srow = lax.broadcasted_iota(jnp.int32, (pair_w, pair_v), 0) // GLA_DK
    scol = lax.broadcasted_iota(jnp.int32, (pair_w, pair_v), 1) // GLA_DV
    own_head = srow == scol

    def chunk(ci, carry):
        r0 = pl.multiple_of(ci * c, c)
        rows = pl.ds(r0, c)
        x = _dot(al_ref[rows, :].astype(BF16), wa_ref[...]) + ba_ref[...]
        log_a = (jnp.minimum(x, 0.0) - jnp.log(1.0 + jnp.exp(-jnp.abs(x)))) * (1.0 / GLA_TAU)
        p1 = log_a.astype(BF16)
        r1 = log_a - p1.astype(F32)
        p2 = r1.astype(BF16)
        p3 = (r1 - p2.astype(F32)).astype(BF16)
        b_scr[...] = _dot(tri, p1) + _dot(tri, p2) + _dot(tri, p3)
        q_scr[...] = q_ref[rows, :] * (GLA_DK ** -0.5)
        k_scr[...] = k_ref[rows, :]
        v_scr[...] = v_ref[rows, :]

        staged = []
        for p in range(n_pairs):
            kl = slice(p * pair_w, (p + 1) * pair_w)
            vl = slice(p * pair_v, (p + 1) * pair_v)
            bp = b_scr[:, kl]
            qp = q_scr[:, kl]
            kp = k_scr[:, kl]
            vp = v_scr[:, vl]
            b_last = b_scr[c - 1:c, kl]
            st = state_ref[p]
            q_dec = (qp * jnp.exp(bp)).astype(BF16)
            k_dec = (kp * jnp.exp(b_last - bp)).astype(BF16)
            o_inter = _dot(q_dec, st.astype(BF16))
            vp_bf = vp.astype(BF16)
            kv = lax.dot_general(k_dec, vp_bf, (((0,), (0,)), ((), ())), preferred_element_type=F32)
            decay_col = jnp.transpose(jnp.broadcast_to(jnp.exp(b_last), (LANES, pair_w)))
            decay_col = jnp.concatenate([decay_col, decay_col], axis=1)
            state_ref[p] = decay_col * st + jnp.where(own_head, kv, 0.0)
            q_blocks, k_blocks = [], []
            for i in range(1, c // sub):
                r = i * sub
                b_r = b_scr[r:r + 1, kl]
                q_i = q_scr[r:r + sub, kl] * jnp.exp(b_scr[r:r + sub, kl] - b_r)
                pieces = [jnp.zeros((r, pair_w), F32), q_i]
                if c - r - sub:
                    pieces.append(jnp.zeros((c - r - sub, pair_w), F32))
                q_blocks.append(jnp.concatenate(pieces, axis=0))
                k_blocks.append(kp * jnp.exp(jnp.where(key_row < r, b_r - bp, -jnp.inf)))
            q_cat = jnp.concatenate(q_blocks, axis=1)
            k_cat = jnp.concatenate(k_blocks, axis=1).astype(BF16)
            a_far = [_dot_t((q_cat * jnp.concatenate([head_lanes[hh]] * len(q_blocks), axis=1)).astype(BF16), k_cat)
                     for hh in range(2)]
            ys = []
            for i in range(c // sub):
                r = i * sub
                for t0 in (r, r + 8):
                    n = r + sub - t0
                    bt = b_scr[t0:r + sub, kl]
                    qt = q_scr[t0:r + sub, kl]
                    t_idx = t0 + lax.broadcasted_iota(jnp.int32, (n, pair_w), 0)
                    for s in range(t0, t0 + 8):
                        e = jnp.exp(jnp.where(t_idx >= s, bt - b_scr[s:s + 1, kl], -jnp.inf))
                        ys.append(qt * e * k_scr[s:s + 1, kl])
            y = jnp.concatenate(ys, axis=0).astype(BF16)
            a_near = [_dot(y, head_sum[hh]) for hh in range(2)]
            staged.append((o_inter, vp_bf, a_far, a_near))

        for p in range(n_pairs):
            o_inter, vp_bf, a_far, a_near = staged[p]
            for hh in range(2):
                hv = slice(p * pair_v + hh * GLA_DV, p * pair_v + (hh + 1) * GLA_DV)
                o_h = o_inter[:, hh * GLA_DV:(hh + 1) * GLA_DV] + _dot(a_far[hh].astype(BF16),
                                                                       vp_bf[:, hh * GLA_DV:(hh + 1) * GLA_DV])
                a = a_near[hh]
                for i in range(c // sub):
                    r = i * sub
                    base = i * (8 * sub + 8 * 8)
                    upd = a[base:base + sub] * v_scr[r:r + 1, hv]
                    for idx in range(1, 8):
                        upd += a[base + idx * sub:base + (idx + 1) * sub] * v_scr[r + idx:r + idx + 1, hv]
                    base += 8 * sub
                    upd8 = a[base:base + 8] * v_scr[r + 8:r + 9, hv]
                    for idx in range(1, 8):
                        upd8 += a[base + idx * 8:base + (idx + 1) * 8] * v_scr[r + 8 + idx:r + 9 + idx, hv]
                    upd = jnp.concatenate([upd[0:8], upd[8:sub] + upd8], axis=0)
                    oi_scr[r:r + sub, hv] = upd + o_h[r:r + sub]

        o = oi_scr[...]
        gate = r_ref[rows, :]
        gate = gate * jax.nn.sigmoid(gate)
        for h in range(GLA_HEADS):
            hv = slice(h * GLA_DV, (h + 1) * GLA_DV)
            oh = o[:, hv]
            oh = oh * lax.rsqrt(jnp.mean(oh * oh, axis=-1, keepdims=True) + EPS)
            o_ref[rows, hv] = (oh * ng_ref[:, hv] * gate[:, hv]).astype(o_ref.dtype)
        return carry

    lax.fori_loop(0, tc // c, chunk, 0, unroll=4)


def gla_and_conv(z, w_a2, b_a, norm_g, conv_taps, layer, *, batch, seq, tc):
    assert seq % tc == 0 and tc % GLA_CHUNK == 0
    nb = seq // tc
    kw = GLA_HEADS * GLA_DK
    vw = GLA_HEADS * GLA_DV
    cw = 3 * CONV_DIM
    tok = lambda width, col: pl.BlockSpec((tc, width), lambda b, i: (b * nb + i, col // width))
    full = lambda a: _layer(layer, a.shape[1:], lambda b, i: (0, 0))
    c = GLA_CHUNK
    return pl.pallas_call(
        functools.partial(_gla_kernel, tc=tc),
        grid=(batch, nb),
        in_specs=[
            tok(kw, Z_GQ), tok(kw, Z_GK), tok(vw, Z_GV), tok(vw, Z_GR), tok(LANES, Z_AL), tok(cw, Z_CONV),
            pl.BlockSpec((SUBLANES, cw),
                         lambda b, i: (jnp.maximum((b * nb + i) * (tc // SUBLANES) - 1, 0), Z_CONV // cw)),
            full(w_a2), full(b_a), full(norm_g), full(conv_taps),
        ],
        out_specs=[pl.BlockSpec((tc, vw), lambda b, i: (b * nb + i, 0)),
                   pl.BlockSpec((tc, CONV_DIM), lambda b, i: (b * nb + i, 0))],
        out_shape=[jax.ShapeDtypeStruct((batch * seq, vw), BF16),
                   jax.ShapeDtypeStruct((batch * seq, CONV_DIM), BF16)],
        scratch_shapes=[
            pltpu.VMEM((GLA_HEADS // 2, 2 * GLA_DK, 2 * GLA_DV), F32),
            pltpu.VMEM((c, kw), F32), pltpu.VMEM((c, kw), F32), pltpu.VMEM((c, kw), F32),
            pltpu.VMEM((c, vw), F32), pltpu.VMEM((c, vw), F32),
        ],
        compiler_params=_params("arbitrary", "arbitrary"),
        name="gla",
    )(z, z, z, z, z, z, z, w_a2, b_a, norm_g, conv_taps)


def _short_conv(x, halo, w, sequence_start):
    d = CONV_DIM
    z = x[:, d:2 * d] * x[:, 2 * d:3 * d]
    hz = halo[:, d:2 * d] * halo[:, 2 * d:3 * d]
    hz = jnp.where(sequence_start, 0.0, hz)
    row = lax.broadcasted_iota(jnp.int32, z.shape, 0)
    z1 = jnp.where(row == 0, hz[7:8], pltpu.roll(z, 1, 0))
    z2 = jnp.where(row == 0, hz[6:7], jnp.where(row == 1, hz[7:8], pltpu.roll(z, 2, 0)))
    y = w[2:3, :] * z + w[1:2, :] * z1 + w[0:1, :] * z2
    return x[:, 0:d] * y


def _merge_kernel(h_ref, g_ref, a0_ref, a1_ref, a2_ref, wg0_ref, wg1_ref, wg2_ref, b0_ref, b1_ref, b2_ref,
                  wp0_ref, wp1_ref, wp2_ref, wo_ref, o_ref, u_ref):
    j = pl.program_id(1)

    @pl.when(j == 0)
    def _():
        h = h_ref[...]
        u_ref[...] = _rms(h, g_ref[...]).astype(BF16)
        o_ref[...] = h

    u = u_ref[...]
    merged = None
    for a_ref, wg_ref, b_ref, wp_ref in ((a0_ref, wg0_ref, b0_ref, wp0_ref),
                                         (a1_ref, wg1_ref, b1_ref, wp1_ref),
                                         (a2_ref, wg2_ref, b2_ref, wp2_ref)):
        gate = jax.nn.sigmoid(_dot(u, wg_ref[...]) + b_ref[...])
        term = gate * _dot(a_ref[...], wp_ref[...])
        merged = term if merged is None else merged + term
    o_ref[...] += _dot(merged.astype(BF16), wo_ref[...])


def merge(h, g, branches, w_gate, gate_col0, b_gate, w_projs, w_out, layer, *, tm, tn):
    m, d = h.shape
    assert m % tm == 0 and d % tn == 0 and gate_col0 % tn == 0
    nj = d // tn
    g0 = gate_col0 // tn
    row = lambda a: pl.BlockSpec((tm, a.shape[1]), lambda i, j: (i, 0))
    gate_w = lambda b: _layer(layer, (d, tn), lambda i, j: (0, g0 + b * nj + j))
    gate_b = lambda b: _layer(layer, (1, tn), lambda i, j: (0, b * nj + j))
    proj_w = lambda w: _layer(layer, (w.shape[1], tn), lambda i, j: (0, j))
    return pl.pallas_call(
        _merge_kernel,
        grid=(m // tm, nj),
        in_specs=[
            row(h), _vec(layer, d),
            row(branches[0]), row(branches[1]), row(branches[2]),
            gate_w(0), gate_w(1), gate_w(2), gate_b(0), gate_b(1), gate_b(2),
            proj_w(w_projs[0]), proj_w(w_projs[1]), proj_w(w_projs[2]),
            _layer(layer, (tn, d), lambda i, j: (j, 0)),
        ],
        out_specs=pl.BlockSpec((tm, d), lambda i, j: (i, 0)),
        out_shape=jax.ShapeDtypeStruct((m, d), F32),
        scratch_shapes=[pltpu.VMEM((tm, d), BF16)],
        compiler_params=_params("arbitrary", "arbitrary"),
        name="merge",
    )(h, g, *branches, w_gate, w_gate, w_gate, b_gate, b_gate, b_gate, *w_projs, w_out)


def _xattn_kernel(h_ref, g_ref, wq_ref, k_ref, v_ref, wo_ref, o_ref):
    h = h_ref[...]
    hn = _rms(h, g_ref[...]).astype(BF16)
    q = (_dot(hn, wq_ref[...]) * (X_HEAD_DIM ** -0.5)).astype(BF16)
    heads = [slice(j * X_HEAD_DIM, (j + 1) * X_HEAD_DIM) for j in range(X_HEADS)]
    scores = [_dot_t(q[:, hs], k_ref[:, hs]) for hs in heads]
    outs = []
    for hs, s in zip(heads, scores):
        p = jnp.exp(s - jnp.max(s, axis=-1, keepdims=True))
        o = _dot(p.astype(BF16), v_ref[:, hs]) / jnp.sum(p, axis=-1, keepdims=True)
        outs.append(o.astype(BF16))
    o_ref[...] = h + _dot(jnp.concatenate(outs, axis=1), wo_ref[...])


def xattn(h, g, w_q, kv, w_o, layer, *, seq, tm):
    m, d = h.shape
    assert seq % tm == 0
    nb = seq // tm
    once = pl.Buffered(1)
    return pl.pallas_call(
        _xattn_kernel,
        grid=(m // tm,),
        in_specs=[
            pl.BlockSpec((tm, d), lambda i: (i, 0)),
            _vec(layer, d),
            pl.BlockSpec((None, d, d), lambda i: (layer, 0, 0), pipeline_mode=once),
            pl.BlockSpec((MEM_LEN, d), lambda i: (i // nb, 0)),
            pl.BlockSpec((MEM_LEN, d), lambda i: (i // nb, 1)),
            pl.BlockSpec((None, d, d), lambda i: (layer, 0, 0), pipeline_mode=once),
        ],
        out_specs=pl.BlockSpec((tm, d), lambda i: (i, 0)),
        out_shape=jax.ShapeDtypeStruct((m, d), F32),
        compiler_params=_params("arbitrary"),
        name="xattn",
    )(h, g, w_q, kv, kv, w_o)


PACK_BLOCK = 512
_IN_SPLIT_NAMES = ("c_q", "c_kv", "k_rope", "g_q", "g_k", "g_v", "g_r", "a_low", "conv", "gate")
_IN_SPLIT_WIDTHS = (MLA_RANK, MLA_RANK, MLA_ROPE, GLA_HEADS * GLA_DK, GLA_HEADS * GLA_DK, GLA_HEADS * GLA_DV,
                    GLA_HEADS * GLA_DV, GLA_RANK, 3 * CONV_DIM, 3 * D_MODEL)
_SRC = {name: sum(_IN_SPLIT_WIDTHS[:i]) for i, name in enumerate(_IN_SPLIT_NAMES)}
_PACK_SRC = (_SRC["conv"], _SRC["conv"] + PACK_BLOCK, _SRC["conv"] + 2 * PACK_BLOCK, _SRC["c_q"], _SRC["c_kv"],
             _SRC["g_v"], _SRC["g_r"], _SRC["g_q"], _SRC["k_rope"])
_PACK_SPECIAL = len(_PACK_SRC) - 1
_SRC_A_LOW = _SRC["a_low"]
_SRC_GATE = _SRC["gate"]
GATE_COL0 = len(_PACK_SRC) * PACK_BLOCK


def _pack_in_proj_kernel(src_ref, w_t_ref, a_low_ref, o_ref):
    del src_ref
    j = pl.program_id(1)

    @pl.when(j != _PACK_SPECIAL)
    def _():
        o_ref[...] = jnp.transpose(w_t_ref[0]).astype(BF16)

    @pl.when(j == _PACK_SPECIAL)
    def _():
        half = MLA_ROPE // 2
        k_rope = w_t_ref[0, 0:MLA_ROPE, :]
        row = lax.broadcasted_iota(jnp.int32, a_low_ref.shape[1:], 0)
        a_low = jnp.where(row < GLA_RANK, a_low_ref[0], 0.0)
        rest = jnp.zeros((PACK_BLOCK - 2 * LANES, k_rope.shape[1]), F32)
        blk = jnp.concatenate([k_rope, k_rope[half:], k_rope[:half], a_low, rest], axis=0)
        o_ref[...] = jnp.transpose(blk).astype(BF16)


def _pack_in_proj(w):
    depth, d, n = w.shape
    w_t = jnp.swapaxes(w, 1, 2)
    n_gate = n - _SRC_GATE
    assert n_gate % PACK_BLOCK == 0
    rows = _PACK_SRC + tuple(_SRC_GATE + PACK_BLOCK * k for k in range(n_gate // PACK_BLOCK))
    assert all(r % SUBLANES == 0 for r in rows) and _SRC_A_LOW % SUBLANES == 0
    src = jnp.asarray([r // SUBLANES for r in rows], jnp.int32)
    n_blocks = src.shape[0]
    grid_spec = pltpu.PrefetchScalarGridSpec(
        num_scalar_prefetch=1,
        grid=(depth, n_blocks),
        in_specs=[
            pl.BlockSpec((pl.Element(1), pl.Element(PACK_BLOCK), pl.Element(d)),
                         lambda l, j, src: (l, src[j] * SUBLANES, 0)),
            pl.BlockSpec((pl.Element(1), pl.Element(LANES), pl.Element(d)), lambda l, j, src: (l, _SRC_A_LOW, 0)),
        ],
        out_specs=pl.BlockSpec((None, d, PACK_BLOCK), lambda l, j, src: (l, 0, j)),
    )
    return pl.pallas_call(
        _pack_in_proj_kernel,
        grid_spec=grid_spec,
        out_shape=jax.ShapeDtypeStruct((depth, d, n_blocks * PACK_BLOCK), BF16),
        compiler_params=_params("arbitrary", "arbitrary"),
        name="pack_in_proj",
    )(src, w_t, w_t)


def _pack_uq(w):
    depth = w.shape[0]
    w = w.reshape(depth, MLA_RANK, MLA_HEADS, MLA_NOPE + MLA_ROPE)
    half = MLA_ROPE // 2
    nope, pe = w[..., :MLA_NOPE], w[..., MLA_NOPE:]
    swap = jnp.concatenate([pe[..., half:], pe[..., :half]], axis=-1)
    packed = jnp.concatenate([nope, pe, swap], axis=-1).reshape(depth, MLA_RANK, MLA_HEADS * MLA_QK_PAD)
    return jnp.swapaxes(packed, 1, 2).astype(BF16)


def _pack_ukv(w):
    depth = w.shape[0]
    w = w.reshape(depth, MLA_RANK, MLA_HEADS, MLA_NOPE + MLA_V)
    wk = w[..., :MLA_NOPE].reshape(depth, MLA_RANK, MLA_HEADS * MLA_NOPE)
    wv = w[..., MLA_NOPE:].reshape(depth, MLA_RANK, MLA_HEADS * MLA_V)
    return wk.astype(BF16), jnp.swapaxes(wv, 1, 2).astype(BF16)


def _rope_table(seq):
    inv_freq = 1.0 / (ROPE_THETA ** (jnp.arange(0, MLA_ROPE, 2, dtype=F32) / MLA_ROPE))
    ang = jnp.arange(seq, dtype=F32)[:, None] * inv_freq[None, :]
    cos, sin = jnp.cos(ang), jnp.sin(ang)
    return jnp.concatenate([cos, cos, -sin, sin], axis=1)


def kernel(x, mem, ffn1_norm, ffn1_w_in, ffn1_w_out, mix_norm, mix_w_in, mix_b_gate, mla_q_norm, mla_kv_norm,
           mla_w_uq, mla_w_ukv, mla_w_proj, gla_w_a2, gla_b_a, gla_norm, gla_w_proj, conv_w, conv_w_proj,
           mix_w_out, xattn_norm, mem_norm, xattn_w_q, xattn_w_kv, xattn_w_o, ffn2_norm, ffn2_w_in,
           ffn2_w_out, final_norm):
    batch, seq, d = x.shape
    depth = ffn1_norm.shape[0]
    tokens = batch * seq
    table = _rope_table(seq)
    h = x.reshape(tokens, d)
    mem2 = mem.reshape(batch * mem.shape[1], d)
    bf = lambda w: w.astype(BF16)
    vec = lambda p: p.reshape(depth, 1, -1)

    w_mix_in = _pack_in_proj(mix_w_in)
    wq_t = _pack_uq(mla_w_uq)
    wk, wv_t = _pack_ukv(mla_w_ukv)
    w_a2 = bf(jnp.concatenate(
        [gla_w_a2, jnp.zeros((depth, LANES - GLA_RANK, GLA_HEADS * GLA_DK), F32)], axis=1))
    w_projs = (bf(mla_w_proj), bf(gla_w_proj), bf(conv_w_proj))
    w_mix_out, w_xq, w_xo = bf(mix_w_out), bf(xattn_w_q), bf(xattn_w_o)
    conv_taps = conv_w.reshape(depth, 3, CONV_DIM)

    t = TILES
    for l in range(depth):
        h = ffn(h, vec(ffn1_norm), ffn1_w_in, ffn1_w_out, final_norm, l, tm=t.ffn_rows, tf=t.ffn_cols,
                final_norm=False)

        z = norm_matmul(h, vec(mix_norm), w_mix_in, l, n=Z_WIDTH, tm=t.rows, tn=Z_WIDTH, out_dtype=F32)
        q_t, k, v_t = mla_prep(z, table, vec(mla_q_norm), vec(mla_kv_norm), wq_t, wk, wv_t, l,
                               batch=batch, seq=seq, tm=t.prep_rows)
        a_mla = mla_attn(q_t, k, v_t, tq=t.rows, heads=t.attn_heads).reshape(tokens, MLA_HEADS * MLA_V)
        a_gla, a_conv = gla_and_conv(z, w_a2, vec(gla_b_a), vec(gla_norm), conv_taps, l,
                                     batch=batch, seq=seq, tc=t.gla_rows)
        h = merge(h, vec(mix_norm), (a_mla, a_gla, a_conv), w_mix_in, GATE_COL0, vec(mix_b_gate), w_projs,
                  w_mix_out, l, tm=t.rows, tn=t.merge_cols)

        kv = norm_matmul(mem2, vec(mem_norm), xattn_w_kv, l, n=2 * d, tm=mem2.shape[0], tn=t.kv_cols,
                         out_dtype=BF16)
        h = xattn(h, vec(xattn_norm), w_xq, kv, w_xo, l, seq=seq, tm=t.rows)

        h = ffn(h, vec(ffn2_norm), ffn2_w_in, ffn2_w_out, final_norm, l, tm=t.ffn_rows, tf=t.ffn_cols,
                final_norm=(l == depth - 1))
    return h.reshape(batch, seq, d)
```
